```python
import math, functools
import jax, jax.numpy as jnp
from jax import lax
import numpy as np

D_MODEL = 1024
BATCH = 8
SEQ = 2048
DEPTH = 4
DEC_BATCH = 32
DEC_SEQ = 4
PAST_LEN = 16384
PAGE_SIZE = 128

SSD_HEAD_DIM = 64
SSD_INNER = D_MODEL
SSD_HEADS = SSD_INNER // SSD_HEAD_DIM
SSD_GROUPS = 2
SSD_HPG = SSD_HEADS // SSD_GROUPS
SSD_STATE = 128
SSD_CONV = 4
SSD_CHUNK = 128
SSD_CONV_DIM = SSD_INNER + 2 * SSD_GROUPS * SSD_STATE
SSD_IN = SSD_INNER + SSD_CONV_DIM + SSD_HEADS

RWKV_HEAD_DIM = 64
RWKV_DIM = D_MODEL
RWKV_HEADS = RWKV_DIM // RWKV_HEAD_DIM
RWKV_DECAY_LORA = 64
RWKV_A_LORA = 64
RWKV_GATE_LORA = 128
RWKV_IN = 3 * RWKV_DIM + RWKV_DECAY_LORA + RWKV_A_LORA + RWKV_GATE_LORA
RWKV_LN_EPS = 64e-5

MLA_HEADS = 8
MLA_NOPE = 128
MLA_ROPE = 64
MLA_QK = MLA_NOPE + MLA_ROPE
MLA_V = 128
MLA_Q_RANK = 512
MLA_KV_RANK = 256
MLA_IN = MLA_Q_RANK + MLA_KV_RANK + MLA_ROPE
MLA_OUT = MLA_HEADS * MLA_V
ROPE_BASE = 10000.0
Q_BLOCK = 128

IN_DIM = SSD_IN + RWKV_IN + MLA_IN
N_BRANCH = 3

D_FF = 2816
FFN_CONV = 3
NORM_EPS = 1e-6

kernel_name = 'hybrid_ssd_rwkv7_mla_convffn_step'


def rmsnorm(x, g, eps=NORM_EPS):
    xf = x.astype(jnp.float32)
    y = xf * lax.rsqrt(jnp.mean(xf * xf, axis=-1, keepdims=True) + eps)
    return (y * g.astype(jnp.float32)).astype(x.dtype)


def causal_dwconv(u, buf, w, b):
    k_w = w.shape[0]
    t = u.shape[1]
    full = jnp.concatenate([buf.astype(u.dtype), u], axis=1)
    out = full[:, 0:t] * w[0]
    for j in range(1, k_w):
        out = out + full[:, j:j + t] * w[j]
    return out + b, full[:, t:]


def rope(x, pos):
    half = x.shape[-1] // 2
    inv = ROPE_BASE ** (-jnp.arange(half, dtype=jnp.float32) / half)
    ang = pos.astype(jnp.float32)[:, None] * inv[None, :]
    shape = (ang.shape[0],) + (1,) * (x.ndim - 3) + (half,)
    cos = jnp.cos(ang).reshape(shape)
    sin = jnp.sin(ang).reshape(shape)
    xf = x.astype(jnp.float32)
    x1, x2 = xf[..., :half], xf[..., half:]
    return jnp.concatenate([x1 * cos - x2 * sin, x1 * sin + x2 * cos], axis=-1).astype(x.dtype)


def segsum(a):
    n = a.shape[-1]
    cs = jnp.cumsum(a, axis=-1)
    diff = cs[..., :, None] - cs[..., None, :]
    return jnp.where(jnp.tril(jnp.ones((n, n), bool)), diff, -jnp.inf)


def ssd_scan(x, dt, a, bmat, cmat, h0, chunk):
    f32 = jnp.float32
    b, t, g, j, p = x.shape
    n = bmat.shape[-1]
    nc = t // chunk
    xd = (x.astype(f32) * dt[..., None]).reshape(b, nc, chunk, g, j, p)
    da = jnp.moveaxis((dt * a).reshape(b, nc, chunk, g, j), 2, -1)
    bm = bmat.astype(f32).reshape(b, nc, chunk, g, n)
    cm = cmat.astype(f32).reshape(b, nc, chunk, g, n)
    cs = jnp.cumsum(da, axis=-1)
    lmat = jnp.exp(segsum(da))
    cb = jnp.einsum('bclgn,bcsgn->bcgls', cm, bm)
    y_diag = jnp.einsum('bcgjls,bcsgjp->bclgjp', cb[:, :, :, None] * lmat, xd)
    decay_to_end = jnp.exp(cs[..., -1:] - cs)
    states = jnp.einsum('bclgn,bcgjl,bclgjp->bcgjpn', bm, decay_to_end, xd)
    chunk_decay = jnp.exp(cs[..., -1])

    def step(h, inp):
        s, d = inp
        return h * d[..., None, None] + s, h

    h_fin, h_in = lax.scan(step, h0, (jnp.moveaxis(states, 1, 0), jnp.moveaxis(chunk_decay, 1, 0)))
    h_in = jnp.moveaxis(h_in, 0, 1)
    y_off = jnp.einsum('bclgn,bcgjpn,bcgjl->bclgjp', cm, h_in, jnp.exp(cs))
    return (y_diag + y_off).reshape(b, t, g, j, p), h_fin


def ssd_mixer(u, conv_buf, h0, conv_w, conv_b, dt_bias, a_log, d_skip, norm_g):
    f32 = jnp.float32
    b, t, _ = u.shape
    z = u[..., :SSD_INNER]
    xbc, conv_buf = causal_dwconv(u[..., SSD_INNER:SSD_INNER + SSD_CONV_DIM], conv_buf, conv_w, conv_b)
    xbc = jax.nn.silu(xbc)
    gn = SSD_GROUPS * SSD_STATE
    xs = xbc[..., :SSD_INNER].reshape(b, t, SSD_GROUPS, SSD_HPG, SSD_HEAD_DIM)
    bm = xbc[..., SSD_INNER:SSD_INNER + gn].reshape(b, t, SSD_GROUPS, SSD_STATE)
    cm = xbc[..., SSD_INNER + gn:].reshape(b, t, SSD_GROUPS, SSD_STATE)
    dt = jax.nn.softplus(u[..., SSD_INNER + SSD_CONV_DIM:].astype(f32) + dt_bias.astype(f32))
    dt = dt.reshape(b, t, SSD_GROUPS, SSD_HPG)
    a = -jnp.exp(a_log.astype(f32)).reshape(SSD_GROUPS, SSD_HPG)
    chunk = SSD_CHUNK if t % SSD_CHUNK == 0 else t
    h0 = h0.astype(f32).reshape(b, SSD_GROUPS, SSD_HPG, SSD_HEAD_DIM, SSD_STATE)
    y, h = ssd_scan(xs, dt, a, bm, cm, h0, chunk)
    y = y + d_skip.astype(f32).reshape(SSD_GROUPS, SSD_HPG, 1) * xs.astype(f32)
    y = y.reshape(b, t, SSD_INNER) * jax.nn.silu(z.astype(f32))
    y = rmsnorm(y.reshape(b, t, SSD_GROUPS, SSD_INNER // SSD_GROUPS), norm_g.reshape(SSD_GROUPS, -1))
    return (y.reshape(b, t, SSD_INNER).astype(u.dtype), conv_buf,
            h.reshape(b, SSD_HEADS, SSD_HEAD_DIM, SSD_STATE))


def rwkv_mixer(u, shift_buf, s0, mu, w0, w_up, a0, a_up, g_up, k_k, k_a, r_k, ln_g, ln_b):
    f32 = jnp.float32
    b, t, _ = u.shape
    d = RWKV_DIM
    prev = jnp.concatenate([shift_buf[:, None, :].astype(u.dtype), u[:, :-1]], axis=1)
    f = (u + mu * (prev - u)).astype(f32)
    new_buf = u[:, -1]
    r, k, v = f[..., :d], f[..., d:2 * d], f[..., 2 * d:3 * d]
    o1 = 3 * d + RWKV_DECAY_LORA
    o2 = o1 + RWKV_A_LORA
    w_lo, a_lo, g_lo = f[..., 3 * d:o1], f[..., o1:o2], f[..., o2:]
    w_log = -jax.nn.softplus(-(w0 + jnp.tanh(w_lo) @ w_up)) - 0.5
    decay = jnp.exp(-jnp.exp(w_log))
    a = jax.nn.sigmoid(a0 + a_lo @ a_up)
    g = jax.nn.sigmoid(g_lo) @ g_up

    def heads(z):
        return z.reshape(b, t, RWKV_HEADS, RWKV_HEAD_DIM)

    kk = heads(k * k_k)
    kk = kk / jnp.maximum(jnp.sqrt(jnp.sum(kk * kk, axis=-1, keepdims=True)), 1e-12)
    k = k * (1.0 + (a - 1.0) * k_a)
    rh, kh, vh, wh, ah = heads(r), heads(k), heads(v), heads(decay), heads(a)

    def step(s, inp):
        r_t, w_t, k_t, v_t, kk_t, a_t = inp
        sa = jnp.einsum('bhvk,bhk->bhv', s, -kk_t)
        s = (s * w_t[:, :, None, :] + sa[..., None] * (kk_t * a_t)[:, :, None, :]
             + v_t[..., None] * k_t[:, :, None, :])
        return s, jnp.einsum('bhvk,bhk->bhv', s, r_t)

    def tm(z):
        return jnp.moveaxis(z, 1, 0)

    s_fin, y = lax.scan(step, s0.astype(f32), (tm(rh), tm(wh), tm(kh), tm(vh), tm(kk), tm(ah)))
    y = jnp.moveaxis(y, 0, 1)
    mean = jnp.mean(y, axis=-1, keepdims=True)
    var = jnp.mean(jnp.square(y - mean), axis=-1, keepdims=True)
    y = ((y - mean) * lax.rsqrt(var + RWKV_LN_EPS)).reshape(b, t, d) * ln_g + ln_b
    bonus = jnp.sum(rh * kh * r_k, axis=-1, keepdims=True) * vh
    out = (y + bonus.reshape(b, t, d)) * g
    return out.astype(u.dtype), new_buf, s_fin


def mla_queries(cq, pos, q_a_g, w_q_b, q_norm_g):
    q = jnp.einsum('btr,rhe->bthe', rmsnorm(cq, q_a_g), w_q_b)
    q = jnp.concatenate([q[..., :MLA_NOPE], rope(q[..., MLA_NOPE:], pos)], axis=-1)
    return rmsnorm(q, q_norm_g)


def mla_keys(ckv, kpe, w_kb, k_norm_g):
    kn = jnp.einsum('blr,rhe->blhe', ckv, w_kb)
    kp = jnp.broadcast_to(kpe[:, :, None, :], kn.shape[:-1] + (MLA_ROPE,)).astype(kn.dtype)
    return rmsnorm(jnp.concatenate([kn, kp], axis=-1), k_norm_g)


def block_causal_attention(q, k, v):
    b, s, h, e = q.shape
    nb = s // Q_BLOCK
    scale = e ** -0.5
    q_blocks = jnp.moveaxis(q.reshape(b, nb, Q_BLOCK, h, e), 1, 0)
    k_pos = jnp.arange(s)

    def one_block(args):
        qi, start = args
        sc = jnp.einsum('bqhe,bkhe->bhqk', qi, k).astype(jnp.float32) * scale
        q_pos = start + jnp.arange(Q_BLOCK)
        sc = jnp.where(k_pos[None, :] <= q_pos[:, None], sc, -jnp.inf)
        pr = jax.nn.softmax(sc, axis=-1).astype(v.dtype)
        return jnp.einsum('bhqk,bkhv->bqhv', pr, v)

    o = lax.map(one_block, (q_blocks, jnp.arange(nb) * Q_BLOCK))
    return jnp.moveaxis(o, 0, 1).reshape(b, s, h, v.shape[-1])


def prompt_mla(q, ckv, kpe, p):
    k = mla_keys(ckv, kpe, p['mla_w_kb'], p['mla_k_norm_g'])
    v = jnp.einsum('blr,rhv->blhv', ckv, p['mla_w_vb'])
    return block_causal_attention(q, k, v)


def sample_mla(q, ckv, kpe, p, lat_pool, rope_pool, page_table):
    f32 = jnp.float32
    b, t, h, e = q.shape
    scale = e ** -0.5

    def scores(c, kr):
        k = mla_keys(c, kr, p['mla_w_kb'], p['mla_k_norm_g'])
        return jnp.einsum('bqhe,blhe->bhql', q, k).astype(f32) * scale

    def update(carry, sc, c):
        m, l, acc = carry
        m_new = jnp.maximum(m, jnp.max(sc, axis=-1))
        corr = jnp.exp(m - m_new)
        pr = jnp.exp(sc - m_new[..., None])
        l = l * corr + jnp.sum(pr, axis=-1)
        acc = acc * corr[..., None] + jnp.einsum('bhql,blr->bhqr', pr, c.astype(f32))
        return (m_new, l, acc)

    def page_step(carry, phys):
        c = lat_pool[phys]
        kr = rope_pool[phys]
        return update(carry, scores(c, kr), c), None

    init = (jnp.full((b, h, t), -jnp.inf, f32), jnp.zeros((b, h, t), f32),
            jnp.zeros((b, h, t, MLA_KV_RANK), f32))
    carry, _ = lax.scan(page_step, init, page_table.T)
    sc = jnp.where(jnp.tril(jnp.ones((t, t), bool)), scores(ckv, kpe), -jnp.inf)
    m, l, acc = update(carry, sc, ckv)
    o_lat = (acc / l[..., None]).astype(ckv.dtype)
    return jnp.einsum('bhqr,rhv->bqhv', o_lat, p['mla_w_vb'])


def trunk_layer(x, pos, states, attend, p):
    ssd_buf, ssd_h, rwkv_buf, rwkv_s, ffn_buf = states
    b, t, _ = x.shape
    h = rmsnorm(x, p['norm_mix_g'])
    u = h @ p['w_in']
    u_ssd = u[..., :SSD_IN]
    u_rwkv = u[..., SSD_IN:SSD_IN + RWKV_IN]
    u_mla = u[..., SSD_IN + RWKV_IN:]
    o_ssd, ssd_buf, ssd_h = ssd_mixer(u_ssd, ssd_buf, ssd_h, p['ssd_conv_w'], p['ssd_conv_b'],
                                      p['ssd_dt_bias'], p['ssd_a_log'], p['ssd_d'], p['ssd_norm_g'])
    o_rwkv, rwkv_buf, rwkv_s = rwkv_mixer(u_rwkv, rwkv_buf, rwkv_s, p['rwkv_shift_mu'], p['rwkv_w0'],
                                          p['rwkv_w_up'], p['rwkv_a0'], p['rwkv_a_up'], p['rwkv_g_up'],
                                          p['rwkv_k_k'], p['rwkv_k_a'], p['rwkv_r_k'],
                                          p['rwkv_ln_g'], p['rwkv_ln_b'])
    q = mla_queries(u_mla[..., :MLA_Q_RANK], pos, p['mla_q_a_g'], p['mla_w_q_b'], p['mla_q_norm_g'])
    ckv = rmsnorm(u_mla[..., MLA_Q_RANK:MLA_Q_RANK + MLA_KV_RANK], p['mla_kv_a_g'])
    kpe = rope(u_mla[..., MLA_Q_RANK + MLA_KV_RANK:], pos)
    o_mla = attend(q, ckv, kpe, p).reshape(b, t, MLA_OUT)
    gates = jax.nn.sigmoid(h @ p['w_gate'] + p['b_gate']).reshape(b, t, N_BRANCH, D_MODEL)
    merged = (gates[:, :, 0] * (o_ssd @ p['w_o_ssd'])
              + gates[:, :, 1] * (o_rwkv @ p['w_o_rwkv'])
              + gates[:, :, 2] * (o_mla @ p['w_o_mla']))
    x = x + merged @ p['w_out']
    h2 = rmsnorm(x, p['norm_ffn_g'])
    up, ffn_buf = causal_dwconv(h2 @ p['ffn_w_up'], ffn_buf, p['ffn_conv_w'], p['ffn_conv_b'])
    x = x + (jax.nn.silu(up[..., :D_FF]) * up[..., D_FF:]) @ p['ffn_w_down']
    return x, (ssd_buf, ssd_h, rwkv_buf, rwkv_s, ffn_buf), ckv, kpe


def setup_inputs(seed: int = 0) -> dict:
    key = jax.random.key(seed)
    keys = iter(jax.random.split(key, 64))
    f32 = jnp.float32

    def nrm(shape, scale=1.0):
        return jax.random.normal(next(keys), shape, f32) * scale

    def gain(shape):
        return 1.0 + nrm(shape, 0.02)

    def unif(shape, lo, hi):
        return jax.random.uniform(next(keys), shape, f32, lo, hi)

    n_pages = PAST_LEN // PAGE_SIZE
    n_used = DEC_BATCH * n_pages
    n_pool = n_used + max(1, n_used // 4)
    L = DEPTH
    x_prompt = nrm((BATCH, SEQ, D_MODEL))
    x_sample = nrm((DEC_BATCH, DEC_SEQ, D_MODEL))
    cache_kv_latent = nrm((L, n_pool, PAGE_SIZE, MLA_KV_RANK))
    cache_k_rope = nrm((L, n_pool, PAGE_SIZE, MLA_ROPE))
    page_table = jax.random.permutation(next(keys), n_pool)[:n_used].reshape(DEC_BATCH, n_pages).astype(jnp.int32)
    dt_init = jnp.exp(unif((L, SSD_HEADS), math.log(1e-3), math.log(1e-1)))
    return {
        'x_prompt': x_prompt,
        'x_sample': x_sample,
        'cache_kv_latent': cache_kv_latent,
        'cache_k_rope': cache_k_rope,
        'page_table': page_table,
        'state_ssm': nrm((L, DEC_BATCH, SSD_HEADS, SSD_HEAD_DIM, SSD_STATE), 0.5),
        'state_ssm_conv': nrm((L, DEC_BATCH, SSD_CONV - 1, SSD_CONV_DIM)),
        'state_rwkv': nrm((L, DEC_BATCH, RWKV_HEADS, RWKV_HEAD_DIM, RWKV_HEAD_DIM), 0.5),
        'state_rwkv_shift': nrm((L, DEC_BATCH, RWKV_IN)),
        'state_ffn_conv': nrm((L, DEC_BATCH, FFN_CONV - 1, 2 * D_FF)),
        'norm_mix_g': gain((L, D_MODEL)),
        'w_in': nrm((L, D_MODEL, IN_DIM), D_MODEL ** -0.5),
        'ssd_conv_w': nrm((L, SSD_CONV, SSD_CONV_DIM), SSD_CONV ** -0.5),
        'ssd_conv_b': nrm((L, SSD_CONV_DIM), 0.02),
        'ssd_dt_bias': dt_init + jnp.log(-jnp.expm1(-dt_init)),
        'ssd_a_log': jnp.log(unif((L, SSD_HEADS), 1.0, 16.0)),
        'ssd_d': 1.0 + nrm((L, SSD_HEADS), 0.1),
        'ssd_norm_g': gain((L, SSD_INNER)),
        'rwkv_shift_mu': unif((L, RWKV_IN), 0.0, 1.0),
        'rwkv_w0': nrm((L, RWKV_DIM), 0.5) - 1.0,
        'rwkv_w_up': nrm((L, RWKV_DECAY_LORA, RWKV_DIM), 0.5 * RWKV_DECAY_LORA ** -0.5),
        'rwkv_a0': nrm((L, RWKV_DIM), 0.1),
        'rwkv_a_up': nrm((L, RWKV_A_LORA, RWKV_DIM), 0.5 * RWKV_A_LORA ** -0.5),
        'rwkv_g_up': nrm((L, RWKV_GATE_LORA, RWKV_DIM), RWKV_GATE_LORA ** -0.5),
        'rwkv_k_k': 0.85 + nrm((L, RWKV_DIM), 0.05),
        'rwkv_k_a': 1.0 + nrm((L, RWKV_DIM), 0.05),
        'rwkv_r_k': nrm((L, RWKV_HEADS, RWKV_HEAD_DIM), 0.1),
        'rwkv_ln_g': gain((L, RWKV_DIM)),
        'rwkv_ln_b': nrm((L, RWKV_DIM), 0.02),
        'mla_q_a_g': gain((L, MLA_Q_RANK)),
        'mla_w_q_b': nrm((L, MLA_Q_RANK, MLA_HEADS, MLA_QK), MLA_Q_RANK ** -0.5),
        'mla_kv_a_g': gain((L, MLA_KV_RANK)),
        'mla_w_kb': nrm((L, MLA_KV_RANK, MLA_HEADS, MLA_NOPE), MLA_KV_RANK ** -0.5),
        'mla_w_vb': nrm((L, MLA_KV_RANK, MLA_HEADS, MLA_V), MLA_KV_RANK ** -0.5),
        'mla_q_norm_g': gain((L, MLA_QK)),
        'mla_k_norm_g': gain((L, MLA_QK)),
        'w_gate': nrm((L, D_MODEL, N_BRANCH * D_MODEL), D_MODEL ** -0.5),
        'b_gate': nrm((L, N_BRANCH * D_MODEL), 0.02),
        'w_o_ssd': nrm((L, SSD_INNER, D_MODEL), SSD_INNER ** -0.5),
        'w_o_rwkv': nrm((L, RWKV_DIM, D_MODEL), RWKV_DIM ** -0.5),
        'w_o_mla': nrm((L, MLA_OUT, D_MODEL), MLA_OUT ** -0.5),
        'w_out': nrm((L, D_MODEL, D_MODEL), D_MODEL ** -0.5),
        'norm_ffn_g': gain((L, D_MODEL)),
        'ffn_w_up': nrm((L, D_MODEL, 2 * D_FF), D_MODEL ** -0.5),
        'ffn_conv_w': nrm((L, FFN_CONV, 2 * D_FF), FFN_CONV ** -0.5),
        'ffn_conv_b': nrm((L, 2 * D_FF), 0.02),
        'ffn_w_down': nrm((L, D_FF, D_MODEL), D_FF ** -0.5),
    }


def reference(x_prompt, x_sample, cache_kv_latent, cache_k_rope, page_table, state_ssm, state_ssm_conv,
              state_rwkv, state_rwkv_shift, state_ffn_conv, norm_mix_g, w_in, ssd_conv_w, ssd_conv_b,
              ssd_dt_bias, ssd_a_log, ssd_d, ssd_norm_g, rwkv_shift_mu, rwkv_w0, rwkv_w_up, rwkv_a0,
              rwkv_a_up, rwkv_g_up, rwkv_k_k, rwkv_k_a, rwkv_r_k, rwkv_ln_g, rwkv_ln_b, mla_q_a_g,
              mla_w_q_b, mla_kv_a_g, mla_w_kb, mla_w_vb, mla_q_norm_g, mla_k_norm_g, w_gate, b_gate,
              w_o_ssd, w_o_rwkv, w_o_mla, w_out, norm_ffn_g, ffn_w_up, ffn_conv_w, ffn_conv_b, ffn_w_down):
    f32 = jnp.float32
    act = x_prompt.dtype
    b, s, _ = x_prompt.shape
    bd, td, _ = x_sample.shape
    past_len = page_table.shape[1] * PAGE_SIZE
    pos_p = jnp.arange(s)
    pos_s = past_len + jnp.arange(td)
    xp, xs = x_prompt, x_sample
    new_p = [[] for _ in range(7)]
    new_s = [[] for _ in range(7)]
    for i in range(DEPTH):
        p = dict(norm_mix_g=norm_mix_g[i], w_in=w_in[i], ssd_conv_w=ssd_conv_w[i], ssd_conv_b=ssd_conv_b[i],
                 ssd_dt_bias=ssd_dt_bias[i], ssd_a_log=ssd_a_log[i], ssd_d=ssd_d[i], ssd_norm_g=ssd_norm_g[i],
                 rwkv_shift_mu=rwkv_shift_mu[i], rwkv_w0=rwkv_w0[i], rwkv_w_up=rwkv_w_up[i],
                 rwkv_a0=rwkv_a0[i], rwkv_a_up=rwkv_a_up[i], rwkv_g_up=rwkv_g_up[i], rwkv_k_k=rwkv_k_k[i],
                 rwkv_k_a=rwkv_k_a[i], rwkv_r_k=rwkv_r_k[i], rwkv_ln_g=rwkv_ln_g[i], rwkv_ln_b=rwkv_ln_b[i],
                 mla_q_a_g=mla_q_a_g[i], mla_w_q_b=mla_w_q_b[i], mla_kv_a_g=mla_kv_a_g[i],
                 mla_w_kb=mla_w_kb[i], mla_w_vb=mla_w_vb[i], mla_q_norm_g=mla_q_norm_g[i],
                 mla_k_norm_g=mla_k_norm_g[i], w_gate=w_gate[i], b_gate=b_gate[i], w_o_ssd=w_o_ssd[i],
                 w_o_rwkv=w_o_rwkv[i], w_o_mla=w_o_mla[i], w_out=w_out[i], norm_ffn_g=norm_ffn_g[i],
                 ffn_w_up=ffn_w_up[i], ffn_conv_w=ffn_conv_w[i], ffn_conv_b=ffn_conv_b[i],
                 ffn_w_down=ffn_w_down[i])
        zero_state = (jnp.zeros((b, SSD_CONV - 1, SSD_CONV_DIM), act),
                      jnp.zeros((b, SSD_HEADS, SSD_HEAD_DIM, SSD_STATE), f32),
                      jnp.zeros((b, RWKV_IN), act),
                      jnp.zeros((b, RWKV_HEADS, RWKV_HEAD_DIM, RWKV_HEAD_DIM), f32),
                      jnp.zeros((b, FFN_CONV - 1, 2 * D_FF), act))
        xp, st_p, ckv_p, kpe_p = trunk_layer(xp, pos_p, zero_state, prompt_mla, p)
        sample_attend = functools.partial(sample_mla, lat_pool=cache_kv_latent[i],
                                          rope_pool=cache_k_rope[i], page_table=page_table)
        st_in = (state_ssm_conv[i], state_ssm[i], state_rwkv_shift[i], state_rwkv[i], state_ffn_conv[i])
        xs, st_s, ckv_s, kpe_s = trunk_layer(xs, pos_s, st_in, sample_attend, p)
        for lst, vals in ((new_p, (ckv_p, kpe_p, st_p[1], st_p[0], st_p[3], st_p[2], st_p[4])),
                          (new_s, (ckv_s, kpe_s, st_s[1], st_s[0], st_s[3], st_s[2], st_s[4]))):
            for j in range(7):
                lst[j].append(vals[j])
    p_lat, p_rope, p_ssm, p_ssm_conv, p_rwkv, p_rwkv_shift, p_ffn_conv = [jnp.stack(v, axis=0) for v in new_p]
    s_lat, s_rope, s_ssm, s_ssm_conv, s_rwkv, s_rwkv_shift, s_ffn_conv = [jnp.stack(v, axis=0) for v in new_s]
    return (xp, xs, p_lat, p_rope, p_ssm, p_ssm_conv, p_rwkv, p_rwkv_shift, p_ffn_conv,
            s_lat, s_rope, s_ssm, s_ssm_conv, s_rwkv, s_rwkv_shift, s_ffn_conv)
```

```python
import functools
import math

import numpy as np
import jax
import jax.numpy as jnp
from jax import lax
from jax.experimental import pallas as pl
from jax.experimental.pallas import tpu as pltpu

F32 = jnp.float32
BF16 = jnp.bfloat16

D_MODEL = 1024
PAGE_SIZE = 128

SSD_HEAD_DIM = 64
SSD_INNER = 1024
SSD_HEADS = 16
SSD_GROUPS = 2
SSD_STATE = 128
SSD_CONV = 4
SSD_CHUNK = 128
SSD_CONV_DIM = SSD_INNER + 2 * SSD_GROUPS * SSD_STATE
SSD_IN = SSD_INNER + SSD_CONV_DIM + SSD_HEADS
SSD_IN_PAD = SSD_INNER + SSD_CONV_DIM + 128

RWKV_HEAD_DIM = 64
RWKV_DIM = 1024
RWKV_HEADS = 16
RWKV_DECAY_LORA = 64
RWKV_A_LORA = 64
RWKV_GATE_LORA = 128
RWKV_IN = 3 * RWKV_DIM + RWKV_DECAY_LORA + RWKV_A_LORA + RWKV_GATE_LORA
RWKV_LN_EPS = 64e-5

MLA_HEADS = 8
MLA_NOPE = 128
MLA_ROPE = 64
MLA_QK = MLA_NOPE + MLA_ROPE
MLA_V = 128
MLA_Q_RANK = 512
MLA_KV_RANK = 256
MLA_IN = MLA_Q_RANK + MLA_KV_RANK + MLA_ROPE
MLA_IN_PAD = MLA_Q_RANK + MLA_KV_RANK + 128
MLA_QK_PAD = 256
ROPE_BASE = 10000.0
MLA_SCALE = MLA_QK ** -0.5

N_BRANCH = 3
D_FF = 2816
FFN_CONV = 3
NORM_EPS = 1e-6

SUBLANES = 8
LANES = 128
SAMPLE_T_PAD = 8
NEG_BIG = -1e30

VMEM_LIMIT = 56 * 1024 * 1024


def _cparams(*sem):
    return pltpu.CompilerParams(dimension_semantics=sem, vmem_limit_bytes=VMEM_LIMIT)


def _full(shape):
    nd = len(shape)
    return pl.BlockSpec(shape, lambda *_: (0,) * nd)


def _rms(x, eps=NORM_EPS):
    return x * lax.rsqrt(jnp.mean(x * x, axis=-1, keepdims=True) + eps)


def _sigmoid(x):
    return 1.0 / (1.0 + jnp.exp(-x))


def _silu(x):
    return x * _sigmoid(x)


def _softplus(x):
    return jnp.maximum(x, 0.0) + jnp.log(1.0 + jnp.exp(-jnp.abs(x)))


def _dot(a, b):
    return jnp.dot(a, b, preferred_element_type=F32)


def _dot_nt(a, b):
    return lax.dot_general(a, b, (((1,), (1,)), ((), ())), preferred_element_type=F32)


def _dot_tn(a, b):
    return lax.dot_general(a, b, (((0,), (0,)), ((), ())), preferred_element_type=F32)


def _split_hi_lo(x):
    hi = x.astype(BF16)
    lo = (x - hi.astype(F32)).astype(BF16)
    return jnp.concatenate([hi, lo], axis=1)


def _norm_matmul_kernel(x_ref, g_ref, w_ref, b_ref, o_ref):
    h = (_rms(x_ref[...]) * g_ref[...]).astype(BF16)
    o_ref[...] = _dot(h, w_ref[...]) + b_ref[...]


def _norm_matmul(x, g, w, bias, tm):
    n, d = x.shape
    nout = w.shape[1]
    return pl.pallas_call(
        _norm_matmul_kernel,
        grid=(n // tm,),
        in_specs=[pl.BlockSpec((tm, d), lambda i: (i, 0)), _full((1, d)),
                  _full((d, nout)), _full((1, nout))],
        out_specs=pl.BlockSpec((tm, nout), lambda i: (i, 0)),
        out_shape=jax.ShapeDtypeStruct((n, nout), F32),
        compiler_params=_cparams("parallel"),
    )(x, g, w, bias)


def _ssd_kernel(u_ref, tail0_ref, h0_ref, cw_ref, cb_ref, dtb_ref, alog_ref, dsk_ref, ng_ref,
                o_ref, tail_ref, h_ref, buf, *, t_valid):
    L = SSD_CHUNK
    c = pl.program_id(1)

    @pl.when(c == 0)
    def _():
        buf[0:SUBLANES, :] = tail0_ref[...]
        h_ref[...] = h0_ref[...]

    u = u_ref[...]
    z = u[:, :SSD_INNER]
    dt_raw = u[:, SSD_INNER + SSD_CONV_DIM:]
    buf[SUBLANES:SUBLANES + L, :] = u[:, SSD_INNER:SSD_INNER + SSD_CONV_DIM]
    acc = cb_ref[...]
    for j in range(SSD_CONV):
        acc = acc + buf[pl.ds(SUBLANES - (SSD_CONV - 1) + j, L), :] * cw_ref[j:j + 1, :]
    tail = buf[pl.ds(t_valid, SUBLANES), :]
    tail_ref[...] = tail
    buf[0:SUBLANES, :] = tail
    xbc = _silu(acc)
    xs = xbc[:, :SSD_INNER]

    dt = _softplus(dt_raw + dtb_ref[...])
    if t_valid < L:
        row = lax.broadcasted_iota(jnp.int32, dt.shape, 0)
        dt = jnp.where(row < t_valid, dt, 0.0)
    a = -jnp.exp(alog_ref[...])
    da = dt * a
    ri = lax.broadcasted_iota(jnp.int32, (L, L), 0)
    ci = lax.broadcasted_iota(jnp.int32, (L, L), 1)
    causal = ri >= ci
    tri = jnp.where(causal, 1.0, 0.0).astype(F32)
    cs = jnp.dot(tri, da, preferred_element_type=F32, precision=lax.Precision.HIGHEST)
    cs_t = cs.T
    total = cs[L - 1:L, :]
    etot = jnp.exp(total)
    lane = lax.broadcasted_iota(jnp.int32, (L, LANES), 1)
    lo_half = lane < SSD_HEAD_DIM
    row_lo = lax.broadcasted_iota(jnp.int32, (LANES, LANES), 0) < SSD_HEAD_DIM

    def pair_cols(m, j0):
        return jnp.where(lo_half, m[:, j0:j0 + 1], m[:, j0 + 1:j0 + 2])

    hpg = SSD_HEADS // SSD_GROUPS
    y_parts = []
    for g in range(SSD_GROUPS):
        b_g = xbc[:, SSD_INNER + g * SSD_STATE:SSD_INNER + (g + 1) * SSD_STATE].astype(BF16)
        c0 = SSD_INNER + SSD_GROUPS * SSD_STATE + g * SSD_STATE
        c_g = xbc[:, c0:c0 + SSD_STATE].astype(BF16)
        cb = _dot_nt(c_g, b_g)
        for m in range(hpg // 2):
            q = g * (hpg // 2) + m
            j0 = 2 * q
            xs_p = xs[:, q * LANES:(q + 1) * LANES]
            xd = xs_p * pair_cols(dt, j0)
            xd_b = xd.astype(BF16)
            ys = []
            for e in range(2):
                j = j0 + e
                diff = cs[:, j:j + 1] - cs_t[j:j + 1, :]
                lm = jnp.where(causal, jnp.exp(jnp.where(causal, diff, 0.0)), 0.0)
                ys.append(_dot((cb * lm).astype(BF16), xd_b))
            y_diag = jnp.where(lo_half, ys[0], ys[1])
            hp = h_ref[q]
            cs_p = pair_cols(cs, j0)
            y_off = _dot_nt(c_g, hp.astype(BF16)) * jnp.exp(cs_p)
            tot_p = jnp.where(lo_half[0:1, :], total[:, j0:j0 + 1], total[:, j0 + 1:j0 + 2])
            dte = jnp.exp(tot_p - cs_p)
            contrib = _dot_tn((xd * dte).astype(BF16), b_g)
            decay = jnp.where(row_lo, etot[:, j0:j0 + 1], etot[:, j0 + 1:j0 + 2])
            h_ref[q] = hp * decay + contrib
            y_parts.append(y_diag + y_off + dsk_ref[:, q * LANES:(q + 1) * LANES] * xs_p)
    y = jnp.concatenate(y_parts, axis=1) * _silu(z)
    gw = SSD_INNER // SSD_GROUPS
    outs = [_rms(y[:, g * gw:(g + 1) * gw]) * ng_ref[:, g * gw:(g + 1) * gw] for g in range(SSD_GROUPS)]
    o_ref[...] = jnp.concatenate(outs, axis=1).astype(BF16)


def _ssd(u, tail0, h0, p, t_valid):
    bsz = h0.shape[0]
    n = u.shape[0]
    L = SSD_CHUNK
    nch = n // bsz // L
    kern = functools.partial(_ssd_kernel, t_valid=t_valid)
    return pl.pallas_call(
        kern,
        grid=(bsz, nch),
        in_specs=[pl.BlockSpec((L, SSD_IN_PAD), lambda b, c: (b * nch + c, 0)),
                  pl.BlockSpec((None, SUBLANES, SSD_CONV_DIM), lambda b, c: (b, 0, 0)),
                  pl.BlockSpec((None, SSD_HEADS // 2, LANES, SSD_STATE), lambda b, c: (b, 0, 0, 0)),
                  _full((SSD_CONV, SSD_CONV_DIM)), _full((1, SSD_CONV_DIM)),
                  _full((1, LANES)), _full((1, LANES)), _full((1, SSD_INNER)), _full((1, SSD_INNER))],
        out_specs=[pl.BlockSpec((L, SSD_INNER), lambda b, c: (b * nch + c, 0)),
                   pl.BlockSpec((None, SUBLANES, SSD_CONV_DIM), lambda b, c: (b, 0, 0)),
                   pl.BlockSpec((None, SSD_HEADS // 2, LANES, SSD_STATE), lambda b, c: (b, 0, 0, 0))],
        out_shape=[jax.ShapeDtypeStruct((n, SSD_INNER), BF16),
                   jax.ShapeDtypeStruct((bsz, SUBLANES, SSD_CONV_DIM), F32),
                   jax.ShapeDtypeStruct((bsz, SSD_HEADS // 2, LANES, SSD_STATE), F32)],
        scratch_shapes=[pltpu.VMEM((SUBLANES + L, SSD_CONV_DIM), F32)],
        compiler_params=_cparams("parallel", "arbitrary"),
    )(u, tail0, h0, p["ssd_conv_w"], p["ssd_conv_b"], p["ssd_dt_bias"], p["ssd_a_log"],
      p["ssd_d"], p["ssd_norm_g"])


def _seg_sum_bcast(x, seg2, seg_t2):
    s16 = _dot(_split_hi_lo(x), seg2)
    return _dot(_split_hi_lo(s16), seg_t2)


def _rwkv_pre_kernel(u_ref, tail0_ref, mu_ref, w0_ref, wup_ref, a0_ref, aup_ref, gup_ref, kk_ref,
                     ka_ref, seg2_ref, segt2_ref,
                     r_ref, w_ref, k_ref, v_ref, al_ref, be_ref, g_ref, tail_ref, buf, *, tm, t_valid):
    j = pl.program_id(1)

    @pl.when(j == 0)
    def _():
        buf[0:SUBLANES, :] = tail0_ref[...]

    u = u_ref[...]
    buf[SUBLANES:SUBLANES + tm, :] = u
    prev = buf[pl.ds(SUBLANES - 1, tm), :]
    tail = buf[pl.ds(t_valid, SUBLANES), :]
    tail_ref[...] = tail
    buf[0:SUBLANES, :] = tail

    d = RWKV_DIM
    f = u + mu_ref[...] * (prev - u)
    r, k, v = f[:, :d], f[:, d:2 * d], f[:, 2 * d:3 * d]
    lo = f[:, 3 * d:3 * d + LANES]
    glo = f[:, 3 * d + LANES:]
    ww = w0_ref[...] + _dot(jnp.tanh(lo).astype(BF16), wup_ref[...])
    w_log = -_softplus(-ww) - 0.5
    a = _sigmoid(a0_ref[...] + _dot(lo.astype(BF16), aup_ref[...]))
    kk = k * kk_ref[...]
    ssb = _seg_sum_bcast(kk * kk, seg2_ref[...], segt2_ref[...])
    kkn = kk / jnp.maximum(jnp.sqrt(ssb), 1e-12)
    r_ref[...] = r
    w_ref[...] = jnp.exp(-jnp.exp(w_log))
    k_ref[...] = k * (1.0 + (a - 1.0) * ka_ref[...])
    v_ref[...] = v
    al_ref[...] = -kkn
    be_ref[...] = kkn * a
    g_ref[...] = _dot(_sigmoid(glo).astype(BF16), gup_ref[...])


def _rwkv_pre(u, tail0, p, tm, t_valid):
    bsz = tail0.shape[0]
    n = u.shape[0]
    nt = n // bsz // tm
    d = RWKV_DIM
    kern = functools.partial(_rwkv_pre_kernel, tm=tm, t_valid=t_valid)
    row = pl.BlockSpec((tm, d), lambda b, j: (b * nt + j, 0))
    return pl.pallas_call(
        kern,
        grid=(bsz, nt),
        in_specs=[pl.BlockSpec((tm, RWKV_IN), lambda b, j: (b * nt + j, 0)),
                  pl.BlockSpec((None, SUBLANES, RWKV_IN), lambda b, j: (b, 0, 0)),
                  _full((1, RWKV_IN)), _full((1, d)), _full((LANES, d)), _full((1, d)),
                  _full((LANES, d)), _full((LANES, d)), _full((1, d)), _full((1, d)),
                  _full((2 * d, LANES)), _full((2 * LANES, d))],
        out_specs=[row] * 7 + [pl.BlockSpec((None, SUBLANES, RWKV_IN), lambda b, j: (b, 0, 0))],
        out_shape=[jax.ShapeDtypeStruct((n, d), F32)] * 7
                  + [jax.ShapeDtypeStruct((bsz, SUBLANES, RWKV_IN), F32)],
        scratch_shapes=[pltpu.VMEM((SUBLANES + tm, RWKV_IN), F32)],
        compiler_params=_cparams("parallel", "arbitrary"),
    )(u, tail0, p["rwkv_shift_mu"], p["rwkv_w0"], p["rwkv_w_up"], p["rwkv_a0"], p["rwkv_a_up"],
      p["rwkv_g_up"], p["rwkv_k_k"], p["rwkv_k_a"], p["seg2"], p["seg_t2"])


RWKV_WINDOW = 64
RWKV_ROWS = RWKV_HEADS // 2 * RWKV_HEAD_DIM


def _rwkv_scan_kernel(r_ref, w_ref, k_ref, al_ref, be_ref, vt_ref, s0_ref, e2_ref,
                      y_ref, s_ref, vthl, yacc, *, bb, t_steps):
    c = pl.program_id(1)

    @pl.when(c == 0)
    def _():
        s_ref[...] = s0_ref[...]

    for b in range(bb):
        vthl[b] = _split_hi_lo(vt_ref[b])
    e2 = e2_ref[...]
    ri = lax.broadcasted_iota(jnp.int32, (LANES, LANES), 0)
    ci = lax.broadcasted_iota(jnp.int32, (LANES, LANES), 1)
    same_head = (ri // RWKV_HEAD_DIM) == (ci // RWKV_HEAD_DIM)
    lo_half = lax.broadcasted_iota(jnp.int32, (RWKV_HEAD_DIM, LANES), 1) < RWKV_HEAD_DIM
    npair = RWKV_HEADS // 2

    def row_tile(ref, b, t):
        x = ref[b, pl.ds(t, 1), :]
        return jnp.concatenate(
            [jnp.broadcast_to(x[:, p * LANES:(p + 1) * LANES], (RWKV_HEAD_DIM, LANES))
             for p in range(npair)], axis=0)

    def step(t, base):
        sel = jnp.where(ri == t, 1.0, 0.0).astype(BF16)
        sel2 = jnp.concatenate([sel, sel], axis=0)
        ey = jnp.where(same_head & ((ci % RWKV_HEAD_DIM) == (t - base)), 1.0, 0.0).astype(BF16)
        for b in range(bb):
            s = s_ref[b]
            sa = _dot(_split_hi_lo(s * row_tile(al_ref, b, t)), e2)
            vb = _dot(vthl[b], sel2)
            vcol = jnp.concatenate(
                [jnp.where(lo_half, vb[p * LANES:p * LANES + RWKV_HEAD_DIM],
                           vb[p * LANES + RWKV_HEAD_DIM:(p + 1) * LANES]) for p in range(npair)], axis=0)
            s_new = s * row_tile(w_ref, b, t) + sa * row_tile(be_ref, b, t) + vcol * row_tile(k_ref, b, t)
            s_ref[b] = s_new
            yacc[b] += _dot((s_new * row_tile(r_ref, b, t)).astype(BF16), ey)

    nwin = -(-t_steps // RWKV_WINDOW)
    for win in range(nwin):
        base = win * RWKV_WINDOW
        yacc[...] = jnp.zeros_like(yacc)
        lax.fori_loop(base, min(base + RWKV_WINDOW, t_steps),
                      lambda t, carry, base=base: (step(t, base), carry)[1], 0)
        y_ref[:, win] = yacc[...]


def _rwkv_scan(r, w, k, al, be, vt, s0, e2, bb, tc, t_steps):
    bsz, t_total, d = r.shape
    nchunk = t_total // tc
    nwin = -(-t_steps // RWKV_WINDOW)
    kern = functools.partial(_rwkv_scan_kernel, bb=bb, t_steps=t_steps)
    row = pl.BlockSpec((bb, tc, d), lambda g, c: (g, c, 0))
    st = pl.BlockSpec((bb, RWKV_ROWS, LANES), lambda g, c: (g, 0, 0))
    return pl.pallas_call(
        kern,
        grid=(bsz // bb, nchunk),
        in_specs=[row] * 5 + [pl.BlockSpec((bb, d, LANES), lambda g, c: (g, 0, c)), st,
                              _full((2 * LANES, LANES))],
        out_specs=[pl.BlockSpec((bb, nwin, RWKV_ROWS, LANES), lambda g, c: (g, c, 0, 0)), st],
        out_shape=[jax.ShapeDtypeStruct((bsz, nchunk * nwin, RWKV_ROWS, LANES), F32),
                   jax.ShapeDtypeStruct((bsz, RWKV_ROWS, LANES), F32)],
        scratch_shapes=[pltpu.VMEM((bb, d, 2 * LANES), BF16), pltpu.VMEM((bb, RWKV_ROWS, LANES), F32)],
        compiler_params=_cparams("parallel", "arbitrary"),
    )(r, w, k, al, be, vt, s0, e2)


def _rwkv_post_kernel(y_ref, r_ref, k_ref, v_ref, g_ref, lng_ref, lnb_ref, rk_ref, seg2_ref, segt2_ref, o_ref):
    seg2, seg_t2 = seg2_ref[...], segt2_ref[...]
    y = y_ref[...]
    inv = 1.0 / RWKV_HEAD_DIM
    dlt = y - _seg_sum_bcast(y, seg2, seg_t2) * inv
    var = _seg_sum_bcast(dlt * dlt, seg2, seg_t2) * inv
    yn = dlt * lax.rsqrt(var + RWKV_LN_EPS) * lng_ref[...] + lnb_ref[...]
    bonus = _seg_sum_bcast(r_ref[...] * k_ref[...] * rk_ref[...], seg2, seg_t2) * v_ref[...]
    o_ref[...] = ((yn + bonus) * g_ref[...]).astype(BF16)


def _rwkv_post(y, r, k, v, g, p, tm):
    n, d = y.shape
    row = pl.BlockSpec((tm, d), lambda i: (i, 0))
    return pl.pallas_call(
        _rwkv_post_kernel,
        grid=(n // tm,),
        in_specs=[row] * 5 + [_full((1, d))] * 3 + [_full((2 * d, LANES)), _full((2 * LANES, d))],
        out_specs=row,
        out_shape=jax.ShapeDtypeStruct((n, d), BF16),
        compiler_params=_cparams("parallel"),
    )(y, r, k, v, g, p["rwkv_ln_g"], p["rwkv_ln_b"], p["rwkv_r_k"], p["seg2"], p["seg_t2"])


def _mla_prep_kernel(u_ref, cos_ref, s1_ref, s2_ref, qag_ref, wqn_ref, wqr_ref, kvg_ref, wkb_ref, wvb_ref,
                     gqn_ref, gqr_ref, gkn_ref, gkr_ref, q_ref, k_ref, v_ref, ckv_ref, kpe_ref):
    u = u_ref[...]
    cq = u[:, :MLA_Q_RANK]
    ckv_raw = u[:, MLA_Q_RANK:MLA_Q_RANK + MLA_KV_RANK]
    kr_raw = u[:, MLA_Q_RANK + MLA_KV_RANK:]
    cos, s1, s2 = cos_ref[...], s1_ref[...], s2_ref[...]

    def rope(blk):
        return (blk * cos + pltpu.roll(blk, LANES - MLA_ROPE // 2, 1) * s1
                + pltpu.roll(blk, MLA_ROPE // 2, 1) * s2)

    cqn = (_rms(cq) * qag_ref[...]).astype(BF16)
    qn = _dot(cqn, wqn_ref[...])
    qr = _dot(cqn, wqr_ref[...])
    ckv = _rms(ckv_raw) * kvg_ref[...]
    ckv_ref[...] = ckv
    kpe = rope(kr_raw)
    kpe_ref[...] = kpe
    ckb = ckv.astype(BF16)
    kn = _dot(ckb, wkb_ref[...])
    v_ref[...] = _dot(ckb, wvb_ref[...]).astype(BF16)
    ss_kpe = jnp.sum(kpe * kpe, axis=-1, keepdims=True)
    gqn, gqr, gkn, gkr = gqn_ref[...], gqr_ref[...], gkn_ref[...], gkr_ref[...]
    for h in range(MLA_HEADS):
        sl = slice(h * MLA_NOPE, (h + 1) * MLA_NOPE)
        o0 = h * MLA_QK_PAD
        qn_h = qn[:, sl]
        qr_h = rope(qr[:, sl])
        ss = jnp.sum(qn_h * qn_h, axis=-1, keepdims=True) + jnp.sum(qr_h * qr_h, axis=-1, keepdims=True)
        rn = lax.rsqrt(ss * (1.0 / MLA_QK) + NORM_EPS) * MLA_SCALE
        q_ref[:, o0:o0 + MLA_NOPE] = (qn_h * rn * gqn).astype(q_ref.dtype)
        q_ref[:, o0 + MLA_NOPE:o0 + MLA_QK_PAD] = (qr_h * rn * gqr).astype(q_ref.dtype)
        kn_h = kn[:, sl]
        ssk = jnp.sum(kn_h * kn_h, axis=-1, keepdims=True) + ss_kpe
        rnk = lax.rsqrt(ssk * (1.0 / MLA_QK) + NORM_EPS)
        k_ref[:, o0:o0 + MLA_NOPE] = (kn_h * rnk * gkn).astype(BF16)
        k_ref[:, o0 + MLA_NOPE:o0 + MLA_QK_PAD] = (kpe * rnk * gkr).astype(BF16)


def _mla_prep(u, rope_tabs, p, tm, q_dtype):
    n = u.shape[0]
    ttab = rope_tabs[0].shape[0]
    ntab = ttab // tm
    hq = MLA_HEADS * MLA_QK_PAD
    hv = MLA_HEADS * MLA_V
    tab = pl.BlockSpec((tm, LANES), lambda i: (i % ntab, 0))
    return pl.pallas_call(
        _mla_prep_kernel,
        grid=(n // tm,),
        in_specs=[pl.BlockSpec((tm, MLA_IN_PAD), lambda i: (i, 0)), tab, tab, tab,
                  _full((1, MLA_Q_RANK)), _full((MLA_Q_RANK, hv)), _full((MLA_Q_RANK, hv)),
                  _full((1, MLA_KV_RANK)), _full((MLA_KV_RANK, hv)), _full((MLA_KV_RANK, hv)),
                  _full((1, LANES)), _full((1, LANES)), _full((1, LANES)), _full((1, LANES))],
        out_specs=[pl.BlockSpec((tm, hq), lambda i: (i, 0)), pl.BlockSpec((tm, hq), lambda i: (i, 0)),
                   pl.BlockSpec((tm, hv), lambda i: (i, 0)),
                   pl.BlockSpec((tm, MLA_KV_RANK), lambda i: (i, 0)),
                   pl.BlockSpec((tm, LANES), lambda i: (i, 0))],
        out_shape=[jax.ShapeDtypeStruct((n, hq), q_dtype), jax.ShapeDtypeStruct((n, hq), BF16),
                   jax.ShapeDtypeStruct((n, hv), BF16), jax.ShapeDtypeStruct((n, MLA_KV_RANK), F32),
                   jax.ShapeDtypeStruct((n, LANES), F32)],
        compiler_params=_cparams("parallel"),
    )(u, *rope_tabs, p["mla_q_a_g"], p["wq_nope"], p["wq_rope"], p["mla_kv_a_g"], p["w_kb"], p["w_vb"],
      p["gq_nope"], p["gq_rope"], p["gk_nope"], p["gk_rope"])


def _flash_kernel(q_ref, k_ref, v_ref, o_ref, m_sc, l_sc, acc_sc, *, tq):
    qi = pl.program_id(2)
    ki = pl.program_id(3)

    @pl.when(ki == 0)
    def _():
        m_sc[...] = jnp.full_like(m_sc, NEG_BIG)
        l_sc[...] = jnp.zeros_like(l_sc)
        acc_sc[...] = jnp.zeros_like(acc_sc)

    @pl.when(ki <= qi)
    def _():
        s = _dot_nt(q_ref[...], k_ref[...])
        qpos = qi * tq + lax.broadcasted_iota(jnp.int32, s.shape, 0)
        kpos = ki * tq + lax.broadcasted_iota(jnp.int32, s.shape, 1)
        s = jnp.where(kpos <= qpos, s, NEG_BIG)
        m_prev = m_sc[...]
        m_new = jnp.maximum(m_prev, jnp.max(s, axis=-1, keepdims=True))
        pr = jnp.exp(s - m_new)
        corr = jnp.exp(m_prev - m_new)
        l_sc[...] = l_sc[...] * corr + jnp.sum(pr, axis=-1, keepdims=True)
        acc_sc[...] = acc_sc[...] * corr + _dot(pr.astype(BF16), v_ref[...])
        m_sc[...] = m_new

    @pl.when(ki == pl.num_programs(3) - 1)
    def _():
        o_ref[...] = (acc_sc[...] / l_sc[...]).astype(BF16)


def _flash(q, k, v, bsz, tq):
    n = q.shape[0]
    nq = n // bsz // tq
    kern = functools.partial(_flash_kernel, tq=tq)
    return pl.pallas_call(
        kern,
        grid=(bsz, MLA_HEADS, nq, nq),
        in_specs=[pl.BlockSpec((tq, MLA_QK_PAD), lambda b, h, i, j: (b * nq + i, h)),
                  pl.BlockSpec((tq, MLA_QK_PAD), lambda b, h, i, j: (b * nq + jnp.minimum(i, j), h)),
                  pl.BlockSpec((tq, MLA_V), lambda b, h, i, j: (b * nq + jnp.minimum(i, j), h))],
        out_specs=pl.BlockSpec((tq, MLA_V), lambda b, h, i, j: (b * nq + i, h)),
        out_shape=jax.ShapeDtypeStruct((n, MLA_HEADS * MLA_V), BF16),
        scratch_shapes=[pltpu.VMEM((tq, 1), F32), pltpu.VMEM((tq, 1), F32), pltpu.VMEM((tq, MLA_V), F32)],
        compiler_params=_cparams("parallel", "parallel", "parallel", "arbitrary"),
    )(q, k, v)


PAGES_PER_STEP = 8


def _paged_kernel(pt_ref, q_ref, *refs, t_new):
    npp = PAGES_PER_STEP
    lat_refs = refs[:npp]
    rope_refs = refs[npp:2 * npp]
    (newc_ref, newr_ref, wkb_ref, wvb_ref, gkn_ref, gkr_ref, o_ref,
     m_sc, l_sc, acc_sc, kr_sc) = refs[2 * npp:]
    step = pl.program_id(1)

    @pl.when(step == 0)
    def _():
        m_sc[...] = jnp.full_like(m_sc, NEG_BIG)
        l_sc[...] = jnp.zeros_like(l_sc)
        acc_sc[...] = jnp.zeros_like(acc_sc)
        kr_sc[...] = jnp.zeros_like(kr_sc)

    q = q_ref[...].astype(BF16)
    wkb = wkb_ref[...]
    gkn, gkr = gkn_ref[...], gkr_ref[...]

    def process(c, kr128, new_tokens):
        cb = c.astype(BF16)
        kn = _dot(cb, wkb)
        ssr = jnp.sum(kr128 * kr128, axis=-1, keepdims=True)
        s_list = []
        for h in range(MLA_HEADS):
            kn_h = kn[:, h * MLA_NOPE:(h + 1) * MLA_NOPE]
            rn = lax.rsqrt((jnp.sum(kn_h * kn_h, axis=-1, keepdims=True) + ssr) * (1.0 / MLA_QK) + NORM_EPS)
            k_h = jnp.concatenate([kn_h * rn * gkn, kr128 * rn * gkr], axis=1).astype(BF16)
            s_list.append(_dot_nt(q[:, h * MLA_QK_PAD:(h + 1) * MLA_QK_PAD], k_h))
        s = jnp.concatenate(s_list, axis=0)
        if new_tokens:
            key = lax.broadcasted_iota(jnp.int32, s.shape, 1)
            qry = lax.broadcasted_iota(jnp.int32, s.shape, 0) % SAMPLE_T_PAD
            s = jnp.where((key <= qry) & (key < t_new), s, NEG_BIG)
        m_prev = m_sc[...]
        m_new = jnp.maximum(m_prev, jnp.max(s, axis=-1, keepdims=True))
        pr = jnp.exp(s - m_new)
        corr = jnp.exp(m_prev - m_new)
        l_sc[...] = l_sc[...] * corr + jnp.sum(pr, axis=-1, keepdims=True)
        acc_sc[...] = acc_sc[...] * corr + _dot(pr.astype(BF16), cb)
        m_sc[...] = m_new

    for i in range(npp):
        kr_sc[:, 0:MLA_ROPE] = rope_refs[i][...]
        process(lat_refs[i][...], kr_sc[...], False)

    @pl.when(step == pl.num_programs(1) - 1)
    def _():
        process(newc_ref[...], newr_ref[...], True)
        o_lat = (acc_sc[...] / l_sc[...]).astype(BF16)
        wvb = wvb_ref[...]
        for h in range(MLA_HEADS):
            o_ref[:, h * MLA_V:(h + 1) * MLA_V] = _dot(
                o_lat[h * SAMPLE_T_PAD:(h + 1) * SAMPLE_T_PAD],
                wvb[:, h * MLA_V:(h + 1) * MLA_V])


def _paged_attention(page_table, q, lat_pool, rope_pool, layer, new_c, new_r, p, t_new):
    bsz, n_pages = page_table.shape
    npp = PAGES_PER_STEP
    nsteps = n_pages // npp
    hq = MLA_HEADS * MLA_QK_PAD
    hv = MLA_HEADS * MLA_V

    def lat_spec(i):
        return pl.BlockSpec((None, None, PAGE_SIZE, MLA_KV_RANK),
                            lambda b, s, pt: (layer, pt[b, s * npp + i], 0, 0))

    def rope_spec(i):
        return pl.BlockSpec((None, None, PAGE_SIZE, MLA_ROPE),
                            lambda b, s, pt: (layer, pt[b, s * npp + i], 0, 0))

    grid_spec = pltpu.PrefetchScalarGridSpec(
        num_scalar_prefetch=1,
        grid=(bsz, nsteps),
        in_specs=[pl.BlockSpec((SAMPLE_T_PAD, hq), lambda b, s, pt: (b, 0))]
                 + [lat_spec(i) for i in range(npp)] + [rope_spec(i) for i in range(npp)]
                 + [pl.BlockSpec((None, PAGE_SIZE, MLA_KV_RANK), lambda b, s, pt: (b, 0, 0)),
                    pl.BlockSpec((None, PAGE_SIZE, LANES), lambda b, s, pt: (b, 0, 0)),
                    pl.BlockSpec((MLA_KV_RANK, hv), lambda b, s, pt: (0, 0)),
                    pl.BlockSpec((MLA_KV_RANK, hv), lambda b, s, pt: (0, 0)),
                    pl.BlockSpec((1, LANES), lambda b, s, pt: (0, 0)),
                    pl.BlockSpec((1, LANES), lambda b, s, pt: (0, 0))],
        out_specs=pl.BlockSpec((SAMPLE_T_PAD, hv), lambda b, s, pt: (b, 0)),
        scratch_shapes=[pltpu.VMEM((MLA_HEADS * SAMPLE_T_PAD, 1), F32),
                        pltpu.VMEM((MLA_HEADS * SAMPLE_T_PAD, 1), F32),
                        pltpu.VMEM((MLA_HEADS * SAMPLE_T_PAD, MLA_KV_RANK), F32),
                        pltpu.VMEM((PAGE_SIZE, LANES), F32)],
    )
    kern = functools.partial(_paged_kernel, t_new=t_new)
    return pl.pallas_call(
        kern,
        grid_spec=grid_spec,
        out_shape=jax.ShapeDtypeStruct((bsz * SAMPLE_T_PAD, hv), F32),
        compiler_params=_cparams("parallel", "arbitrary"),
    )(page_table, q, *([lat_pool] * npp), *([rope_pool] * npp), new_c, new_r,
      p["w_kb"], p["w_vb"], p["gk_nope"], p["gk_rope"])


def _merge_kernel(x_ref, gl_ref, os_ref, or_ref, om_ref, ws_ref, wr_ref, wm_ref, wo_ref, o_ref):
    gl = gl_ref[...]
    d = D_MODEL
    merged = (_sigmoid(gl[:, :d]) * _dot(os_ref[...], ws_ref[...])
              + _sigmoid(gl[:, d:2 * d]) * _dot(or_ref[...], wr_ref[...])
              + _sigmoid(gl[:, 2 * d:]) * _dot(om_ref[...].astype(BF16), wm_ref[...]))
    o_ref[...] = x_ref[...] + _dot(merged.astype(BF16), wo_ref[...])


def _merge(x, gl, o_ssd, o_rwkv, o_mla, p, tm):
    n, d = x.shape
    row = pl.BlockSpec((tm, d), lambda i: (i, 0))
    return pl.pallas_call(
        _merge_kernel,
        grid=(n // tm,),
        in_specs=[row, pl.BlockSpec((tm, N_BRANCH * d), lambda i: (i, 0)), row, row, row]
                 + [_full((d, d))] * 4,
        out_specs=row,
        out_shape=jax.ShapeDtypeStruct((n, d), F32),
        compiler_params=_cparams("parallel"),
    )(x, gl, o_ssd, o_rwkv, o_mla, p["w_o_ssd"], p["w_o_rwkv"], p["w_o_mla"], p["w_out"])


def _ffn_kernel(x_ref, tail0_ref, g_ref, wup_ref, cw_ref, cb_ref, wdn_ref, o_ref, tail_ref, buf, *, tm, t_valid):
    j = pl.program_id(1)

    @pl.when(j == 0)
    def _():
        buf[0:SUBLANES, :] = tail0_ref[...]

    x = x_ref[...]
    h2 = (_rms(x) * g_ref[...]).astype(BF16)
    buf[SUBLANES:SUBLANES + tm, :] = _dot(h2, wup_ref[...])
    conv = cb_ref[...]
    for i in range(FFN_CONV):
        conv = conv + buf[pl.ds(SUBLANES - (FFN_CONV - 1) + i, tm), :] * cw_ref[i:i + 1, :]
    tail = buf[pl.ds(t_valid, SUBLANES), :]
    tail_ref[...] = tail
    buf[0:SUBLANES, :] = tail
    act = (_silu(conv[:, :D_FF]) * conv[:, D_FF:]).astype(BF16)
    o_ref[...] = x + _dot(act, wdn_ref[...])


def _ffn(x, tail0, p, tm, t_valid):
    bsz = tail0.shape[0]
    n, d = x.shape
    nt = n // bsz // tm
    kern = functools.partial(_ffn_kernel, tm=tm, t_valid=t_valid)
    return pl.pallas_call(
        kern,
        grid=(bsz, nt),
        in_specs=[pl.BlockSpec((tm, d), lambda b, j: (b * nt + j, 0)),
                  pl.BlockSpec((None, SUBLANES, 2 * D_FF), lambda b, j: (b, 0, 0)),
                  _full((1, d)),
                  pl.BlockSpec((d, 2 * D_FF), lambda b, j: (0, 0), pipeline_mode=pl.Buffered(1)),
                  _full((FFN_CONV, 2 * D_FF)), _full((1, 2 * D_FF)),
                  pl.BlockSpec((D_FF, d), lambda b, j: (0, 0), pipeline_mode=pl.Buffered(1))],
        out_specs=[pl.BlockSpec((tm, d), lambda b, j: (b * nt + j, 0)),
                   pl.BlockSpec((None, SUBLANES, 2 * D_FF), lambda b, j: (b, 0, 0))],
        out_shape=[jax.ShapeDtypeStruct((n, d), F32),
                   jax.ShapeDtypeStruct((bsz, SUBLANES, 2 * D_FF), F32)],
        scratch_shapes=[pltpu.VMEM((SUBLANES + tm, 2 * D_FF), F32)],
        compiler_params=_cparams("parallel", "arbitrary"),
    )(x, tail0, p["norm_ffn_g"], p["ffn_w_up"], p["ffn_conv_w"], p["ffn_conv_b"], p["ffn_w_down"])


def _pad_cols(w, width):
    return jnp.pad(w, ((0, 0), (0, width - w.shape[1])))


def _row(v):
    return v.reshape(1, -1)


def _prep_layer(i, w):
    d = RWKV_DIM
    w_in = w["w_in"][i]
    o_r = SSD_IN
    o_m = SSD_IN + RWKV_IN
    p = {}
    p["norm_mix_g"] = _row(w["norm_mix_g"][i])
    p["w_ssd"] = _pad_cols(w_in[:, :o_r], SSD_IN_PAD).astype(BF16)
    p["w_rwkv"] = w_in[:, o_r:o_m].astype(BF16)
    p["w_mla"] = _pad_cols(w_in[:, o_m:], MLA_IN_PAD).astype(BF16)
    p["w_gate"] = w["w_gate"][i].astype(BF16)
    p["b_gate"] = _row(w["b_gate"][i])
    p["ssd_conv_w"] = w["ssd_conv_w"][i]
    p["ssd_conv_b"] = _row(w["ssd_conv_b"][i])
    p["ssd_dt_bias"] = _pad_cols(_row(w["ssd_dt_bias"][i]), LANES)
    p["ssd_a_log"] = _pad_cols(_row(w["ssd_a_log"][i]), LANES)
    p["ssd_d"] = _row(jnp.repeat(w["ssd_d"][i], SSD_HEAD_DIM))
    p["ssd_norm_g"] = _row(w["ssd_norm_g"][i])
    p["rwkv_shift_mu"] = _row(w["rwkv_shift_mu"][i])
    p["rwkv_w0"] = _row(w["rwkv_w0"][i])
    zeros_lora = jnp.zeros((RWKV_DECAY_LORA, d), F32)
    p["rwkv_w_up"] = jnp.concatenate([w["rwkv_w_up"][i], zeros_lora], axis=0).astype(BF16)
    p["rwkv_a0"] = _row(w["rwkv_a0"][i])
    p["rwkv_a_up"] = jnp.concatenate([zeros_lora, w["rwkv_a_up"][i]], axis=0).astype(BF16)
    p["rwkv_g_up"] = w["rwkv_g_up"][i].astype(BF16)
    p["rwkv_k_k"] = _row(w["rwkv_k_k"][i])
    p["rwkv_k_a"] = _row(w["rwkv_k_a"][i])
    p["rwkv_r_k"] = _row(w["rwkv_r_k"][i])
    p["rwkv_ln_g"] = _row(w["rwkv_ln_g"][i])
    p["rwkv_ln_b"] = _row(w["rwkv_ln_b"][i])
    wq = w["mla_w_q_b"][i]
    hv = MLA_HEADS * MLA_NOPE
    p["mla_q_a_g"] = _row(w["mla_q_a_g"][i])
    p["wq_nope"] = wq[:, :, :MLA_NOPE].reshape(MLA_Q_RANK, hv).astype(BF16)
    p["wq_rope"] = jnp.pad(wq[:, :, MLA_NOPE:], ((0, 0), (0, 0), (0, MLA_NOPE - MLA_ROPE))
                           ).reshape(MLA_Q_RANK, hv).astype(BF16)
    p["mla_kv_a_g"] = _row(w["mla_kv_a_g"][i])
    p["w_kb"] = w["mla_w_kb"][i].reshape(MLA_KV_RANK, hv).astype(BF16)
    p["w_vb"] = w["mla_w_vb"][i].reshape(MLA_KV_RANK, MLA_HEADS * MLA_V).astype(BF16)
    gq, gk = w["mla_q_norm_g"][i], w["mla_k_norm_g"][i]
    p["gq_nope"] = _row(gq[:MLA_NOPE])
    p["gq_rope"] = _pad_cols(_row(gq[MLA_NOPE:]), LANES)
    p["gk_nope"] = _row(gk[:MLA_NOPE])
    p["gk_rope"] = _pad_cols(_row(gk[MLA_NOPE:]), LANES)
    for name in ("w_o_ssd", "w_o_rwkv", "w_o_mla", "w_out"):
        p[name] = w[name][i].astype(BF16)
    p["norm_ffn_g"] = _row(w["norm_ffn_g"][i])
    p["ffn_w_up"] = w["ffn_w_up"][i].astype(BF16)
    p["ffn_conv_w"] = w["ffn_conv_w"][i]
    p["ffn_conv_b"] = _row(w["ffn_conv_b"][i])
    p["ffn_w_down"] = w["ffn_w_down"][i].astype(BF16)
    return p


def _constants():
    ch = np.arange(RWKV_DIM)
    seg = (ch[:, None] // RWKV_HEAD_DIM == np.arange(LANES)[None, :]).astype(np.float32)
    seg2 = np.concatenate([seg, seg], axis=0)
    seg_t2 = np.concatenate([seg.T, seg.T], axis=0)
    ln = np.arange(LANES)
    e = (ln[:, None] // RWKV_HEAD_DIM == ln[None, :] // RWKV_HEAD_DIM).astype(np.float32)
    e2 = np.concatenate([e, e], axis=0)
    return (jnp.asarray(seg2, BF16), jnp.asarray(seg_t2, BF16), jnp.asarray(e2, BF16))


def _rope_tables(pos):
    half = MLA_ROPE // 2
    inv = ROPE_BASE ** (-jnp.arange(half, dtype=F32) / half)
    ang = pos.astype(F32)[:, None] * inv[None, :]
    cos, sin = jnp.cos(ang), jnp.sin(ang)
    z = jnp.zeros_like(cos)
    z2 = jnp.zeros((pos.shape[0], LANES - MLA_ROPE), F32)
    return (jnp.concatenate([cos, cos, z2], axis=1),
            jnp.concatenate([-sin, z, z2], axis=1),
            jnp.concatenate([z, sin, z2], axis=1))


def _tail_block(state, rows):
    return jnp.pad(state, ((0, 0), (SUBLANES - rows, 0), (0, 0)))


def _pack_rwkv_state(s):
    b = s.shape[0]
    return s.reshape(b, RWKV_HEADS // 2, 2, RWKV_HEAD_DIM, RWKV_HEAD_DIM).transpose(0, 1, 3, 2, 4).reshape(
        b, RWKV_ROWS, LANES)


def _unpack_rwkv_state(s):
    b = s.shape[0]
    return s.reshape(b, RWKV_HEADS // 2, RWKV_HEAD_DIM, 2, RWKV_HEAD_DIM).transpose(0, 1, 3, 2, 4).reshape(
        b, RWKV_HEADS, RWKV_HEAD_DIM, RWKV_HEAD_DIM)


def _unpack_rwkv_y(y, t_total):
    b, nw = y.shape[:2]
    y = y.reshape(b, nw, RWKV_HEADS // 2, RWKV_HEAD_DIM, 2, RWKV_WINDOW)
    y = y.transpose(0, 1, 5, 2, 4, 3).reshape(b, nw * RWKV_WINDOW, RWKV_DIM)
    return y[:, :t_total].reshape(b * t_total, RWKV_DIM)


def _layer(x, st, p, cfg, attend):
    bsz, t, t_valid, tm = cfg["bsz"], cfg["t"], cfg["t_valid"], cfg["tm"]
    n = x.shape[0]
    tmn = min(tm, n)
    zb = lambda w_: jnp.zeros((1, w_), F32)
    g = p["norm_mix_g"]
    u_ssd = _norm_matmul(x, g, p["w_ssd"], zb(SSD_IN_PAD), tmn)
    u_rwkv = _norm_matmul(x, g, p["w_rwkv"], zb(RWKV_IN), tmn)
    u_mla = _norm_matmul(x, g, p["w_mla"], zb(MLA_IN_PAD), tmn)
    gl = _norm_matmul(x, g, p["w_gate"], p["b_gate"], tmn)

    if t % SSD_CHUNK:
        u_pad = jnp.pad(u_ssd.reshape(bsz, t, SSD_IN_PAD), ((0, 0), (0, SSD_CHUNK - t), (0, 0)))
        o_ssd, ssd_tail, ssd_h = _ssd(u_pad.reshape(bsz * SSD_CHUNK, SSD_IN_PAD), st["ssd_tail"], st["ssd_h"],
                                      p, t_valid)
        o_ssd = o_ssd.reshape(bsz, SSD_CHUNK, SSD_INNER)[:, :t].reshape(n, SSD_INNER)
    else:
        o_ssd, ssd_tail, ssd_h = _ssd(u_ssd, st["ssd_tail"], st["ssd_h"], p, SSD_CHUNK)

    tm_r = min(tm, t)
    r, wd, k2, v, al, be, gg, rwkv_tail = _rwkv_pre(u_rwkv, st["rwkv_tail"], p, tm_r, min(t_valid, tm_r))
    d = RWKV_DIM
    to3 = lambda a_: a_.reshape(bsz, t, d)
    vt = jnp.swapaxes(to3(v), 1, 2)
    if t % LANES:
        vt = jnp.pad(vt, ((0, 0), (0, 0), (0, LANES - t)))
    tc = min(t, LANES)
    y_w, rwkv_s = _rwkv_scan(to3(r), to3(wd), to3(k2), to3(al), to3(be), vt, st["rwkv_s"], p["e2"],
                             cfg["bb"], tc, min(t_valid, tc))
    if t < RWKV_WINDOW:
        yy = _unpack_rwkv_y(y_w, RWKV_WINDOW).reshape(bsz, RWKV_WINDOW, d)[:, :t].reshape(n, d)
    else:
        yy = _unpack_rwkv_y(y_w, t)
    o_rwkv = _rwkv_post(yy, r, k2, v, gg, p, tmn)

    q, k, vv, ckv, kpe = _mla_prep(u_mla, cfg["rope"], p, tmn, cfg["q_dtype"])
    o_mla = attend(q, k, vv, ckv, kpe)

    x = _merge(x, gl, o_ssd, o_rwkv, o_mla, p, tmn)
    tm_f = min(cfg["tm_ffn"], t)
    x, ffn_tail = _ffn(x, st["ffn_tail"], p, tm_f, min(t_valid, tm_f))
    new_st = dict(ssd_tail=ssd_tail, ssd_h=ssd_h, rwkv_tail=rwkv_tail, rwkv_s=rwkv_s, ffn_tail=ffn_tail)
    return x, new_st, ckv, kpe


def kernel(x_prompt, x_sample, cache_kv_latent, cache_k_rope, page_table, state_ssm, state_ssm_conv,
           state_rwkv, state_rwkv_shift, state_ffn_conv, norm_mix_g, w_in, ssd_conv_w, ssd_conv_b,
           ssd_dt_bias, ssd_a_log, ssd_d, ssd_norm_g, rwkv_shift_mu, rwkv_w0, rwkv_w_up, rwkv_a0,
           rwkv_a_up, rwkv_g_up, rwkv_k_k, rwkv_k_a, rwkv_r_k, rwkv_ln_g, rwkv_ln_b, mla_q_a_g,
           mla_w_q_b, mla_kv_a_g, mla_w_kb, mla_w_vb, mla_q_norm_g, mla_k_norm_g, w_gate, b_gate,
           w_o_ssd, w_o_rwkv, w_o_mla, w_out, norm_ffn_g, ffn_w_up, ffn_conv_w, ffn_conv_b, ffn_w_down):
    w = dict(norm_mix_g=norm_mix_g, w_in=w_in, ssd_conv_w=ssd_conv_w, ssd_conv_b=ssd_conv_b,
             ssd_dt_bias=ssd_dt_bias, ssd_a_log=ssd_a_log, ssd_d=ssd_d, ssd_norm_g=ssd_norm_g,
             rwkv_shift_mu=rwkv_shift_mu, rwkv_w0=rwkv_w0, rwkv_w_up=rwkv_w_up, rwkv_a0=rwkv_a0,
             rwkv_a_up=rwkv_a_up, rwkv_g_up=rwkv_g_up, rwkv_k_k=rwkv_k_k, rwkv_k_a=rwkv_k_a,
             rwkv_r_k=rwkv_r_k, rwkv_ln_g=rwkv_ln_g, rwkv_ln_b=rwkv_ln_b, mla_q_a_g=mla_q_a_g,
             mla_w_q_b=mla_w_q_b, mla_kv_a_g=mla_kv_a_g, mla_w_kb=mla_w_kb, mla_w_vb=mla_w_vb,
             mla_q_norm_g=mla_q_norm_g, mla_k_norm_g=mla_k_norm_g, w_gate=w_gate, b_gate=b_gate,
             w_o_ssd=w_o_ssd, w_o_rwkv=w_o_rwkv, w_o_mla=w_o_mla, w_out=w_out, norm_ffn_g=norm_ffn_g,
             ffn_w_up=ffn_w_up, ffn_conv_w=ffn_conv_w, ffn_conv_b=ffn_conv_b, ffn_w_down=ffn_w_down)
    depth = w_in.shape[0]
    bp, tp, d = x_prompt.shape
    bs, ts, _ = x_sample.shape
    tsp = SAMPLE_T_PAD
    past_len = page_table.shape[1] * PAGE_SIZE
    seg2, seg_t2, e2 = _constants()

    tm_p = min(256, tp)
    cfg_p = dict(bsz=bp, t=tp, t_valid=tp, tm=tm_p, tm_ffn=min(128, tp), bb=min(4, bp),
                 rope=_rope_tables(jnp.arange(tp)), q_dtype=BF16)
    pos_s = past_len + jnp.arange(tsp)
    rope_s = tuple(jnp.tile(tb, (bs, 1)) for tb in _rope_tables(pos_s))
    cfg_s = dict(bsz=bs, t=tsp, t_valid=ts, tm=bs * tsp, tm_ffn=tsp, bb=min(4, bs), rope=rope_s,
                 q_dtype=F32)

    xp = x_prompt.reshape(bp * tp, d)
    xs = jnp.pad(x_sample, ((0, 0), (0, tsp - ts), (0, 0))).reshape(bs * tsp, d)

    zero_st = dict(ssd_tail=jnp.zeros((bp, SUBLANES, SSD_CONV_DIM), F32),
                   ssd_h=jnp.zeros((bp, SSD_HEADS // 2, LANES, SSD_STATE), F32),
                   rwkv_tail=jnp.zeros((bp, SUBLANES, RWKV_IN), F32),
                   rwkv_s=jnp.zeros((bp, RWKV_ROWS, LANES), F32),
                   ffn_tail=jnp.zeros((bp, SUBLANES, 2 * D_FF), F32))

    new_p = [[] for _ in range(7)]
    new_s = [[] for _ in range(7)]
    for i in range(depth):
        p = _prep_layer(i, w)
        p["seg2"], p["seg_t2"], p["e2"] = seg2, seg_t2, e2

        def prompt_attend(q, k, v, ckv, kpe):
            return _flash(q, k, v, bp, min(512, tp))

        xp, st_p, ckv_p, kpe_p = _layer(xp, zero_st, p, cfg_p, prompt_attend)

        st_in = dict(ssd_tail=_tail_block(state_ssm_conv[i], SSD_CONV - 1),
                     ssd_h=state_ssm[i].reshape(bs, SSD_HEADS // 2, LANES, SSD_STATE),
                     rwkv_tail=_tail_block(state_rwkv_shift[i][:, None, :], 1),
                     rwkv_s=_pack_rwkv_state(state_rwkv[i]),
                     ffn_tail=_tail_block(state_ffn_conv[i], FFN_CONV - 1))

        def sample_attend(q, k, v, ckv, kpe, i=i, p=p):
            pad = ((0, 0), (0, PAGE_SIZE - tsp), (0, 0))
            new_c = jnp.pad(ckv.reshape(bs, tsp, MLA_KV_RANK), pad)
            new_r = jnp.pad(kpe.reshape(bs, tsp, LANES), pad)
            return _paged_attention(page_table, q, cache_kv_latent, cache_k_rope, i, new_c, new_r, p, ts)

        xs, st_s, ckv_s, kpe_s = _layer(xs, st_in, p, cfg_s, sample_attend)

        for lst, st, ckv, kpe, b_, t_, tv in ((new_p, st_p, ckv_p, kpe_p, bp, tp, tp),
                                              (new_s, st_s, ckv_s, kpe_s, bs, tsp, ts)):
            lst[0].append(ckv.reshape(b_, t_, MLA_KV_RANK)[:, :tv])
            lst[1].append(kpe.reshape(b_, t_, LANES)[:, :tv, :MLA_ROPE])
            lst[2].append(st["ssd_h"].reshape(b_, SSD_HEADS, SSD_HEAD_DIM, SSD_STATE))
            lst[3].append(st["ssd_tail"][:, SUBLANES - (SSD_CONV - 1):])
            lst[4].append(_unpack_rwkv_state(st["rwkv_s"]))
            lst[5].append(st["rwkv_tail"][:, SUBLANES - 1])
            lst[6].append(st["ffn_tail"][:, SUBLANES - (FFN_CONV - 1):])

    outs_p = [jnp.stack(v_, axis=0) for v_ in new_p]
    outs_s = [jnp.stack(v_, axis=0) for v_ in new_s]
    y_p = xp.reshape(bp, tp, d)
    y_s = xs.reshape(bs, tsp, d)[:, :ts]
    return (y_p, y_s, *outs_p, *outs_s)
```

```python
import functools
import math

import numpy as np
import jax
import jax.numpy as jnp
from jax import lax
from jax.experimental import pallas as pl
from jax.experimental.pallas import tpu as pltpu

F32 = jnp.float32
BF16 = jnp.bfloat16

D_MODEL = 1024
PAGE_SIZE = 128

SSD_HEAD_DIM = 64
SSD_INNER = 1024
SSD_HEADS = 16
SSD_GROUPS = 2
SSD_STATE = 128
SSD_CONV = 4
SSD_CHUNK = 128
SSD_CONV_DIM = SSD_INNER + 2 * SSD_GROUPS * SSD_STATE
SSD_IN = SSD_INNER + SSD_CONV_DIM + SSD_HEADS
SSD_IN_PAD = SSD_INNER + SSD_CONV_DIM + 128

RWKV_HEAD_DIM = 64
RWKV_DIM = 1024
RWKV_HEADS = 16
RWKV_DECAY_LORA = 64
RWKV_A_LORA = 64
RWKV_GATE_LORA = 128
RWKV_IN = 3 * RWKV_DIM + RWKV_DECAY_LORA + RWKV_A_LORA + RWKV_GATE_LORA
RWKV_LN_EPS = 64e-5

MLA_HEADS = 8
MLA_NOPE = 128
MLA_ROPE = 64
MLA_QK = MLA_NOPE + MLA_ROPE
MLA_V = 128
MLA_Q_RANK = 512
MLA_KV_RANK = 256
MLA_IN = MLA_Q_RANK + MLA_KV_RANK + MLA_ROPE
MLA_IN_PAD = MLA_Q_RANK + MLA_KV_RANK + 128
MLA_QK_PAD = 256
ROPE_BASE = 10000.0
MLA_SCALE = MLA_QK ** -0.5

N_BRANCH = 3
D_FF = 2816
FFN_CONV = 3
NORM_EPS = 1e-6

SUBLANES = 8
LANES = 128
SAMPLE_T_PAD = 8
NEG_BIG = -1e30

VMEM_LIMIT = 56 * 1024 * 1024


def _cparams(*sem):
    return pltpu.CompilerParams(dimension_semantics=sem, vmem_limit_bytes=VMEM_LIMIT)


def _full(shape):
    nd = len(shape)
    return pl.BlockSpec(shape, lambda *_: (0,) * nd)


def _rms(x, eps=NORM_EPS):
    return x * lax.rsqrt(jnp.mean(x * x, axis=-1, keepdims=True) + eps)


def _sigmoid(x):
    return 1.0 / (1.0 + jnp.exp(-x))


def _silu(x):
    return x * _sigmoid(x)


def _softplus(x):
    return jnp.maximum(x, 0.0) + jnp.log(1.0 + jnp.exp(-jnp.abs(x)))


def _dot(a, b):
    return jnp.dot(a, b, preferred_element_type=F32)


def _dot_nt(a, b):
    return lax.dot_general(a, b, (((1,), (1,)), ((), ())), preferred_element_type=F32)


def _dot_tn(a, b):
    return lax.dot_general(a, b, (((0,), (0,)), ((), ())), preferred_element_type=F32)


def _split_hi_lo(x):
    hi = x.astype(BF16)
    lo = (x - hi.astype(F32)).astype(BF16)
    return jnp.concatenate([hi, lo], axis=1)


def _norm_matmul_kernel(x_ref, g_ref, w_ref, b_ref, o_ref):
    h = (_rms(x_ref[...]) * g_ref[...]).astype(BF16)
    o_ref[...] = _dot(h, w_ref[...]) + b_ref[...]


def _norm_matmul(x, g, w, bias, tm):
    n, d = x.shape
    nout = w.shape[1]
    return pl.pallas_call(
        _norm_matmul_kernel,
        grid=(n // tm,),
        in_specs=[pl.BlockSpec((tm, d), lambda i: (i, 0)), _full((1, d)),
                  _full((d, nout)), _full((1, nout))],
        out_specs=pl.BlockSpec((tm, nout), lambda i: (i, 0)),
        out_shape=jax.ShapeDtypeStruct((n, nout), F32),
        compiler_params=_cparams("parallel"),
    )(x, g, w, bias)


def _ssd_kernel(u_ref, tail0_ref, h0_ref, cw_ref, cb_ref, dtb_ref, alog_ref, dsk_ref, ng_ref,
                o_ref, tail_ref, h_ref, buf, *, t_valid):
    L = SSD_CHUNK
    c = pl.program_id(1)

    @pl.when(c == 0)
    def _():
        buf[0:SUBLANES, :] = tail0_ref[...]
        h_ref[...] = h0_ref[...]

    u = u_ref[...]
    z = u[:, :SSD_INNER]
    dt_raw = u[:, SSD_INNER + SSD_CONV_DIM:]
    buf[SUBLANES:SUBLANES + L, :] = u[:, SSD_INNER:SSD_INNER + SSD_CONV_DIM]
    acc = cb_ref[...]
    for j in range(SSD_CONV):
        acc = acc + buf[pl.ds(SUBLANES - (SSD_CONV - 1) + j, L), :] * cw_ref[j:j + 1, :]
    tail = buf[pl.ds(t_valid, SUBLANES), :]
    tail_ref[...] = tail
    buf[0:SUBLANES, :] = tail
    xbc = _silu(acc)
    xs = xbc[:, :SSD_INNER]

    dt = _softplus(dt_raw + dtb_ref[...])
    if t_valid < L:
        row = lax.broadcasted_iota(jnp.int32, dt.shape, 0)
        dt = jnp.where(row < t_valid, dt, 0.0)
    a = -jnp.exp(alog_ref[...])
    da = dt * a
    ri = lax.broadcasted_iota(jnp.int32, (L, L), 0)
    ci = lax.broadcasted_iota(jnp.int32, (L, L), 1)
    causal = ri >= ci
    tri = jnp.where(causal, 1.0, 0.0).astype(F32)
    cs = jnp.dot(tri, da, preferred_element_type=F32, precision=lax.Precision.HIGHEST)
    cs_t = cs.T
    total = cs[L - 1:L, :]
    etot = jnp.exp(total)
    lane = lax.broadcasted_iota(jnp.int32, (L, LANES), 1)
    lo_half = lane < SSD_HEAD_DIM
    row_lo = lax.broadcasted_iota(jnp.int32, (LANES, LANES), 0) < SSD_HEAD_DIM

    def pair_cols(m, j0):
        return jnp.where(lo_half, m[:, j0:j0 + 1], m[:, j0 + 1:j0 + 2])

    hpg = SSD_HEADS // SSD_GROUPS
    y_parts = []
    for g in range(SSD_GROUPS):
        b_g = xbc[:, SSD_INNER + g * SSD_STATE:SSD_INNER + (g + 1) * SSD_STATE].astype(BF16)
        c0 = SSD_INNER + SSD_GROUPS * SSD_STATE + g * SSD_STATE
        c_g = xbc[:, c0:c0 + SSD_STATE].astype(BF16)
        cb = _dot_nt(c_g, b_g)
        for m in range(hpg // 2):
            q = g * (hpg // 2) + m
            j0 = 2 * q
            xs_p = xs[:, q * LANES:(q + 1) * LANES]
            xd = xs_p * pair_cols(dt, j0)
            xd_b = xd.astype(BF16)
            ys = []
            for e in range(2):
                j = j0 + e
                diff = cs[:, j:j + 1] - cs_t[j:j + 1, :]
                lm = jnp.where(causal, jnp.exp(jnp.where(causal, diff, 0.0)), 0.0)
                ys.append(_dot((cb * lm).astype(BF16), xd_b))
            y_diag = jnp.where(lo_half, ys[0], ys[1])
            hp = h_ref[q]
            cs_p = pair_cols(cs, j0)
            y_off = _dot_nt(c_g, hp.astype(BF16)) * jnp.exp(cs_p)
            tot_p = jnp.where(lo_half[0:1, :], total[:, j0:j0 + 1], total[:, j0 + 1:j0 + 2])
            dte = jnp.exp(tot_p - cs_p)
            contrib = _dot_tn((xd * dte).astype(BF16), b_g)
            decay = jnp.where(row_lo, etot[:, j0:j0 + 1], etot[:, j0 + 1:j0 + 2])
            h_ref[q] = hp * decay + contrib
            y_parts.append(y_diag + y_off + dsk_ref[:, q * LANES:(q + 1) * LANES] * xs_p)
    y = jnp.concatenate(y_parts, axis=1) * _silu(z)
    gw = SSD_INNER // SSD_GROUPS
    outs = [_rms(y[:, g * gw:(g + 1) * gw]) * ng_ref[:, g * gw:(g + 1) * gw] for g in range(SSD_GROUPS)]
    o_ref[...] = jnp.concatenate(outs, axis=1).astype(BF16)


def _ssd(u, tail0, h0, p, t_valid):
    bsz = h0.shape[0]
    n = u.shape[0]
    L = SSD_CHUNK
    nch = n // bsz // L
    kern = functools.partial(_ssd_kernel, t_valid=t_valid)
    return pl.pallas_call(
        kern,
        grid=(bsz, nch),
        in_specs=[pl.BlockSpec((L, SSD_IN_PAD), lambda b, c: (b * nch + c, 0)),
                  pl.BlockSpec((None, SUBLANES, SSD_CONV_DIM), lambda b, c: (b, 0, 0)),
                  pl.BlockSpec((None, SSD_HEADS // 2, LANES, SSD_STATE), lambda b, c: (b, 0, 0, 0)),
                  _full((SSD_CONV, SSD_CONV_DIM)), _full((1, SSD_CONV_DIM)),
                  _full((1, LANES)), _full((1, LANES)), _full((1, SSD_INNER)), _full((1, SSD_INNER))],
        out_specs=[pl.BlockSpec((L, SSD_INNER), lambda b, c: (b * nch + c, 0)),
                   pl.BlockSpec((None, SUBLANES, SSD_CONV_DIM), lambda b, c: (b, 0, 0)),
                   pl.BlockSpec((None, SSD_HEADS // 2, LANES, SSD_STATE), lambda b, c: (b, 0, 0, 0))],
        out_shape=[jax.ShapeDtypeStruct((n, SSD_INNER), BF16),
                   jax.ShapeDtypeStruct((bsz, SUBLANES, SSD_CONV_DIM), F32),
                   jax.ShapeDtypeStruct((bsz, SSD_HEADS // 2, LANES, SSD_STATE), F32)],
        scratch_shapes=[pltpu.VMEM((SUBLANES + L, SSD_CONV_DIM), F32)],
        compiler_params=_cparams("parallel", "arbitrary"),
    )(u, tail0, h0, p["ssd_conv_w"], p["ssd_conv_b"], p["ssd_dt_bias"], p["ssd_a_log"],
      p["ssd_d"], p["ssd_norm_g"])


def _seg_sum_bcast(x, seg2, seg_t2):
    s16 = _dot(_split_hi_lo(x), seg2)
    return _dot(_split_hi_lo(s16), seg_t2)


def _rwkv_pre_kernel(u_ref, tail0_ref, mu_ref, w0_ref, wup_ref, a0_ref, aup_ref, gup_ref, kk_ref,
                     ka_ref, seg2_ref, segt2_ref,
                     r_ref, w_ref, k_ref, v_ref, al_ref, be_ref, g_ref, tail_ref, buf, *, tm, t_valid):
    j = pl.program_id(1)

    @pl.when(j == 0)
    def _():
        buf[0:SUBLANES, :] = tail0_ref[...]

    u = u_ref[...]
    buf[SUBLANES:SUBLANES + tm, :] = u
    prev = buf[pl.ds(SUBLANES - 1, tm), :]
    tail = buf[pl.ds(t_valid, SUBLANES), :]
    tail_ref[...] = tail
    buf[0:SUBLANES, :] = tail

    d = RWKV_DIM
    f = u + mu_ref[...] * (prev - u)
    r, k, v = f[:, :d], f[:, d:2 * d], f[:, 2 * d:3 * d]
    lo = f[:, 3 * d:3 * d + LANES]
    glo = f[:, 3 * d + LANES:]
    ww = w0_ref[...] + _dot(jnp.tanh(lo).astype(BF16), wup_ref[...])
    w_log = -_softplus(-ww) - 0.5
    a = _sigmoid(a0_ref[...] + _dot(lo.astype(BF16), aup_ref[...]))
    kk = k * kk_ref[...]
    ssb = _seg_sum_bcast(kk * kk, seg2_ref[...], segt2_ref[...])
    kkn = kk / jnp.maximum(jnp.sqrt(ssb), 1e-12)
    r_ref[...] = r
    w_ref[...] = -jnp.exp(w_log)
    k_ref[...] = k * (1.0 + (a - 1.0) * ka_ref[...])
    v_ref[...] = v
    al_ref[...] = -kkn
    be_ref[...] = kkn * a
    g_ref[...] = _dot(_sigmoid(glo).astype(BF16), gup_ref[...])


def _rwkv_pre(u, tail0, p, tm, t_valid):
    bsz = tail0.shape[0]
    n = u.shape[0]
    nt = n // bsz // tm
    d = RWKV_DIM
    kern = functools.partial(_rwkv_pre_kernel, tm=tm, t_valid=t_valid)
    row = pl.BlockSpec((tm, d), lambda b, j: (b * nt + j, 0))
    return pl.pallas_call(
        kern,
        grid=(bsz, nt),
        in_specs=[pl.BlockSpec((tm, RWKV_IN), lambda b, j: (b * nt + j, 0)),
                  pl.BlockSpec((None, SUBLANES, RWKV_IN), lambda b, j: (b, 0, 0)),
                  _full((1, RWKV_IN)), _full((1, d)), _full((LANES, d)), _full((1, d)),
                  _full((LANES, d)), _full((LANES, d)), _full((1, d)), _full((1, d)),
                  _full((2 * d, LANES)), _full((2 * LANES, d))],
        out_specs=[row] * 7 + [pl.BlockSpec((None, SUBLANES, RWKV_IN), lambda b, j: (b, 0, 0))],
        out_shape=[jax.ShapeDtypeStruct((n, d), F32)] * 7
                  + [jax.ShapeDtypeStruct((bsz, SUBLANES, RWKV_IN), F32)],
        scratch_shapes=[pltpu.VMEM((SUBLANES + tm, RWKV_IN), F32)],
        compiler_params=_cparams("parallel", "arbitrary"),
    )(u, tail0, p["rwkv_shift_mu"], p["rwkv_w0"], p["rwkv_w_up"], p["rwkv_a0"], p["rwkv_a_up"],
      p["rwkv_g_up"], p["rwkv_k_k"], p["rwkv_k_a"], p["seg2"], p["seg_t2"])


RWKV_CHUNK = 64
RWKV_SOLVE_BLOCK = 16
RWKV_PAIRS = RWKV_HEADS // 2


def _hi_lo(x):
    hi = x.astype(BF16)
    return hi, (x - hi.astype(F32)).astype(BF16)


def _mm(a, b):
    return _dot(a.astype(BF16), b.astype(BF16))


def _mm3(a, b):
    ah, al = _hi_lo(a)
    bh, bl = _hi_lo(b)
    return _dot(ah, bh) + (_dot(al, bh) + _dot(ah, bl))


def _mm3_nt(a, b):
    ah, al = _hi_lo(a)
    bh, bl = _hi_lo(b)
    return _dot_nt(ah, bh) + (_dot_nt(al, bh) + _dot_nt(ah, bl))


def _rwkv_chunk_kernel(r_ref, lw_ref, k_ref, v_ref, al_ref, be_ref, s0_ref, y_ref, s_ref, *, nchunks):
    c = pl.program_id(1)

    @pl.when(c == 0)
    def _():
        s_ref[...] = s0_ref[...]

    C, HD = RWKV_CHUNK, RWKV_HEAD_DIM
    r2 = lax.broadcasted_iota(jnp.int32, (LANES, LANES), 0)
    c2 = lax.broadcasted_iota(jnp.int32, (LANES, LANES), 1)
    eye = jnp.where(r2 == c2, 1.0, 0.0).astype(F32)
    diag_blk = (r2 // RWKV_SOLVE_BLOCK) == (c2 // RWKV_SOLVE_BLOCK)
    same_head = (r2 // HD) == (c2 // HD)
    tri = jnp.where(lax.broadcasted_iota(jnp.int32, (C, C), 0) >= lax.broadcasted_iota(jnp.int32, (C, C), 1),
                    1.0, 0.0).astype(F32)
    t_i = lax.broadcasted_iota(jnp.int32, (C, LANES), 0)
    lane = lax.broadcasted_iota(jnp.int32, (C, LANES), 1)
    lo_half = lane < HD
    hi_half = lane >= HD
    strict = (lane % HD) < t_i
    incl = (lane % HD) <= t_i
    zeros = jnp.zeros((C, LANES), F32)

    def body(ci, carry):
        rows = pl.ds(pl.multiple_of(ci * C, C), C)
        cum_all = jnp.dot(tri, lw_ref[rows, :], preferred_element_type=F32, precision=lax.Precision.HIGHEST)
        pairs = range(RWKV_PAIRS)
        sls = [slice(p * LANES, (p + 1) * LANES) for p in pairs]
        cum = [cum_all[:, sl] for sl in sls]
        be = [be_ref[rows, sl] for sl in sls]
        kk = [k_ref[rows, sl] for sl in sls]
        vv = [v_ref[rows, sl] for sl in sls]
        g_inv = [jnp.exp(-cum[p]) for p in pairs]
        ar = [jnp.concatenate([al_ref[rows, sls[p]] * jnp.exp(cum[p] - lw_ref[rows, sls[p]]),
                               r_ref[rows, sls[p]] * jnp.exp(cum[p])], axis=0) for p in pairs]
        bt = [be[p] * g_inv[p] for p in pairs]
        kt = [kk[p] * g_inv[p] for p in pairs]
        x0 = [_mm3_nt(ar[p], jnp.concatenate([jnp.where(lo_half, bt[p], 0.0), jnp.where(lo_half, kt[p], 0.0)],
                                             axis=0)) for p in pairs]
        x1 = [_mm3_nt(ar[p], jnp.concatenate([jnp.where(hi_half, kt[p], 0.0), jnp.where(hi_half, bt[p], 0.0)],
                                             axis=0)) for p in pairs]
        nb = [jnp.concatenate([jnp.where(lo_half & strict, x0[p][:C], 0.0),
                               jnp.where(hi_half & strict, x1[p][:C], 0.0)], axis=0) for p in pairs]
        sbd = [s_ref[p] for p in pairs]

        def read_state(p):
            s_hi, s_lo = _hi_lo(sbd[p])
            ar_b = ar[p].astype(BF16)
            return _dot_nt(ar_b, s_hi) + _dot_nt(ar_b, s_lo)

        ars = [read_state(p) for p in pairs]
        rhs = [ars[p][:C] + jnp.where(
            lo_half,
            _mm(jnp.where(hi_half & strict, x0[p][:C], 0.0), jnp.concatenate([zeros, vv[p]], axis=0)),
            _mm(jnp.where(lo_half & strict, x1[p][:C], 0.0), jnp.concatenate([vv[p], zeros], axis=0)))
            for p in pairs]
        rhs2 = [jnp.concatenate([rhs[p], rhs[p]], axis=0) for p in pairs]
        nd = [jnp.where(diag_blk, nb[p], 0.0) for p in pairs]
        loff = [nb[p] - nd[p] for p in pairs]
        pm = [eye + nd[p] for p in pairs]
        n2 = [_mm3(nd[p], nd[p]) for p in pairs]
        pm = [pm[p] + _mm3(pm[p], n2[p]) for p in pairs]
        n4 = [_mm3(n2[p], n2[p]) for p in pairs]
        pm = [pm[p] + _mm3(pm[p], n4[p]) for p in pairs]
        n8 = [_mm3(n4[p], n4[p]) for p in pairs]
        td = [pm[p] + _mm3(pm[p], n8[p]) for p in pairs]
        m1 = [_mm3(td[p], loff[p]) for p in pairs]
        m2 = [_mm3(m1[p], m1[p]) for p in pairs]
        x_a = [_mm3(td[p], rhs2[p]) for p in pairs]
        x_b = [x_a[p] + _mm3(m2[p], x_a[p]) for p in pairs]
        u2 = [x_b[p] + _mm3(m1[p], x_b[p]) for p in pairs]
        up = [jnp.where(lo_half, u2[p][:C], u2[p][C:]) for p in pairs]
        for p in pairs:
            uv = jnp.concatenate([up[p], vv[p]], axis=0)
            vu = jnp.concatenate([vv[p], up[p]], axis=0)
            y_ref[rows, sls[p]] = ars[p][C:] + jnp.where(lo_half, _mm(jnp.where(incl, x0[p][C:], 0.0), uv),
                                                         _mm(jnp.where(incl, x1[p][C:], 0.0), vu))
            cum_c = cum[p][C - 1:C, :]
            g_end = jnp.exp(cum_c - cum[p])
            bkh = jnp.concatenate([be[p] * g_end, kk[p] * g_end], axis=0)
            upd = _dot_tn(uv.astype(BF16), bkh.astype(BF16))
            s_ref[p] = sbd[p] * jnp.exp(cum_c) + jnp.where(same_head, upd, 0.0)
        return carry

    lax.fori_loop(0, nchunks, body, 0)


def _rwkv_scan(r, lw, k, v, al, be, s0, bsz, tc):
    n, d = r.shape
    t_total = n // bsz
    nt = t_total // tc
    kern = functools.partial(_rwkv_chunk_kernel, nchunks=tc // RWKV_CHUNK)
    row = pl.BlockSpec((tc, d), lambda b, c: (b * nt + c, 0))
    st = pl.BlockSpec((None, RWKV_PAIRS, LANES, LANES), lambda b, c: (b, 0, 0, 0))
    return pl.pallas_call(
        kern,
        grid=(bsz, nt),
        in_specs=[row] * 6 + [st],
        out_specs=[row, st],
        out_shape=[jax.ShapeDtypeStruct((n, d), F32),
                   jax.ShapeDtypeStruct((bsz, RWKV_PAIRS, LANES, LANES), F32)],
        compiler_params=_cparams("parallel", "arbitrary"),
    )(r, lw, k, v, al, be, s0)


def _rwkv_post_kernel(y_ref, r_ref, k_ref, v_ref, g_ref, lng_ref, lnb_ref, rk_ref, seg2_ref, segt2_ref, o_ref):
    seg2, seg_t2 = seg2_ref[...], segt2_ref[...]
    y = y_ref[...]
    inv = 1.0 / RWKV_HEAD_DIM
    dlt = y - _seg_sum_bcast(y, seg2, seg_t2) * inv
    var = _seg_sum_bcast(dlt * dlt, seg2, seg_t2) * inv
    yn = dlt * lax.rsqrt(var + RWKV_LN_EPS) * lng_ref[...] + lnb_ref[...]
    bonus = _seg_sum_bcast(r_ref[...] * k_ref[...] * rk_ref[...], seg2, seg_t2) * v_ref[...]
    o_ref[...] = ((yn + bonus) * g_ref[...]).astype(BF16)


def _rwkv_post(y, r, k, v, g, p, tm):
    n, d = y.shape
    row = pl.BlockSpec((tm, d), lambda i: (i, 0))
    return pl.pallas_call(
        _rwkv_post_kernel,
        grid=(n // tm,),
        in_specs=[row] * 5 + [_full((1, d))] * 3 + [_full((2 * d, LANES)), _full((2 * LANES, d))],
        out_specs=row,
        out_shape=jax.ShapeDtypeStruct((n, d), BF16),
        compiler_params=_cparams("parallel"),
    )(y, r, k, v, g, p["rwkv_ln_g"], p["rwkv_ln_b"], p["rwkv_r_k"], p["seg2"], p["seg_t2"])


def _mla_prep_kernel(u_ref, cos_ref, s1_ref, s2_ref, qag_ref, wqn_ref, wqr_ref, kvg_ref, wkb_ref, wvb_ref,
                     gqn_ref, gqr_ref, gkn_ref, gkr_ref, q_ref, k_ref, v_ref, ckv_ref, kpe_ref):
    u = u_ref[...]
    cq = u[:, :MLA_Q_RANK]
    ckv_raw = u[:, MLA_Q_RANK:MLA_Q_RANK + MLA_KV_RANK]
    kr_raw = u[:, MLA_Q_RANK + MLA_KV_RANK:]
    cos, s1, s2 = cos_ref[...], s1_ref[...], s2_ref[...]

    def rope(blk):
        return (blk * cos + pltpu.roll(blk, LANES - MLA_ROPE // 2, 1) * s1
                + pltpu.roll(blk, MLA_ROPE // 2, 1) * s2)

    cqn = (_rms(cq) * qag_ref[...]).astype(BF16)
    qn = _dot(cqn, wqn_ref[...])
    qr = _dot(cqn, wqr_ref[...])
    ckv = _rms(ckv_raw) * kvg_ref[...]
    ckv_ref[...] = ckv
    kpe = rope(kr_raw)
    kpe_ref[...] = kpe
    ckb = ckv.astype(BF16)
    kn = _dot(ckb, wkb_ref[...])
    v_ref[...] = _dot(ckb, wvb_ref[...]).astype(BF16)
    ss_kpe = jnp.sum(kpe * kpe, axis=-1, keepdims=True)
    gqn, gqr, gkn, gkr = gqn_ref[...], gqr_ref[...], gkn_ref[...], gkr_ref[...]
    for h in range(MLA_HEADS):
        sl = slice(h * MLA_NOPE, (h + 1) * MLA_NOPE)
        o0 = h * MLA_QK_PAD
        qn_h = qn[:, sl]
        qr_h = rope(qr[:, sl])
        ss = jnp.sum(qn_h * qn_h, axis=-1, keepdims=True) + jnp.sum(qr_h * qr_h, axis=-1, keepdims=True)
        rn = lax.rsqrt(ss * (1.0 / MLA_QK) + NORM_EPS) * MLA_SCALE
        q_ref[:, o0:o0 + MLA_NOPE] = (qn_h * rn * gqn).astype(q_ref.dtype)
        q_ref[:, o0 + MLA_NOPE:o0 + MLA_QK_PAD] = (qr_h * rn * gqr).astype(q_ref.dtype)
        kn_h = kn[:, sl]
        ssk = jnp.sum(kn_h * kn_h, axis=-1, keepdims=True) + ss_kpe
        rnk = lax.rsqrt(ssk * (1.0 / MLA_QK) + NORM_EPS)
        k_ref[:, o0:o0 + MLA_NOPE] = (kn_h * rnk * gkn).astype(BF16)
        k_ref[:, o0 + MLA_NOPE:o0 + MLA_QK_PAD] = (kpe * rnk * gkr).astype(BF16)


def _mla_prep(u, rope_tabs, p, tm, q_dtype):
    n = u.shape[0]
    ttab = rope_tabs[0].shape[0]
    ntab = ttab // tm
    hq = MLA_HEADS * MLA_QK_PAD
    hv = MLA_HEADS * MLA_V
    tab = pl.BlockSpec((tm, LANES), lambda i: (i % ntab, 0))
    return pl.pallas_call(
        _mla_prep_kernel,
        grid=(n // tm,),
        in_specs=[pl.BlockSpec((tm, MLA_IN_PAD), lambda i: (i, 0)), tab, tab, tab,
                  _full((1, MLA_Q_RANK)), _full((MLA_Q_RANK, hv)), _full((MLA_Q_RANK, hv)),
                  _full((1, MLA_KV_RANK)), _full((MLA_KV_RANK, hv)), _full((MLA_KV_RANK, hv)),
                  _full((1, LANES)), _full((1, LANES)), _full((1, LANES)), _full((1, LANES))],
        out_specs=[pl.BlockSpec((tm, hq), lambda i: (i, 0)), pl.BlockSpec((tm, hq), lambda i: (i, 0)),
                   pl.BlockSpec((tm, hv), lambda i: (i, 0)),
                   pl.BlockSpec((tm, MLA_KV_RANK), lambda i: (i, 0)),
                   pl.BlockSpec((tm, LANES), lambda i: (i, 0))],
        out_shape=[jax.ShapeDtypeStruct((n, hq), q_dtype), jax.ShapeDtypeStruct((n, hq), BF16),
                   jax.ShapeDtypeStruct((n, hv), BF16), jax.ShapeDtypeStruct((n, MLA_KV_RANK), F32),
                   jax.ShapeDtypeStruct((n, LANES), F32)],
        compiler_params=_cparams("parallel"),
    )(u, *rope_tabs, p["mla_q_a_g"], p["wq_nope"], p["wq_rope"], p["mla_kv_a_g"], p["w_kb"], p["w_vb"],
      p["gq_nope"], p["gq_rope"], p["gk_nope"], p["gk_rope"])


def _flash_kernel(q_ref, k_ref, v_ref, o_ref, m_sc, l_sc, acc_sc, *, tq):
    qi = pl.program_id(1)
    ki = pl.program_id(2)

    @pl.when(ki == 0)
    def _():
        m_sc[...] = jnp.full_like(m_sc, NEG_BIG)
        l_sc[...] = jnp.zeros_like(l_sc)
        acc_sc[...] = jnp.zeros_like(acc_sc)

    def update(diagonal):
        heads = range(MLA_HEADS)
        s = [_dot_nt(q_ref[:, h * MLA_QK_PAD:(h + 1) * MLA_QK_PAD], k_ref[:, h * MLA_QK_PAD:(h + 1) * MLA_QK_PAD])
             for h in heads]
        if diagonal:
            row = lax.broadcasted_iota(jnp.int32, s[0].shape, 0)
            col = lax.broadcasted_iota(jnp.int32, s[0].shape, 1)
            s = [jnp.where(col <= row, s[h], NEG_BIG) for h in heads]
        m_prev = [m_sc[h] for h in heads]
        m_new = [jnp.maximum(m_prev[h], jnp.max(s[h], axis=-1, keepdims=True)) for h in heads]
        pr = [jnp.exp(s[h] - m_new[h]) for h in heads]
        corr = [jnp.exp(m_prev[h] - m_new[h]) for h in heads]
        for h in heads:
            l_sc[h] = l_sc[h] * corr[h] + jnp.sum(pr[h], axis=-1, keepdims=True)
            acc_sc[h] = acc_sc[h] * corr[h] + _dot(pr[h].astype(BF16), v_ref[:, h * MLA_V:(h + 1) * MLA_V])
            m_sc[h] = m_new[h]

    @pl.when(ki < qi)
    def _():
        update(False)

    @pl.when(ki == qi)
    def _():
        update(True)
        for h in range(MLA_HEADS):
            o_ref[:, h * MLA_V:(h + 1) * MLA_V] = (acc_sc[h] / l_sc[h]).astype(BF16)


def _flash(q, k, v, bsz, tq):
    n = q.shape[0]
    nq = n // bsz // tq
    hq = MLA_HEADS * MLA_QK_PAD
    hv = MLA_HEADS * MLA_V
    kern = functools.partial(_flash_kernel, tq=tq)
    return pl.pallas_call(
        kern,
        grid=(bsz, nq, nq),
        in_specs=[pl.BlockSpec((tq, hq), lambda b, i, j: (b * nq + i, 0)),
                  pl.BlockSpec((tq, hq), lambda b, i, j: (b * nq + jnp.minimum(i, j), 0)),
                  pl.BlockSpec((tq, hv), lambda b, i, j: (b * nq + jnp.minimum(i, j), 0))],
        out_specs=pl.BlockSpec((tq, hv), lambda b, i, j: (b * nq + i, 0)),
        out_shape=jax.ShapeDtypeStruct((n, hv), BF16),
        scratch_shapes=[pltpu.VMEM((MLA_HEADS, tq, 1), F32), pltpu.VMEM((MLA_HEADS, tq, 1), F32),
                        pltpu.VMEM((MLA_HEADS, tq, MLA_V), F32)],
        compiler_params=_cparams("parallel", "parallel", "arbitrary"),
    )(q, k, v)


PAGES_PER_STEP = 16
PAGED_ROWS = MLA_HEADS * SAMPLE_T_PAD


def _paged_kernel(pt_ref, q_ref, *refs, t_new, npp):
    lat_refs = refs[:npp]
    rope_refs = refs[npp:2 * npp]
    (newc_ref, newr_ref, wkb_ref, wvb_ref, gkn_ref, gkr_ref, sel_ref, o_ref,
     m_sc, l_sc, acc_sc, qa_sc, qr_sc, kr_sc) = refs[2 * npp:]
    step = pl.program_id(1)
    wkb = wkb_ref[...]

    @pl.when(step == 0)
    def _():
        m_sc[...] = jnp.full_like(m_sc, NEG_BIG)
        l_sc[...] = jnp.zeros_like(l_sc)
        acc_sc[...] = jnp.zeros_like(acc_sc)
        kr_sc[...] = jnp.zeros_like(kr_sc)
        q = q_ref[...]
        gkn, gkr = gkn_ref[...], gkr_ref[...]
        for h in range(MLA_HEADS):
            o0 = h * MLA_QK_PAD
            rows = slice(h * SAMPLE_T_PAD, (h + 1) * SAMPLE_T_PAD)
            qn = (q[:, o0:o0 + MLA_NOPE] * gkn).astype(BF16)
            qa_sc[rows, :] = _dot_nt(qn, wkb[:, h * MLA_NOPE:(h + 1) * MLA_NOPE]).astype(BF16)
            qr_sc[rows, :] = (q[:, o0 + MLA_NOPE:o0 + MLA_QK_PAD] * gkr).astype(BF16)

    qa = qa_sc[...]
    qr = qr_sc[...]
    sel = sel_ref[...]

    def scores(cb, kr_t):
        kn = _dot(cb, wkb)
        ss = _dot_nt(sel, (kn * kn).astype(BF16)) + jnp.sum(kr_t * kr_t, axis=0, keepdims=True)
        rn = lax.rsqrt(ss * (1.0 / MLA_QK) + NORM_EPS)
        return (_dot_nt(qa, cb) + _dot(qr, kr_t.astype(BF16))) * rn

    def softmax_update(s_list, cb_list):
        m_prev = m_sc[...]
        m_new = m_prev
        for s in s_list:
            m_new = jnp.maximum(m_new, jnp.max(s, axis=-1, keepdims=True))
        corr = jnp.exp(m_prev - m_new)
        l_new = l_sc[...] * corr
        acc = acc_sc[...] * corr
        for s, cb in zip(s_list, cb_list):
            pr = jnp.exp(s - m_new)
            l_new = l_new + jnp.sum(pr, axis=-1, keepdims=True)
            acc = acc + _dot(pr.astype(BF16), cb)
        l_sc[...] = l_new
        acc_sc[...] = acc
        m_sc[...] = m_new

    cbs, ss_ = [], []
    for i in range(npp):
        kr_sc[i, 0:MLA_ROPE, :] = rope_refs[i][...]
        cbs.append(lat_refs[i][...].astype(BF16))
        ss_.append(scores(cbs[i], kr_sc[i]))
    softmax_update(ss_, cbs)

    @pl.when(step == pl.num_programs(1) - 1)
    def _():
        cb = newc_ref[...].astype(BF16)
        s = scores(cb, newr_ref[...])
        key = lax.broadcasted_iota(jnp.int32, s.shape, 1)
        qry = lax.broadcasted_iota(jnp.int32, s.shape, 0) % SAMPLE_T_PAD
        softmax_update([jnp.where((key <= qry) & (key < t_new), s, NEG_BIG)], [cb])
        o_lat = (acc_sc[...] / l_sc[...]).astype(BF16)
        wvb = wvb_ref[...]
        for h in range(MLA_HEADS):
            o_ref[:, h * MLA_V:(h + 1) * MLA_V] = _dot(
                o_lat[h * SAMPLE_T_PAD:(h + 1) * SAMPLE_T_PAD],
                wvb[:, h * MLA_V:(h + 1) * MLA_V])


def _paged_attention(page_table, q, lat_pool, rope_pool_t, layer, new_c, new_r_t, p, t_new):
    bsz, n_pages = page_table.shape
    npp = math.gcd(PAGES_PER_STEP, n_pages)
    nsteps = n_pages // npp
    hq = MLA_HEADS * MLA_QK_PAD
    hv = MLA_HEADS * MLA_V

    def lat_spec(i):
        return pl.BlockSpec((None, None, PAGE_SIZE, MLA_KV_RANK),
                            lambda b, s, pt: (layer, pt[b, s * npp + i], 0, 0))

    def rope_spec(i):
        return pl.BlockSpec((None, None, MLA_ROPE, PAGE_SIZE),
                            lambda b, s, pt: (layer, pt[b, s * npp + i], 0, 0))

    grid_spec = pltpu.PrefetchScalarGridSpec(
        num_scalar_prefetch=1,
        grid=(bsz, nsteps),
        in_specs=[pl.BlockSpec((SAMPLE_T_PAD, hq), lambda b, s, pt: (b, 0))]
                 + [lat_spec(i) for i in range(npp)] + [rope_spec(i) for i in range(npp)]
                 + [pl.BlockSpec((None, PAGE_SIZE, MLA_KV_RANK), lambda b, s, pt: (b, 0, 0)),
                    pl.BlockSpec((None, LANES, PAGE_SIZE), lambda b, s, pt: (b, 0, 0)),
                    pl.BlockSpec((MLA_KV_RANK, hv), lambda b, s, pt: (0, 0)),
                    pl.BlockSpec((MLA_KV_RANK, hv), lambda b, s, pt: (0, 0)),
                    pl.BlockSpec((1, LANES), lambda b, s, pt: (0, 0)),
                    pl.BlockSpec((1, LANES), lambda b, s, pt: (0, 0)),
                    pl.BlockSpec((PAGED_ROWS, hv), lambda b, s, pt: (0, 0))],
        out_specs=pl.BlockSpec((SAMPLE_T_PAD, hv), lambda b, s, pt: (b, 0)),
        scratch_shapes=[pltpu.VMEM((PAGED_ROWS, 1), F32),
                        pltpu.VMEM((PAGED_ROWS, 1), F32),
                        pltpu.VMEM((PAGED_ROWS, MLA_KV_RANK), F32),
                        pltpu.VMEM((PAGED_ROWS, MLA_KV_RANK), BF16),
                        pltpu.VMEM((PAGED_ROWS, LANES), BF16),
                        pltpu.VMEM((npp, LANES, PAGE_SIZE), F32)],
    )
    kern = functools.partial(_paged_kernel, t_new=t_new, npp=npp)
    return pl.pallas_call(
        kern,
        grid_spec=grid_spec,
        out_shape=jax.ShapeDtypeStruct((bsz * SAMPLE_T_PAD, hv), F32),
        compiler_params=_cparams("parallel", "arbitrary"),
    )(page_table, q, *([lat_pool] * npp), *([rope_pool_t] * npp), new_c, new_r_t,
      p["w_kb"], p["w_vb"], p["gk_nope"], p["gk_rope"], p["head_sel"])


def _merge_kernel(x_ref, gl_ref, os_ref, or_ref, om_ref, ws_ref, wr_ref, wm_ref, wo_ref, o_ref):
    gl = gl_ref[...]
    d = D_MODEL
    merged = (_sigmoid(gl[:, :d]) * _dot(os_ref[...], ws_ref[...])
              + _sigmoid(gl[:, d:2 * d]) * _dot(or_ref[...], wr_ref[...])
              + _sigmoid(gl[:, 2 * d:]) * _dot(om_ref[...].astype(BF16), wm_ref[...]))
    o_ref[...] = x_ref[...] + _dot(merged.astype(BF16), wo_ref[...])


def _merge(x, gl, o_ssd, o_rwkv, o_mla, p, tm):
    n, d = x.shape
    row = pl.BlockSpec((tm, d), lambda i: (i, 0))
    return pl.pallas_call(
        _merge_kernel,
        grid=(n // tm,),
        in_specs=[row, pl.BlockSpec((tm, N_BRANCH * d), lambda i: (i, 0)), row, row, row]
                 + [_full((d, d))] * 4,
        out_specs=row,
        out_shape=jax.ShapeDtypeStruct((n, d), F32),
        compiler_params=_cparams("parallel"),
    )(x, gl, o_ssd, o_rwkv, o_mla, p["w_o_ssd"], p["w_o_rwkv"], p["w_o_mla"], p["w_out"])


def _ffn_kernel(x_ref, tail0_ref, g_ref, wup_ref, cw_ref, cb_ref, wdn_ref, o_ref, tail_ref, buf, *, tm, t_valid):
    j = pl.program_id(1)

    @pl.when(j == 0)
    def _():
        buf[0:SUBLANES, :] = tail0_ref[...]

    x = x_ref[...]
    h2 = (_rms(x) * g_ref[...]).astype(BF16)
    buf[SUBLANES:SUBLANES + tm, :] = _dot(h2, wup_ref[...])
    conv = cb_ref[...]
    for i in range(FFN_CONV):
        conv = conv + buf[pl.ds(SUBLANES - (FFN_CONV - 1) + i, tm), :] * cw_ref[i:i + 1, :]
    tail = buf[pl.ds(t_valid, SUBLANES), :]
    tail_ref[...] = tail
    buf[0:SUBLANES, :] = tail
    act = (_silu(conv[:, :D_FF]) * conv[:, D_FF:]).astype(BF16)
    o_ref[...] = x + _dot(act, wdn_ref[...])


def _ffn(x, tail0, p, tm, t_valid):
    bsz = tail0.shape[0]
    n, d = x.shape
    nt = n // bsz // tm
    kern = functools.partial(_ffn_kernel, tm=tm, t_valid=t_valid)
    return pl.pallas_call(
        kern,
        grid=(bsz, nt),
        in_specs=[pl.BlockSpec((tm, d), lambda b, j: (b * nt + j, 0)),
                  pl.BlockSpec((None, SUBLANES, 2 * D_FF), lambda b, j: (b, 0, 0)),
                  _full((1, d)),
                  pl.BlockSpec((d, 2 * D_FF), lambda b, j: (0, 0), pipeline_mode=pl.Buffered(1)),
                  _full((FFN_CONV, 2 * D_FF)), _full((1, 2 * D_FF)),
                  pl.BlockSpec((D_FF, d), lambda b, j: (0, 0), pipeline_mode=pl.Buffered(1))],
        out_specs=[pl.BlockSpec((tm, d), lambda b, j: (b * nt + j, 0)),
                   pl.BlockSpec((None, SUBLANES, 2 * D_FF), lambda b, j: (b, 0, 0))],
        out_shape=[jax.ShapeDtypeStruct((n, d), F32),
                   jax.ShapeDtypeStruct((bsz, SUBLANES, 2 * D_FF), F32)],
        scratch_shapes=[pltpu.VMEM((SUBLANES + tm, 2 * D_FF), F32)],
        compiler_params=_cparams("parallel", "arbitrary"),
    )(x, tail0, p["norm_ffn_g"], p["ffn_w_up"], p["ffn_conv_w"], p["ffn_conv_b"], p["ffn_w_down"])


def _pad_cols(w, width):
    return jnp.pad(w, ((0, 0), (0, width - w.shape[1])))


def _row(v):
    return v.reshape(1, -1)


def _prep_layer(i, w):
    d = RWKV_DIM
    w_in = w["w_in"][i]
    o_r = SSD_IN
    o_m = SSD_IN + RWKV_IN
    p = {}
    p["norm_mix_g"] = _row(w["norm_mix_g"][i])
    p["w_ssd"] = _pad_cols(w_in[:, :o_r], SSD_IN_PAD).astype(BF16)
    p["w_rwkv"] = w_in[:, o_r:o_m].astype(BF16)
    p["w_mla"] = _pad_cols(w_in[:, o_m:], MLA_IN_PAD).astype(BF16)
    p["w_gate"] = w["w_gate"][i].astype(BF16)
    p["b_gate"] = _row(w["b_gate"][i])
    p["ssd_conv_w"] = w["ssd_conv_w"][i]
    p["ssd_conv_b"] = _row(w["ssd_conv_b"][i])
    p["ssd_dt_bias"] = _pad_cols(_row(w["ssd_dt_bias"][i]), LANES)
    p["ssd_a_log"] = _pad_cols(_row(w["ssd_a_log"][i]), LANES)
    p["ssd_d"] = _row(jnp.repeat(w["ssd_d"][i], SSD_HEAD_DIM))
    p["ssd_norm_g"] = _row(w["ssd_norm_g"][i])
    p["rwkv_shift_mu"] = _row(w["rwkv_shift_mu"][i])
    p["rwkv_w0"] = _row(w["rwkv_w0"][i])
    zeros_lora = jnp.zeros((RWKV_DECAY_LORA, d), F32)
    p["rwkv_w_up"] = jnp.concatenate([w["rwkv_w_up"][i], zeros_lora], axis=0).astype(BF16)
    p["rwkv_a0"] = _row(w["rwkv_a0"][i])
    p["rwkv_a_up"] = jnp.concatenate([zeros_lora, w["rwkv_a_up"][i]], axis=0).astype(BF16)
    p["rwkv_g_up"] = w["rwkv_g_up"][i].astype(BF16)
    p["rwkv_k_k"] = _row(w["rwkv_k_k"][i])
    p["rwkv_k_a"] = _row(w["rwkv_k_a"][i])
    p["rwkv_r_k"] = _row(w["rwkv_r_k"][i])
    p["rwkv_ln_g"] = _row(w["rwkv_ln_g"][i])
    p["rwkv_ln_b"] = _row(w["rwkv_ln_b"][i])
    wq = w["mla_w_q_b"][i]
    hv = MLA_HEADS * MLA_NOPE
    p["mla_q_a_g"] = _row(w["mla_q_a_g"][i])
    p["wq_nope"] = wq[:, :, :MLA_NOPE].reshape(MLA_Q_RANK, hv).astype(BF16)
    p["wq_rope"] = jnp.pad(wq[:, :, MLA_NOPE:], ((0, 0), (0, 0), (0, MLA_NOPE - MLA_ROPE))
                           ).reshape(MLA_Q_RANK, hv).astype(BF16)
    p["mla_kv_a_g"] = _row(w["mla_kv_a_g"][i])
    p["w_kb"] = w["mla_w_kb"][i].reshape(MLA_KV_RANK, hv).astype(BF16)
    p["w_vb"] = w["mla_w_vb"][i].reshape(MLA_KV_RANK, MLA_HEADS * MLA_V).astype(BF16)
    gq, gk = w["mla_q_norm_g"][i], w["mla_k_norm_g"][i]
    p["gq_nope"] = _row(gq[:MLA_NOPE])
    p["gq_rope"] = _pad_cols(_row(gq[MLA_NOPE:]), LANES)
    p["gk_nope"] = _row(gk[:MLA_NOPE])
    p["gk_rope"] = _pad_cols(_row(gk[MLA_NOPE:]), LANES)
    for name in ("w_o_ssd", "w_o_rwkv", "w_o_mla", "w_out"):
        p[name] = w[name][i].astype(BF16)
    p["norm_ffn_g"] = _row(w["norm_ffn_g"][i])
    p["ffn_w_up"] = w["ffn_w_up"][i].astype(BF16)
    p["ffn_conv_w"] = w["ffn_conv_w"][i]
    p["ffn_conv_b"] = _row(w["ffn_conv_b"][i])
    p["ffn_w_down"] = w["ffn_w_down"][i].astype(BF16)
    return p


def _constants():
    ch = np.arange(RWKV_DIM)
    seg = (ch[:, None] // RWKV_HEAD_DIM == np.arange(LANES)[None, :]).astype(np.float32)
    seg2 = np.concatenate([seg, seg], axis=0)
    seg_t2 = np.concatenate([seg.T, seg.T], axis=0)
    head_sel = (np.arange(PAGED_ROWS)[:, None] // SAMPLE_T_PAD
                == np.arange(MLA_HEADS * MLA_NOPE)[None, :] // MLA_NOPE).astype(np.float32)
    return jnp.asarray(seg2, BF16), jnp.asarray(seg_t2, BF16), jnp.asarray(head_sel, BF16)


def _rope_tables(pos):
    half = MLA_ROPE // 2
    inv = ROPE_BASE ** (-jnp.arange(half, dtype=F32) / half)
    ang = pos.astype(F32)[:, None] * inv[None, :]
    cos, sin = jnp.cos(ang), jnp.sin(ang)
    z = jnp.zeros_like(cos)
    z2 = jnp.zeros((pos.shape[0], LANES - MLA_ROPE), F32)
    return (jnp.concatenate([cos, cos, z2], axis=1),
            jnp.concatenate([-sin, z, z2], axis=1),
            jnp.concatenate([z, sin, z2], axis=1))


def _tail_block(state, rows):
    return jnp.pad(state, ((0, 0), (SUBLANES - rows, 0), (0, 0)))


def _pack_rwkv_state(s):
    b = s.shape[0]
    s5 = s.reshape(b, RWKV_PAIRS, 2, RWKV_HEAD_DIM, RWKV_HEAD_DIM)
    z = jnp.zeros_like(s5[:, :, 0])
    top = jnp.concatenate([s5[:, :, 0], z], axis=-1)
    bot = jnp.concatenate([z, s5[:, :, 1]], axis=-1)
    return jnp.concatenate([top, bot], axis=-2)


def _unpack_rwkv_state(s):
    b = s.shape[0]
    hd = RWKV_HEAD_DIM
    return jnp.stack([s[:, :, :hd, :hd], s[:, :, hd:, hd:]], axis=2).reshape(b, RWKV_HEADS, hd, hd)


def _layer(x, st, p, cfg, attend):
    bsz, t, t_valid, tm = cfg["bsz"], cfg["t"], cfg["t_valid"], cfg["tm"]
    n = x.shape[0]
    tmn = min(tm, n)
    zb = lambda w_: jnp.zeros((1, w_), F32)
    g = p["norm_mix_g"]
    u_ssd = _norm_matmul(x, g, p["w_ssd"], zb(SSD_IN_PAD), tmn)
    u_rwkv = _norm_matmul(x, g, p["w_rwkv"], zb(RWKV_IN), tmn)
    u_mla = _norm_matmul(x, g, p["w_mla"], zb(MLA_IN_PAD), tmn)
    gl = _norm_matmul(x, g, p["w_gate"], p["b_gate"], tmn)

    if t % SSD_CHUNK:
        u_pad = jnp.pad(u_ssd.reshape(bsz, t, SSD_IN_PAD), ((0, 0), (0, SSD_CHUNK - t), (0, 0)))
        o_ssd, ssd_tail, ssd_h = _ssd(u_pad.reshape(bsz * SSD_CHUNK, SSD_IN_PAD), st["ssd_tail"], st["ssd_h"],
                                      p, t_valid)
        o_ssd = o_ssd.reshape(bsz, SSD_CHUNK, SSD_INNER)[:, :t].reshape(n, SSD_INNER)
    else:
        o_ssd, ssd_tail, ssd_h = _ssd(u_ssd, st["ssd_tail"], st["ssd_h"], p, SSD_CHUNK)

    tm_r = min(tm, t)
    r, lw, k2, v, al, be, gg, rwkv_tail = _rwkv_pre(u_rwkv, st["rwkv_tail"], p, tm_r, min(t_valid, tm_r))
    d = RWKV_DIM
    if t % RWKV_CHUNK:
        def chunk_pad(a_):
            a_ = a_.reshape(bsz, t, d)[:, :t_valid]
            return jnp.pad(a_, ((0, 0), (0, RWKV_CHUNK - t_valid), (0, 0))).reshape(bsz * RWKV_CHUNK, d)
        yy, rwkv_s = _rwkv_scan(*(chunk_pad(a_) for a_ in (r, lw, k2, v, al, be)), st["rwkv_s"], bsz, RWKV_CHUNK)
        yy = yy.reshape(bsz, RWKV_CHUNK, d)[:, :t].reshape(n, d)
    else:
        yy, rwkv_s = _rwkv_scan(r, lw, k2, v, al, be, st["rwkv_s"], bsz, min(t, cfg["tc_rwkv"]))
    o_rwkv = _rwkv_post(yy, r, k2, v, gg, p, tmn)

    q, k, vv, ckv, kpe = _mla_prep(u_mla, cfg["rope"], p, tmn, cfg["q_dtype"])
    o_mla = attend(q, k, vv, ckv, kpe)

    x = _merge(x, gl, o_ssd, o_rwkv, o_mla, p, tmn)
    tm_f = min(cfg["tm_ffn"], t)
    x, ffn_tail = _ffn(x, st["ffn_tail"], p, tm_f, min(t_valid, tm_f))
    new_st = dict(ssd_tail=ssd_tail, ssd_h=ssd_h, rwkv_tail=rwkv_tail, rwkv_s=rwkv_s, ffn_tail=ffn_tail)
    return x, new_st, ckv, kpe


def kernel(x_prompt, x_sample, cache_kv_latent, cache_k_rope, page_table, state_ssm, state_ssm_conv,
           state_rwkv, state_rwkv_shift, state_ffn_conv, norm_mix_g, w_in, ssd_conv_w, ssd_conv_b,
           ssd_dt_bias, ssd_a_log, ssd_d, ssd_norm_g, rwkv_shift_mu, rwkv_w0, rwkv_w_up, rwkv_a0,
           rwkv_a_up, rwkv_g_up, rwkv_k_k, rwkv_k_a, rwkv_r_k, rwkv_ln_g, rwkv_ln_b, mla_q_a_g,
           mla_w_q_b, mla_kv_a_g, mla_w_kb, mla_w_vb, mla_q_norm_g, mla_k_norm_g, w_gate, b_gate,
           w_o_ssd, w_o_rwkv, w_o_mla, w_out, norm_ffn_g, ffn_w_up, ffn_conv_w, ffn_conv_b, ffn_w_down):
    w = dict(norm_mix_g=norm_mix_g, w_in=w_in, ssd_conv_w=ssd_conv_w, ssd_conv_b=ssd_conv_b,
             ssd_dt_bias=ssd_dt_bias, ssd_a_log=ssd_a_log, ssd_d=ssd_d, ssd_norm_g=ssd_norm_g,
             rwkv_shift_mu=rwkv_shift_mu, rwkv_w0=rwkv_w0, rwkv_w_up=rwkv_w_up, rwkv_a0=rwkv_a0,
             rwkv_a_up=rwkv_a_up, rwkv_g_up=rwkv_g_up, rwkv_k_k=rwkv_k_k, rwkv_k_a=rwkv_k_a,
             rwkv_r_k=rwkv_r_k, rwkv_ln_g=rwkv_ln_g, rwkv_ln_b=rwkv_ln_b, mla_q_a_g=mla_q_a_g,
             mla_w_q_b=mla_w_q_b, mla_kv_a_g=mla_kv_a_g, mla_w_kb=mla_w_kb, mla_w_vb=mla_w_vb,
             mla_q_norm_g=mla_q_norm_g, mla_k_norm_g=mla_k_norm_g, w_gate=w_gate, b_gate=b_gate,
             w_o_ssd=w_o_ssd, w_o_rwkv=w_o_rwkv, w_o_mla=w_o_mla, w_out=w_out, norm_ffn_g=norm_ffn_g,
             ffn_w_up=ffn_w_up, ffn_conv_w=ffn_conv_w, ffn_conv_b=ffn_conv_b, ffn_w_down=ffn_w_down)
    depth = w_in.shape[0]
    bp, tp, d = x_prompt.shape
    bs, ts, _ = x_sample.shape
    tsp = SAMPLE_T_PAD
    past_len = page_table.shape[1] * PAGE_SIZE
    seg2, seg_t2, head_sel = _constants()
    rope_pool_t = jnp.swapaxes(cache_k_rope, 2, 3)

    tm_p = min(256, tp)
    cfg_p = dict(bsz=bp, t=tp, t_valid=tp, tm=tm_p, tm_ffn=min(128, tp), tc_rwkv=256,
                 rope=_rope_tables(jnp.arange(tp)), q_dtype=BF16)
    pos_s = past_len + jnp.arange(tsp)
    rope_s = tuple(jnp.tile(tb, (bs, 1)) for tb in _rope_tables(pos_s))
    cfg_s = dict(bsz=bs, t=tsp, t_valid=ts, tm=bs * tsp, tm_ffn=tsp, tc_rwkv=RWKV_CHUNK, rope=rope_s,
                 q_dtype=F32)

    xp = x_prompt.reshape(bp * tp, d)
    xs = jnp.pad(x_sample, ((0, 0), (0, tsp - ts), (0, 0))).reshape(bs * tsp, d)

    zero_st = dict(ssd_tail=jnp.zeros((bp, SUBLANES, SSD_CONV_DIM), F32),
                   ssd_h=jnp.zeros((bp, SSD_HEADS // 2, LANES, SSD_STATE), F32),
                   rwkv_tail=jnp.zeros((bp, SUBLANES, RWKV_IN), F32),
                   rwkv_s=jnp.zeros((bp, RWKV_PAIRS, LANES, LANES), F32),
                   ffn_tail=jnp.zeros((bp, SUBLANES, 2 * D_FF), F32))

    new_p = [[] for _ in range(7)]
    new_s = [[] for _ in range(7)]
    for i in range(depth):
        p = _prep_layer(i, w)
        p["seg2"], p["seg_t2"], p["head_sel"] = seg2, seg_t2, head_sel

        def prompt_attend(q, k, v, ckv, kpe):
            return _flash(q, k, v, bp, min(512, tp))

        xp, st_p, ckv_p, kpe_p = _layer(xp, zero_st, p, cfg_p, prompt_attend)

        st_in = dict(ssd_tail=_tail_block(state_ssm_conv[i], SSD_CONV - 1),
                     ssd_h=state_ssm[i].reshape(bs, SSD_HEADS // 2, LANES, SSD_STATE),
                     rwkv_tail=_tail_block(state_rwkv_shift[i][:, None, :], 1),
                     rwkv_s=_pack_rwkv_state(state_rwkv[i]),
                     ffn_tail=_tail_block(state_ffn_conv[i], FFN_CONV - 1))

        def sample_attend(q, k, v, ckv, kpe, i=i, p=p):
            new_c = jnp.pad(ckv.reshape(bs, tsp, MLA_KV_RANK), ((0, 0), (0, PAGE_SIZE - tsp), (0, 0)))
            new_r_t = jnp.pad(jnp.swapaxes(kpe.reshape(bs, tsp, LANES), 1, 2), ((0, 0), (0, 0), (0, PAGE_SIZE - tsp)))
            return _paged_attention(page_table, q, cache_kv_latent, rope_pool_t, i, new_c, new_r_t, p, ts)

        xs, st_s, ckv_s, kpe_s = _layer(xs, st_in, p, cfg_s, sample_attend)

        for lst, st, ckv, kpe, b_, t_, tv in ((new_p, st_p, ckv_p, kpe_p, bp, tp, tp),
                                              (new_s, st_s, ckv_s, kpe_s, bs, tsp, ts)):
            lst[0].append(ckv.reshape(b_, t_, MLA_KV_RANK)[:, :tv])
            lst[1].append(kpe.reshape(b_, t_, LANES)[:, :tv, :MLA_ROPE])
            lst[2].append(st["ssd_h"].reshape(b_, SSD_HEADS, SSD_HEAD_DIM, SSD_STATE))
            lst[3].append(st["ssd_tail"][:, SUBLANES - (SSD_CONV - 1):])
            lst[4].append(_unpack_rwkv_state(st["rwkv_s"]))
            lst[5].append(st["rwkv_tail"][:, SUBLANES - 1])
            lst[6].append(st["ffn_tail"][:, SUBLANES - (FFN_CONV - 1):])

    outs_p = [jnp.stack(v_, axis=0) for v_ in new_p]
    outs_s = [jnp.stack(v_, axis=0) for v_ in new_s]
    y_p = xp.reshape(bp, tp, d)
    y_s = xs.reshape(bs, tsp, d)[:, :ts]
    return (y_p, y_s, *outs_p, *outs_s)
```

```python
import functools
import math

import numpy as np
import jax
import jax.numpy as jnp
from jax import lax
from jax.experimental import pallas as pl
from jax.experimental.pallas import tpu as pltpu

F32 = jnp.float32
BF16 = jnp.bfloat16

D_MODEL = 1024
PAGE_SIZE = 128

SSD_HEAD_DIM = 64
SSD_INNER = 1024
SSD_HEADS = 16
SSD_GROUPS = 2
SSD_STATE = 128
SSD_CONV = 4
SSD_CHUNK = 128
SSD_CONV_DIM = SSD_INNER + 2 * SSD_GROUPS * SSD_STATE
SSD_IN = SSD_INNER + SSD_CONV_DIM + SSD_HEADS
SSD_IN_PAD = SSD_INNER + SSD_CONV_DIM + 128

RWKV_HEAD_DIM = 64
RWKV_DIM = 1024
RWKV_HEADS = 16
RWKV_DECAY_LORA = 64
RWKV_A_LORA = 64
RWKV_GATE_LORA = 128
RWKV_IN = 3 * RWKV_DIM + RWKV_DECAY_LORA + RWKV_A_LORA + RWKV_GATE_LORA
RWKV_LN_EPS = 64e-5

MLA_HEADS = 8
MLA_NOPE = 128
MLA_ROPE = 64
MLA_QK = MLA_NOPE + MLA_ROPE
MLA_V = 128
MLA_Q_RANK = 512
MLA_KV_RANK = 256
MLA_IN = MLA_Q_RANK + MLA_KV_RANK + MLA_ROPE
MLA_IN_PAD = MLA_Q_RANK + MLA_KV_RANK + 128
MLA_QK_PAD = 256
ROPE_BASE = 10000.0
MLA_SCALE = MLA_QK ** -0.5

N_BRANCH = 3
D_FF = 2816
FFN_CONV = 3
NORM_EPS = 1e-6

SUBLANES = 8
LANES = 128
SAMPLE_T_PAD = 8
NEG_BIG = -1e30

VMEM_LIMIT = 56 * 1024 * 1024


def _cparams(*sem):
    return pltpu.CompilerParams(dimension_semantics=sem, vmem_limit_bytes=VMEM_LIMIT)


def _full(shape):
    nd = len(shape)
    return pl.BlockSpec(shape, lambda *_: (0,) * nd)


def _rms(x, eps=NORM_EPS):
    return x * lax.rsqrt(jnp.mean(x * x, axis=-1, keepdims=True) + eps)


def _sigmoid(x):
    return 1.0 / (1.0 + jnp.exp(-x))


def _silu(x):
    return x * _sigmoid(x)


def _softplus(x):
    return jnp.maximum(x, 0.0) + jnp.log(1.0 + jnp.exp(-jnp.abs(x)))


def _dot(a, b):
    return jnp.dot(a, b, preferred_element_type=F32)


def _dot_nt(a, b):
    return lax.dot_general(a, b, (((1,), (1,)), ((), ())), preferred_element_type=F32)


def _dot_tn(a, b):
    return lax.dot_general(a, b, (((0,), (0,)), ((), ())), preferred_element_type=F32)


def _split_hi_lo(x):
    hi = x.astype(BF16)
    lo = (x - hi.astype(F32)).astype(BF16)
    return jnp.concatenate([hi, lo], axis=1)


def _norm_matmul_kernel(x_ref, g_ref, w_ref, b_ref, o_ref):
    h = (_rms(x_ref[...]) * g_ref[...]).astype(BF16)
    o_ref[...] = _dot(h, w_ref[...]) + b_ref[...]


def _norm_matmul(x, g, w, bias, tm):
    n, d = x.shape
    nout = w.shape[1]
    return pl.pallas_call(
        _norm_matmul_kernel,
        grid=(n // tm,),
        in_specs=[pl.BlockSpec((tm, d), lambda i: (i, 0)), _full((1, d)),
                  _full((d, nout)), _full((1, nout))],
        out_specs=pl.BlockSpec((tm, nout), lambda i: (i, 0)),
        out_shape=jax.ShapeDtypeStruct((n, nout), F32),
        compiler_params=_cparams("parallel"),
    )(x, g, w, bias)


def _ssd_kernel(u_ref, tail0_ref, h0_ref, cw_ref, cb_ref, dtb_ref, alog_ref, dsk_ref, ng_ref,
                o_ref, tail_ref, h_ref, buf, *, t_valid):
    L = SSD_CHUNK
    c = pl.program_id(1)

    @pl.when(c == 0)
    def _():
        buf[0:SUBLANES, :] = tail0_ref[...]
        h_ref[...] = h0_ref[...]

    u = u_ref[...]
    z = u[:, :SSD_INNER]
    dt_raw = u[:, SSD_INNER + SSD_CONV_DIM:]
    buf[SUBLANES:SUBLANES + L, :] = u[:, SSD_INNER:SSD_INNER + SSD_CONV_DIM]
    acc = cb_ref[...]
    for j in range(SSD_CONV):
        acc = acc + buf[pl.ds(SUBLANES - (SSD_CONV - 1) + j, L), :] * cw_ref[j:j + 1, :]
    tail = buf[pl.ds(t_valid, SUBLANES), :]
    tail_ref[...] = tail
    buf[0:SUBLANES, :] = tail
    xbc = _silu(acc)
    xs = xbc[:, :SSD_INNER]

    dt = _softplus(dt_raw + dtb_ref[...])
    if t_valid < L:
        row = lax.broadcasted_iota(jnp.int32, dt.shape, 0)
        dt = jnp.where(row < t_valid, dt, 0.0)
    a = -jnp.exp(alog_ref[...])
    da = dt * a
    ri = lax.broadcasted_iota(jnp.int32, (L, L), 0)
    ci = lax.broadcasted_iota(jnp.int32, (L, L), 1)
    causal = ri >= ci
    tri = jnp.where(causal, 1.0, 0.0).astype(F32)
    cs = jnp.dot(tri, da, preferred_element_type=F32, precision=lax.Precision.HIGHEST)
    cs_t = cs.T
    total = cs[L - 1:L, :]
    etot = jnp.exp(total)
    lane = lax.broadcasted_iota(jnp.int32, (L, LANES), 1)
    lo_half = lane < SSD_HEAD_DIM
    row_lo = lax.broadcasted_iota(jnp.int32, (LANES, LANES), 0) < SSD_HEAD_DIM

    def pair_cols(m, j0):
        return jnp.where(lo_half, m[:, j0:j0 + 1], m[:, j0 + 1:j0 + 2])

    hpg = SSD_HEADS // SSD_GROUPS
    y_parts = []
    for g in range(SSD_GROUPS):
        b_g = xbc[:, SSD_INNER + g * SSD_STATE:SSD_INNER + (g + 1) * SSD_STATE].astype(BF16)
        c0 = SSD_INNER + SSD_GROUPS * SSD_STATE + g * SSD_STATE
        c_g = xbc[:, c0:c0 + SSD_STATE].astype(BF16)
        cb = _dot_nt(c_g, b_g)
        for m in range(hpg // 2):
            q = g * (hpg // 2) + m
            j0 = 2 * q
            xs_p = xs[:, q * LANES:(q + 1) * LANES]
            xd = xs_p * pair_cols(dt, j0)
            xd_b = xd.astype(BF16)
            ys = []
            for e in range(2):
                j = j0 + e
                diff = cs[:, j:j + 1] - cs_t[j:j + 1, :]
                lm = jnp.where(causal, jnp.exp(jnp.where(causal, diff, 0.0)), 0.0)
                ys.append(_dot((cb * lm).astype(BF16), xd_b))
            y_diag = jnp.where(lo_half, ys[0], ys[1])
            hp = h_ref[q]
            cs_p = pair_cols(cs, j0)
            y_off = _dot_nt(c_g, hp.astype(BF16)) * jnp.exp(cs_p)
            tot_p = jnp.where(lo_half[0:1, :], total[:, j0:j0 + 1], total[:, j0 + 1:j0 + 2])
            dte = jnp.exp(tot_p - cs_p)
            contrib = _dot_tn((xd * dte).astype(BF16), b_g)
            decay = jnp.where(row_lo, etot[:, j0:j0 + 1], etot[:, j0 + 1:j0 + 2])
            h_ref[q] = hp * decay + contrib
            y_parts.append(y_diag + y_off + dsk_ref[:, q * LANES:(q + 1) * LANES] * xs_p)
    y = jnp.concatenate(y_parts, axis=1) * _silu(z)
    gw = SSD_INNER // SSD_GROUPS
    outs = [_rms(y[:, g * gw:(g + 1) * gw]) * ng_ref[:, g * gw:(g + 1) * gw] for g in range(SSD_GROUPS)]
    o_ref[...] = jnp.concatenate(outs, axis=1).astype(BF16)


def _ssd(u, tail0, h0, p, t_valid):
    bsz = h0.shape[0]
    n = u.shape[0]
    L = SSD_CHUNK
    nch = n // bsz // L
    kern = functools.partial(_ssd_kernel, t_valid=t_valid)
    return pl.pallas_call(
        kern,
        grid=(bsz, nch),
        in_specs=[pl.BlockSpec((L, SSD_IN_PAD), lambda b, c: (b * nch + c, 0)),
                  pl.BlockSpec((None, SUBLANES, SSD_CONV_DIM), lambda b, c: (b, 0, 0)),
                  pl.BlockSpec((None, SSD_HEADS // 2, LANES, SSD_STATE), lambda b, c: (b, 0, 0, 0)),
                  _full((SSD_CONV, SSD_CONV_DIM)), _full((1, SSD_CONV_DIM)),
                  _full((1, LANES)), _full((1, LANES)), _full((1, SSD_INNER)), _full((1, SSD_INNER))],
        out_specs=[pl.BlockSpec((L, SSD_INNER), lambda b, c: (b * nch + c, 0)),
                   pl.BlockSpec((None, SUBLANES, SSD_CONV_DIM), lambda b, c: (b, 0, 0)),
                   pl.BlockSpec((None, SSD_HEADS // 2, LANES, SSD_STATE), lambda b, c: (b, 0, 0, 0))],
        out_shape=[jax.ShapeDtypeStruct((n, SSD_INNER), BF16),
                   jax.ShapeDtypeStruct((bsz, SUBLANES, SSD_CONV_DIM), F32),
                   jax.ShapeDtypeStruct((bsz, SSD_HEADS // 2, LANES, SSD_STATE), F32)],
        scratch_shapes=[pltpu.VMEM((SUBLANES + L, SSD_CONV_DIM), F32)],
        compiler_params=_cparams("parallel", "arbitrary"),
    )(u, tail0, h0, p["ssd_conv_w"], p["ssd_conv_b"], p["ssd_dt_bias"], p["ssd_a_log"],
      p["ssd_d"], p["ssd_norm_g"])


def _seg_sum_bcast(x, seg2, seg_t2):
    s16 = _dot(_split_hi_lo(x), seg2)
    return _dot(_split_hi_lo(s16), seg_t2)


def _rwkv_pre_kernel(u_ref, tail0_ref, mu_ref, w0_ref, wup_ref, a0_ref, aup_ref, gup_ref, kk_ref,
                     ka_ref, seg2_ref, segt2_ref,
                     r_ref, w_ref, k_ref, v_ref, al_ref, be_ref, g_ref, tail_ref, buf, *, tm, t_valid):
    j = pl.program_id(1)

    @pl.when(j == 0)
    def _():
        buf[0:SUBLANES, :] = tail0_ref[...]

    u = u_ref[...]
    buf[SUBLANES:SUBLANES + tm, :] = u
    prev = buf[pl.ds(SUBLANES - 1, tm), :]
    tail = buf[pl.ds(t_valid, SUBLANES), :]
    tail_ref[...] = tail
    buf[0:SUBLANES, :] = tail

    d = RWKV_DIM
    f = u + mu_ref[...] * (prev - u)
    r, k, v = f[:, :d], f[:, d:2 * d], f[:, 2 * d:3 * d]
    lo = f[:, 3 * d:3 * d + LANES]
    glo = f[:, 3 * d + LANES:]
    ww = w0_ref[...] + _dot(jnp.tanh(lo).astype(BF16), wup_ref[...])
    w_log = -_softplus(-ww) - 0.5
    a = _sigmoid(a0_ref[...] + _dot(lo.astype(BF16), aup_ref[...]))
    kk = k * kk_ref[...]
    ssb = _seg_sum_bcast(kk * kk, seg2_ref[...], segt2_ref[...])
    kkn = kk / jnp.maximum(jnp.sqrt(ssb), 1e-12)
    r_ref[...] = r
    w_ref[...] = -jnp.exp(w_log)
    k_ref[...] = k * (1.0 + (a - 1.0) * ka_ref[...])
    v_ref[...] = v
    al_ref[...] = -kkn
    be_ref[...] = kkn * a
    g_ref[...] = _dot(_sigmoid(glo).astype(BF16), gup_ref[...])


def _rwkv_pre(u, tail0, p, tm, t_valid):
    bsz = tail0.shape[0]
    n = u.shape[0]
    nt = n // bsz // tm
    d = RWKV_DIM
    kern = functools.partial(_rwkv_pre_kernel, tm=tm, t_valid=t_valid)
    row = pl.BlockSpec((tm, d), lambda b, j: (b * nt + j, 0))
    return pl.pallas_call(
        kern,
        grid=(bsz, nt),
        in_specs=[pl.BlockSpec((tm, RWKV_IN), lambda b, j: (b * nt + j, 0)),
                  pl.BlockSpec((None, SUBLANES, RWKV_IN), lambda b, j: (b, 0, 0)),
                  _full((1, RWKV_IN)), _full((1, d)), _full((LANES, d)), _full((1, d)),
                  _full((LANES, d)), _full((LANES, d)), _full((1, d)), _full((1, d)),
                  _full((2 * d, LANES)), _full((2 * LANES, d))],
        out_specs=[row] * 7 + [pl.BlockSpec((None, SUBLANES, RWKV_IN), lambda b, j: (b, 0, 0))],
        out_shape=[jax.ShapeDtypeStruct((n, d), F32)] * 7
                  + [jax.ShapeDtypeStruct((bsz, SUBLANES, RWKV_IN), F32)],
        scratch_shapes=[pltpu.VMEM((SUBLANES + tm, RWKV_IN), F32)],
        compiler_params=_cparams("parallel", "arbitrary"),
    )(u, tail0, p["rwkv_shift_mu"], p["rwkv_w0"], p["rwkv_w_up"], p["rwkv_a0"], p["rwkv_a_up"],
      p["rwkv_g_up"], p["rwkv_k_k"], p["rwkv_k_a"], p["seg2"], p["seg_t2"])


RWKV_CHUNK = 64
RWKV_SOLVE_BLOCK = 16
RWKV_PAIRS = RWKV_HEADS // 2


def _hi_lo(x):
    hi = x.astype(BF16)
    return hi, (x - hi.astype(F32)).astype(BF16)


def _mm(a, b):
    return _dot(a.astype(BF16), b.astype(BF16))


def _mm2(a, b):
    ah, al = _hi_lo(a)
    bb = b.astype(BF16)
    return _dot(ah, bb) + _dot(al, bb)


def _rwkv_chunk_kernel(r_ref, lw_ref, k_ref, v_ref, al_ref, be_ref, s0_ref, y_ref, s_ref, *, nchunks):
    c = pl.program_id(1)

    @pl.when(c == 0)
    def _():
        s_ref[...] = s0_ref[...]

    C, HD = RWKV_CHUNK, RWKV_HEAD_DIM
    r2 = lax.broadcasted_iota(jnp.int32, (LANES, LANES), 0)
    c2 = lax.broadcasted_iota(jnp.int32, (LANES, LANES), 1)
    eye = jnp.where(r2 == c2, 1.0, 0.0).astype(F32)
    diag_blk = (r2 // RWKV_SOLVE_BLOCK) == (c2 // RWKV_SOLVE_BLOCK)
    same_head = (r2 // HD) == (c2 // HD)
    tri = jnp.where(lax.broadcasted_iota(jnp.int32, (C, C), 0) >= lax.broadcasted_iota(jnp.int32, (C, C), 1),
                    1.0, 0.0).astype(F32)
    t_i = lax.broadcasted_iota(jnp.int32, (C, LANES), 0)
    lane = lax.broadcasted_iota(jnp.int32, (C, LANES), 1)
    lo_half = lane < HD
    hi_half = lane >= HD
    strict = (lane % HD) < t_i
    incl = (lane % HD) <= t_i
    zeros = jnp.zeros((C, LANES), F32)

    def body(ci, carry):
        rows = pl.ds(pl.multiple_of(ci * C, C), C)
        cum_all = jnp.dot(tri, lw_ref[rows, :], preferred_element_type=F32, precision=lax.Precision.HIGHEST)
        pairs = range(RWKV_PAIRS)
        sls = [slice(p * LANES, (p + 1) * LANES) for p in pairs]
        cum = [cum_all[:, sl] for sl in sls]
        be = [be_ref[rows, sl] for sl in sls]
        kk = [k_ref[rows, sl] for sl in sls]
        vv = [v_ref[rows, sl] for sl in sls]
        g_inv = [jnp.exp(-cum[p]) for p in pairs]
        ar = [jnp.concatenate([al_ref[rows, sls[p]] * jnp.exp(cum[p] - lw_ref[rows, sls[p]]),
                               r_ref[rows, sls[p]] * jnp.exp(cum[p])], axis=0) for p in pairs]
        ar_b = [ar[p].astype(BF16) for p in pairs]
        bt = [be[p] * g_inv[p] for p in pairs]
        kt = [kk[p] * g_inv[p] for p in pairs]
        x0 = [_dot_nt(ar_b[p], jnp.concatenate([jnp.where(lo_half, bt[p], 0.0), jnp.where(lo_half, kt[p], 0.0)],
                                                axis=0).astype(BF16)) for p in pairs]
        x1 = [_dot_nt(ar_b[p], jnp.concatenate([jnp.where(hi_half, kt[p], 0.0), jnp.where(hi_half, bt[p], 0.0)],
                                                axis=0).astype(BF16)) for p in pairs]
        nb = [jnp.concatenate([jnp.where(lo_half & strict, x0[p][:C], 0.0),
                               jnp.where(hi_half & strict, x1[p][:C], 0.0)], axis=0) for p in pairs]
        sbd = [s_ref[p] for p in pairs]

        def read_state(p):
            s_hi, s_lo = _hi_lo(sbd[p])
            return _dot_nt(ar_b[p], s_hi) + _dot_nt(ar_b[p], s_lo)

        ars = [read_state(p) for p in pairs]
        rhs = [ars[p][:C] + jnp.where(
            lo_half,
            _mm(jnp.where(hi_half & strict, x0[p][:C], 0.0), jnp.concatenate([zeros, vv[p]], axis=0)),
            _mm(jnp.where(lo_half & strict, x1[p][:C], 0.0), jnp.concatenate([vv[p], zeros], axis=0)))
            for p in pairs]
        rhs2 = [jnp.concatenate([rhs[p], rhs[p]], axis=0) for p in pairs]
        nd = [jnp.where(diag_blk, nb[p], 0.0) for p in pairs]
        loff = [nb[p] - nd[p] for p in pairs]
        pm = [eye + nd[p] for p in pairs]
        n2 = [_mm2(nd[p], nd[p]) for p in pairs]
        pm = [pm[p] + _mm2(pm[p], n2[p]) for p in pairs]
        n4 = [_mm2(n2[p], n2[p]) for p in pairs]
        pm = [pm[p] + _mm2(pm[p], n4[p]) for p in pairs]
        n8 = [_mm2(n4[p], n4[p]) for p in pairs]
        td = [pm[p] + _mm2(pm[p], n8[p]) for p in pairs]
        m1 = [_mm2(td[p], loff[p]) for p in pairs]
        m2 = [_mm2(m1[p], m1[p]) for p in pairs]
        x_a = [_mm2(td[p], rhs2[p]) for p in pairs]
        x_b = [x_a[p] + _mm2(m2[p], x_a[p]) for p in pairs]
        u2 = [x_b[p] + _mm2(m1[p], x_b[p]) for p in pairs]
        up = [jnp.where(lo_half, u2[p][:C], u2[p][C:]) for p in pairs]
        for p in pairs:
            uv = jnp.concatenate([up[p], vv[p]], axis=0)
            vu = jnp.concatenate([vv[p], up[p]], axis=0)
            y_ref[rows, sls[p]] = ars[p][C:] + jnp.where(lo_half, _mm(jnp.where(incl, x0[p][C:], 0.0), uv),
                                                         _mm(jnp.where(incl, x1[p][C:], 0.0), vu))
            cum_c = cum[p][C - 1:C, :]
            g_end = jnp.exp(cum_c - cum[p])
            bkh = jnp.concatenate([be[p] * g_end, kk[p] * g_end], axis=0)
            upd = _dot_tn(uv.astype(BF16), bkh.astype(BF16))
            s_ref[p] = sbd[p] * jnp.exp(cum_c) + jnp.where(same_head, upd, 0.0)
        return carry

    lax.fori_loop(0, nchunks, body, 0)


def _rwkv_scan(r, lw, k, v, al, be, s0, bsz, tc):
    n, d = r.shape
    t_total = n // bsz
    nt = t_total // tc
    kern = functools.partial(_rwkv_chunk_kernel, nchunks=tc // RWKV_CHUNK)
    row = pl.BlockSpec((tc, d), lambda b, c: (b * nt + c, 0))
    st = pl.BlockSpec((None, RWKV_PAIRS, LANES, LANES), lambda b, c: (b, 0, 0, 0))
    return pl.pallas_call(
        kern,
        grid=(bsz, nt),
        in_specs=[row] * 6 + [st],
        out_specs=[row, st],
        out_shape=[jax.ShapeDtypeStruct((n, d), F32),
                   jax.ShapeDtypeStruct((bsz, RWKV_PAIRS, LANES, LANES), F32)],
        compiler_params=_cparams("parallel", "arbitrary"),
    )(r, lw, k, v, al, be, s0)


def _rwkv_post_kernel(y_ref, r_ref, k_ref, v_ref, g_ref, lng_ref, lnb_ref, rk_ref, seg2_ref, segt2_ref, o_ref):
    seg2, seg_t2 = seg2_ref[...], segt2_ref[...]
    y = y_ref[...]
    inv = 1.0 / RWKV_HEAD_DIM
    dlt = y - _seg_sum_bcast(y, seg2, seg_t2) * inv
    var = _seg_sum_bcast(dlt * dlt, seg2, seg_t2) * inv
    yn = dlt * lax.rsqrt(var + RWKV_LN_EPS) * lng_ref[...] + lnb_ref[...]
    bonus = _seg_sum_bcast(r_ref[...] * k_ref[...] * rk_ref[...], seg2, seg_t2) * v_ref[...]
    o_ref[...] = ((yn + bonus) * g_ref[...]).astype(BF16)


def _rwkv_post(y, r, k, v, g, p, tm):
    n, d = y.shape
    row = pl.BlockSpec((tm, d), lambda i: (i, 0))
    return pl.pallas_call(
        _rwkv_post_kernel,
        grid=(n // tm,),
        in_specs=[row] * 5 + [_full((1, d))] * 3 + [_full((2 * d, LANES)), _full((2 * LANES, d))],
        out_specs=row,
        out_shape=jax.ShapeDtypeStruct((n, d), BF16),
        compiler_params=_cparams("parallel"),
    )(y, r, k, v, g, p["rwkv_ln_g"], p["rwkv_ln_b"], p["rwkv_r_k"], p["seg2"], p["seg_t2"])


def _mla_prep_kernel(u_ref, cos_ref, s1_ref, s2_ref, qag_ref, wqn_ref, wqr_ref, kvg_ref, wkb_ref, wvb_ref,
                     gqn_ref, gqr_ref, gkn_ref, gkr_ref, q_ref, k_ref, v_ref, ckv_ref, kpe_ref):
    u = u_ref[...]
    cq = u[:, :MLA_Q_RANK]
    ckv_raw = u[:, MLA_Q_RANK:MLA_Q_RANK + MLA_KV_RANK]
    kr_raw = u[:, MLA_Q_RANK + MLA_KV_RANK:]
    cos, s1, s2 = cos_ref[...], s1_ref[...], s2_ref[...]

    def rope(blk):
        return (blk * cos + pltpu.roll(blk, LANES - MLA_ROPE // 2, 1) * s1
                + pltpu.roll(blk, MLA_ROPE // 2, 1) * s2)

    cqn = (_rms(cq) * qag_ref[...]).astype(BF16)
    qn = _dot(cqn, wqn_ref[...])
    qr = _dot(cqn, wqr_ref[...])
    ckv = _rms(ckv_raw) * kvg_ref[...]
    ckv_ref[...] = ckv
    kpe = rope(kr_raw)
    kpe_ref[...] = kpe
    ckb = ckv.astype(BF16)
    kn = _dot(ckb, wkb_ref[...])
    v_ref[...] = _dot(ckb, wvb_ref[...]).astype(BF16)
    ss_kpe = jnp.sum(kpe * kpe, axis=-1, keepdims=True)
    gqn, gqr, gkn, gkr = gqn_ref[...], gqr_ref[...], gkn_ref[...], gkr_ref[...]
    for h in range(MLA_HEADS):
        sl = slice(h * MLA_NOPE, (h + 1) * MLA_NOPE)
        o0 = h * MLA_QK_PAD
        qn_h = qn[:, sl]
        qr_h = rope(qr[:, sl])
        ss = jnp.sum(qn_h * qn_h, axis=-1, keepdims=True) + jnp.sum(qr_h * qr_h, axis=-1, keepdims=True)
        rn = lax.rsqrt(ss * (1.0 / MLA_QK) + NORM_EPS) * MLA_SCALE
        q_ref[:, o0:o0 + MLA_NOPE] = (qn_h * rn * gqn).astype(q_ref.dtype)
        q_ref[:, o0 + MLA_NOPE:o0 + MLA_QK_PAD] = (qr_h * rn * gqr).astype(q_ref.dtype)
        kn_h = kn[:, sl]
        ssk = jnp.sum(kn_h * kn_h, axis=-1, keepdims=True) + ss_kpe
        rnk = lax.rsqrt(ssk * (1.0 / MLA_QK) + NORM_EPS)
        k_ref[:, o0:o0 + MLA_NOPE] = (kn_h * rnk * gkn).astype(BF16)
        k_ref[:, o0 + MLA_NOPE:o0 + MLA_QK_PAD] = (kpe * rnk * gkr).astype(BF16)


def _mla_prep(u, rope_tabs, p, tm, q_dtype):
    n = u.shape[0]
    ttab = rope_tabs[0].shape[0]
    ntab = ttab // tm
    hq = MLA_HEADS * MLA_QK_PAD
    hv = MLA_HEADS * MLA_V
    tab = pl.BlockSpec((tm, LANES), lambda i: (i % ntab, 0))
    return pl.pallas_call(
        _mla_prep_kernel,
        grid=(n // tm,),
        in_specs=[pl.BlockSpec((tm, MLA_IN_PAD), lambda i: (i, 0)), tab, tab, tab,
                  _full((1, MLA_Q_RANK)), _full((MLA_Q_RANK, hv)), _full((MLA_Q_RANK, hv)),
                  _full((1, MLA_KV_RANK)), _full((MLA_KV_RANK, hv)), _full((MLA_KV_RANK, hv)),
                  _full((1, LANES)), _full((1, LANES)), _full((1, LANES)), _full((1, LANES))],
        out_specs=[pl.BlockSpec((tm, hq), lambda i: (i, 0)), pl.BlockSpec((tm, hq), lambda i: (i, 0)),
                   pl.BlockSpec((tm, hv), lambda i: (i, 0)),
                   pl.BlockSpec((tm, MLA_KV_RANK), lambda i: (i, 0)),
                   pl.BlockSpec((tm, LANES), lambda i: (i, 0))],
        out_shape=[jax.ShapeDtypeStruct((n, hq), q_dtype), jax.ShapeDtypeStruct((n, hq), BF16),
                   jax.ShapeDtypeStruct((n, hv), BF16), jax.ShapeDtypeStruct((n, MLA_KV_RANK), F32),
                   jax.ShapeDtypeStruct((n, LANES), F32)],
        compiler_params=_cparams("parallel"),
    )(u, *rope_tabs, p["mla_q_a_g"], p["wq_nope"], p["wq_rope"], p["mla_kv_a_g"], p["w_kb"], p["w_vb"],
      p["gq_nope"], p["gq_rope"], p["gk_nope"], p["gk_rope"])


FLASH_HEAD_GROUP = 2


def _flash_kernel(q_ref, k_ref, v_ref, o_ref, m_sc, acc_sc, *, tq):
    qi = pl.program_id(1)
    ki = pl.program_id(2)
    nblk = tq // LANES

    @pl.when(ki == 0)
    def _():
        m_sc[...] = jnp.full_like(m_sc, NEG_BIG)
        acc_sc[...] = jnp.zeros_like(acc_sc)

    ones = jnp.ones((tq, LANES), BF16)

    def update(diagonal):
        for h0 in range(0, MLA_HEADS, FLASH_HEAD_GROUP):
            heads = range(h0, h0 + FLASH_HEAD_GROUP)
            s = {h: _dot_nt(q_ref[:, h * MLA_QK_PAD:(h + 1) * MLA_QK_PAD],
                            k_ref[:, h * MLA_QK_PAD:(h + 1) * MLA_QK_PAD]) for h in heads}
            if diagonal:
                row = lax.broadcasted_iota(jnp.int32, (tq, tq), 0)
                col = lax.broadcasted_iota(jnp.int32, (tq, tq), 1)
                s = {h: jnp.where(col <= row, s[h], NEG_BIG) for h in heads}
            m_prev = {h: m_sc[h] for h in heads}
            m_new = {h: jnp.maximum(m_prev[h], jnp.max(s[h], axis=-1, keepdims=True)) for h in heads}
            pr = {h: jnp.concatenate([jnp.exp(s[h][:, j * LANES:(j + 1) * LANES] - m_new[h]) for j in range(nblk)],
                                     axis=1).astype(BF16) for h in heads}
            corr = {h: jnp.exp(m_prev[h] - m_new[h]) for h in heads}
            for h in heads:
                v_ext = jnp.concatenate([v_ref[:, h * MLA_V:(h + 1) * MLA_V], ones], axis=1)
                acc_sc[h] = acc_sc[h] * jnp.concatenate([corr[h], corr[h]], axis=1) + _dot(pr[h], v_ext)
                m_sc[h] = m_new[h]

    @pl.when(ki < qi)
    def _():
        update(False)

    @pl.when(ki == qi)
    def _():
        update(True)
        for h in range(MLA_HEADS):
            acc = acc_sc[h]
            o_ref[:, h * MLA_V:(h + 1) * MLA_V] = (acc[:, :MLA_V] / acc[:, MLA_V:]).astype(BF16)


def _flash(q, k, v, bsz, tq):
    n = q.shape[0]
    nq = n // bsz // tq
    hq = MLA_HEADS * MLA_QK_PAD
    hv = MLA_HEADS * MLA_V
    kern = functools.partial(_flash_kernel, tq=tq)
    return pl.pallas_call(
        kern,
        grid=(bsz, nq, nq),
        in_specs=[pl.BlockSpec((tq, hq), lambda b, i, j: (b * nq + i, 0)),
                  pl.BlockSpec((tq, hq), lambda b, i, j: (b * nq + jnp.minimum(i, j), 0)),
                  pl.BlockSpec((tq, hv), lambda b, i, j: (b * nq + jnp.minimum(i, j), 0))],
        out_specs=pl.BlockSpec((tq, hv), lambda b, i, j: (b * nq + i, 0)),
        out_shape=jax.ShapeDtypeStruct((n, hv), BF16),
        scratch_shapes=[pltpu.VMEM((MLA_HEADS, tq, LANES), F32),
                        pltpu.VMEM((MLA_HEADS, tq, MLA_V + LANES), F32)],
        compiler_params=_cparams("parallel", "parallel", "arbitrary"),
    )(q, k, v)


PAGES_PER_STEP = 16
PAGED_ROWS = MLA_HEADS * SAMPLE_T_PAD


def _paged_kernel(pt_ref, q_ref, *refs, t_new, npp):
    lat_refs = refs[:npp]
    rope_refs = refs[npp:2 * npp]
    (newc_ref, newr_ref, wkbt_ref, wvb_ref, gkn_ref, gkr_ref, o_ref,
     m_sc, l_sc, acc_sc, wq_sc, qr_sc) = refs[2 * npp:]
    step = pl.program_id(1)
    nkn = MLA_HEADS * MLA_NOPE

    @pl.when(step == 0)
    def _():
        m_sc[...] = jnp.full_like(m_sc, NEG_BIG)
        l_sc[...] = jnp.zeros_like(l_sc)
        acc_sc[...] = jnp.zeros_like(acc_sc)
        q = q_ref[...]
        gkn, gkr = gkn_ref[...], gkr_ref[...]
        wkbt = wkbt_ref[...]
        wq_sc[0:nkn, :] = wkbt
        for h in range(MLA_HEADS):
            o0 = h * MLA_QK_PAD
            rows = slice(h * SAMPLE_T_PAD, (h + 1) * SAMPLE_T_PAD)
            qn = (q[:, o0:o0 + MLA_NOPE] * gkn).astype(BF16)
            wq_sc[nkn + h * SAMPLE_T_PAD:nkn + (h + 1) * SAMPLE_T_PAD, :] = _dot(
                qn, wkbt[h * MLA_NOPE:(h + 1) * MLA_NOPE, :]).astype(BF16)
            qr_sc[rows, :] = (q[:, o0 + MLA_NOPE:o0 + MLA_QK_PAD] * gkr).astype(BF16)

    wq = wq_sc[...]
    qr = qr_sc[...]

    def scores(cbs, krs):
        n = len(cbs)
        big = [_dot_nt(wq, cbs[i]) for i in range(n)]
        rope = [_dot(qr, krs[i].astype(BF16)) for i in range(n)]
        ssr = [jnp.sum(krs[i] * krs[i], axis=0, keepdims=True) for i in range(n)]
        out = []
        for i in range(n):
            rn = []
            for h in range(MLA_HEADS):
                kn_h = big[i][h * MLA_NOPE:(h + 1) * MLA_NOPE]
                ss = jnp.sum(kn_h * kn_h, axis=0, keepdims=True) + ssr[i]
                rn.append(jnp.broadcast_to(lax.rsqrt(ss * (1.0 / MLA_QK) + NORM_EPS), (SAMPLE_T_PAD, PAGE_SIZE)))
            out.append((big[i][nkn:] + rope[i]) * jnp.concatenate(rn, axis=0))
        return out

    def softmax_update(s_list, cb_list):
        m_prev = m_sc[...]
        m_new = m_prev
        for s in s_list:
            m_new = jnp.maximum(m_new, jnp.max(s, axis=-1, keepdims=True))
        corr = jnp.exp(m_prev - m_new)
        l_new = l_sc[...] * corr
        acc = acc_sc[...] * corr
        for s, cb in zip(s_list, cb_list):
            pr = jnp.exp(s - m_new)
            l_new = l_new + jnp.sum(pr, axis=-1, keepdims=True)
            acc = acc + _dot(pr.astype(BF16), cb)
        l_sc[...] = l_new
        acc_sc[...] = acc
        m_sc[...] = m_new

    cbs = [lat_refs[i][...].astype(BF16) for i in range(npp)]
    rope_pad = jnp.zeros((LANES - MLA_ROPE, PAGE_SIZE), F32)
    krs = [jnp.concatenate([rope_refs[i][...], rope_pad], axis=0) for i in range(npp)]
    softmax_update(scores(cbs, krs), cbs)

    @pl.when(step == pl.num_programs(1) - 1)
    def _():
        cb = newc_ref[...].astype(BF16)
        s = scores([cb], [newr_ref[...]])[0]
        key = lax.broadcasted_iota(jnp.int32, s.shape, 1)
        qry = lax.broadcasted_iota(jnp.int32, s.shape, 0) % SAMPLE_T_PAD
        softmax_update([jnp.where((key <= qry) & (key < t_new), s, NEG_BIG)], [cb])
        o_lat = (acc_sc[...] / l_sc[...]).astype(BF16)
        wvb = wvb_ref[...]
        for h in range(MLA_HEADS):
            o_ref[:, h * MLA_V:(h + 1) * MLA_V] = _dot(
                o_lat[h * SAMPLE_T_PAD:(h + 1) * SAMPLE_T_PAD],
                wvb[:, h * MLA_V:(h + 1) * MLA_V])


def _paged_attention(page_table, q, lat_pool, rope_pool_t, layer, new_c, new_r_t, p, t_new):
    bsz, n_pages = page_table.shape
    npp = math.gcd(PAGES_PER_STEP, n_pages)
    nsteps = n_pages // npp
    hq = MLA_HEADS * MLA_QK_PAD
    hv = MLA_HEADS * MLA_V

    def lat_spec(i):
        return pl.BlockSpec((None, None, PAGE_SIZE, MLA_KV_RANK),
                            lambda b, s, pt: (layer, pt[b, s * npp + i], 0, 0))

    def rope_spec(i):
        return pl.BlockSpec((None, None, MLA_ROPE, PAGE_SIZE),
                            lambda b, s, pt: (layer, pt[b, s * npp + i], 0, 0))

    grid_spec = pltpu.PrefetchScalarGridSpec(
        num_scalar_prefetch=1,
        grid=(bsz, nsteps),
        in_specs=[pl.BlockSpec((SAMPLE_T_PAD, hq), lambda b, s, pt: (b, 0))]
                 + [lat_spec(i) for i in range(npp)] + [rope_spec(i) for i in range(npp)]
                 + [pl.BlockSpec((None, PAGE_SIZE, MLA_KV_RANK), lambda b, s, pt: (b, 0, 0)),
                    pl.BlockSpec((None, LANES, PAGE_SIZE), lambda b, s, pt: (b, 0, 0)),
                    pl.BlockSpec((hv, MLA_KV_RANK), lambda b, s, pt: (0, 0)),
                    pl.BlockSpec((MLA_KV_RANK, hv), lambda b, s, pt: (0, 0)),
                    pl.BlockSpec((1, LANES), lambda b, s, pt: (0, 0)),
                    pl.BlockSpec((1, LANES), lambda b, s, pt: (0, 0))],
        out_specs=pl.BlockSpec((SAMPLE_T_PAD, hv), lambda b, s, pt: (b, 0)),
        scratch_shapes=[pltpu.VMEM((PAGED_ROWS, 1), F32),
                        pltpu.VMEM((PAGED_ROWS, 1), F32),
                        pltpu.VMEM((PAGED_ROWS, MLA_KV_RANK), F32),
                        pltpu.VMEM((hv + PAGED_ROWS, MLA_KV_RANK), BF16),
                        pltpu.VMEM((PAGED_ROWS, LANES), BF16)],
    )
    kern = functools.partial(_paged_kernel, t_new=t_new, npp=npp)
    return pl.pallas_call(
        kern,
        grid_spec=grid_spec,
        out_shape=jax.ShapeDtypeStruct((bsz * SAMPLE_T_PAD, hv), F32),
        compiler_params=_cparams("parallel", "arbitrary"),
    )(page_table, q, *([lat_pool] * npp), *([rope_pool_t] * npp), new_c, new_r_t,
      p["w_kb_t"], p["w_vb"], p["gk_nope"], p["gk_rope"])


def _merge_kernel(x_ref, g_ref, wg_ref, bg_ref, os_ref, or_ref, om_ref, ws_ref, wr_ref, wm_ref, wo_ref, o_ref):
    h = (_rms(x_ref[...]) * g_ref[...]).astype(BF16)
    gl = _dot(h, wg_ref[...]) + bg_ref[...]
    d = D_MODEL
    merged = (_sigmoid(gl[:, :d]) * _dot(os_ref[...], ws_ref[...])
              + _sigmoid(gl[:, d:2 * d]) * _dot(or_ref[...], wr_ref[...])
              + _sigmoid(gl[:, 2 * d:]) * _dot(om_ref[...].astype(BF16), wm_ref[...]))
    o_ref[...] = x_ref[...] + _dot(merged.astype(BF16), wo_ref[...])


def _merge(x, o_ssd, o_rwkv, o_mla, p, tm):
    n, d = x.shape
    row = pl.BlockSpec((tm, d), lambda i: (i, 0))
    return pl.pallas_call(
        _merge_kernel,
        grid=(n // tm,),
        in_specs=[row, _full((1, d)), _full((d, N_BRANCH * d)), _full((1, N_BRANCH * d)), row, row, row]
                 + [_full((d, d))] * 4,
        out_specs=row,
        out_shape=jax.ShapeDtypeStruct((n, d), F32),
        compiler_params=_cparams("parallel"),
    )(x, p["norm_mix_g"], p["w_gate"], p["b_gate"], o_ssd, o_rwkv, o_mla,
      p["w_o_ssd"], p["w_o_rwkv"], p["w_o_mla"], p["w_out"])


def _ffn_kernel(x_ref, tail0_ref, g_ref, wup_ref, cw_ref, cb_ref, wdn_ref, o_ref, tail_ref, buf, *, tm, t_valid):
    j = pl.program_id(1)

    @pl.when(j == 0)
    def _():
        buf[0:SUBLANES, :] = tail0_ref[...]

    x = x_ref[...]
    h2 = (_rms(x) * g_ref[...]).astype(BF16)
    buf[SUBLANES:SUBLANES + tm, :] = _dot(h2, wup_ref[...])
    conv = cb_ref[...]
    for i in range(FFN_CONV):
        conv = conv + buf[pl.ds(SUBLANES - (FFN_CONV - 1) + i, tm), :] * cw_ref[i:i + 1, :]
    tail = buf[pl.ds(t_valid, SUBLANES), :]
    tail_ref[...] = tail
    buf[0:SUBLANES, :] = tail
    act = (_silu(conv[:, :D_FF]) * conv[:, D_FF:]).astype(BF16)
    o_ref[...] = x + _dot(act, wdn_ref[...])


def _ffn(x, tail0, p, tm, t_valid):
    bsz = tail0.shape[0]
    n, d = x.shape
    nt = n // bsz // tm
    kern = functools.partial(_ffn_kernel, tm=tm, t_valid=t_valid)
    return pl.pallas_call(
        kern,
        grid=(bsz, nt),
        in_specs=[pl.BlockSpec((tm, d), lambda b, j: (b * nt + j, 0)),
                  pl.BlockSpec((None, SUBLANES, 2 * D_FF), lambda b, j: (b, 0, 0)),
                  _full((1, d)),
                  pl.BlockSpec((d, 2 * D_FF), lambda b, j: (0, 0), pipeline_mode=pl.Buffered(1)),
                  _full((FFN_CONV, 2 * D_FF)), _full((1, 2 * D_FF)),
                  pl.BlockSpec((D_FF, d), lambda b, j: (0, 0), pipeline_mode=pl.Buffered(1))],
        out_specs=[pl.BlockSpec((tm, d), lambda b, j: (b * nt + j, 0)),
                   pl.BlockSpec((None, SUBLANES, 2 * D_FF), lambda b, j: (b, 0, 0))],
        out_shape=[jax.ShapeDtypeStruct((n, d), F32),
                   jax.ShapeDtypeStruct((bsz, SUBLANES, 2 * D_FF), F32)],
        scratch_shapes=[pltpu.VMEM((SUBLANES + tm, 2 * D_FF), F32)],
        compiler_params=_cparams("parallel", "arbitrary"),
    )(x, tail0, p["norm_ffn_g"], p["ffn_w_up"], p["ffn_conv_w"], p["ffn_conv_b"], p["ffn_w_down"])


def _pad_cols(w, width):
    return jnp.pad(w, ((0, 0), (0, width - w.shape[1])))


def _row(v):
    return v.reshape(1, -1)


def _prep_layer(i, w):
    d = RWKV_DIM
    w_in = w["w_in"][i]
    o_r = SSD_IN
    o_m = SSD_IN + RWKV_IN
    p = {}
    p["norm_mix_g"] = _row(w["norm_mix_g"][i])
    p["w_ssd"] = _pad_cols(w_in[:, :o_r], SSD_IN_PAD).astype(BF16)
    p["w_rwkv"] = w_in[:, o_r:o_m].astype(BF16)
    p["w_mla"] = _pad_cols(w_in[:, o_m:], MLA_IN_PAD).astype(BF16)
    p["w_gate"] = w["w_gate"][i].astype(BF16)
    p["b_gate"] = _row(w["b_gate"][i])
    p["ssd_conv_w"] = w["ssd_conv_w"][i]
    p["ssd_conv_b"] = _row(w["ssd_conv_b"][i])
    p["ssd_dt_bias"] = _pad_cols(_row(w["ssd_dt_bias"][i]), LANES)
    p["ssd_a_log"] = _pad_cols(_row(w["ssd_a_log"][i]), LANES)
    p["ssd_d"] = _row(jnp.repeat(w["ssd_d"][i], SSD_HEAD_DIM))
    p["ssd_norm_g"] = _row(w["ssd_norm_g"][i])
    p["rwkv_shift_mu"] = _row(w["rwkv_shift_mu"][i])
    p["rwkv_w0"] = _row(w["rwkv_w0"][i])
    zeros_lora = jnp.zeros((RWKV_DECAY_LORA, d), F32)
    p["rwkv_w_up"] = jnp.concatenate([w["rwkv_w_up"][i], zeros_lora], axis=0).astype(BF16)
    p["rwkv_a0"] = _row(w["rwkv_a0"][i])
    p["rwkv_a_up"] = jnp.concatenate([zeros_lora, w["rwkv_a_up"][i]], axis=0).astype(BF16)
    p["rwkv_g_up"] = w["rwkv_g_up"][i].astype(BF16)
    p["rwkv_k_k"] = _row(w["rwkv_k_k"][i])
    p["rwkv_k_a"] = _row(w["rwkv_k_a"][i])
    p["rwkv_r_k"] = _row(w["rwkv_r_k"][i])
    p["rwkv_ln_g"] = _row(w["rwkv_ln_g"][i])
    p["rwkv_ln_b"] = _row(w["rwkv_ln_b"][i])
    wq = w["mla_w_q_b"][i]
    hv = MLA_HEADS * MLA_NOPE
    p["mla_q_a_g"] = _row(w["mla_q_a_g"][i])
    p["wq_nope"] = wq[:, :, :MLA_NOPE].reshape(MLA_Q_RANK, hv).astype(BF16)
    p["wq_rope"] = jnp.pad(wq[:, :, MLA_NOPE:], ((0, 0), (0, 0), (0, MLA_NOPE - MLA_ROPE))
                           ).reshape(MLA_Q_RANK, hv).astype(BF16)
    p["mla_kv_a_g"] = _row(w["mla_kv_a_g"][i])
    p["w_kb"] = w["mla_w_kb"][i].reshape(MLA_KV_RANK, hv).astype(BF16)
    p["w_kb_t"] = p["w_kb"].T
    p["w_vb"] = w["mla_w_vb"][i].reshape(MLA_KV_RANK, MLA_HEADS * MLA_V).astype(BF16)
    gq, gk = w["mla_q_norm_g"][i], w["mla_k_norm_g"][i]
    p["gq_nope"] = _row(gq[:MLA_NOPE])
    p["gq_rope"] = _pad_cols(_row(gq[MLA_NOPE:]), LANES)
    p["gk_nope"] = _row(gk[:MLA_NOPE])
    p["gk_rope"] = _pad_cols(_row(gk[MLA_NOPE:]), LANES)
    for name in ("w_o_ssd", "w_o_rwkv", "w_o_mla", "w_out"):
        p[name] = w[name][i].astype(BF16)
    p["norm_ffn_g"] = _row(w["norm_ffn_g"][i])
    p["ffn_w_up"] = w["ffn_w_up"][i].astype(BF16)
    p["ffn_conv_w"] = w["ffn_conv_w"][i]
    p["ffn_conv_b"] = _row(w["ffn_conv_b"][i])
    p["ffn_w_down"] = w["ffn_w_down"][i].astype(BF16)
    return p


def _constants():
    ch = np.arange(RWKV_DIM)
    seg = (ch[:, None] // RWKV_HEAD_DIM == np.arange(LANES)[None, :]).astype(np.float32)
    seg2 = np.concatenate([seg, seg], axis=0)
    seg_t2 = np.concatenate([seg.T, seg.T], axis=0)
    return jnp.asarray(seg2, BF16), jnp.asarray(seg_t2, BF16)


def _rope_tables(pos):
    half = MLA_ROPE // 2
    inv = ROPE_BASE ** (-jnp.arange(half, dtype=F32) / half)
    ang = pos.astype(F32)[:, None] * inv[None, :]
    cos, sin = jnp.cos(ang), jnp.sin(ang)
    z = jnp.zeros_like(cos)
    z2 = jnp.zeros((pos.shape[0], LANES - MLA_ROPE), F32)
    return (jnp.concatenate([cos, cos, z2], axis=1),
            jnp.concatenate([-sin, z, z2], axis=1),
            jnp.concatenate([z, sin, z2], axis=1))


def _tail_block(state, rows):
    return jnp.pad(state, ((0, 0), (SUBLANES - rows, 0), (0, 0)))


def _pack_rwkv_state(s):
    b = s.shape[0]
    s5 = s.reshape(b, RWKV_PAIRS, 2, RWKV_HEAD_DIM, RWKV_HEAD_DIM)
    z = jnp.zeros_like(s5[:, :, 0])
    top = jnp.concatenate([s5[:, :, 0], z], axis=-1)
    bot = jnp.concatenate([z, s5[:, :, 1]], axis=-1)
    return jnp.concatenate([top, bot], axis=-2)


def _unpack_rwkv_state(s):
    b = s.shape[0]
    hd = RWKV_HEAD_DIM
    return jnp.stack([s[:, :, :hd, :hd], s[:, :, hd:, hd:]], axis=2).reshape(b, RWKV_HEADS, hd, hd)


def _layer(x, st, p, cfg, attend):
    bsz, t, t_valid, tm = cfg["bsz"], cfg["t"], cfg["t_valid"], cfg["tm"]
    n = x.shape[0]
    tmn = min(tm, n)
    zb = lambda w_: jnp.zeros((1, w_), F32)
    g = p["norm_mix_g"]
    u_ssd = _norm_matmul(x, g, p["w_ssd"], zb(SSD_IN_PAD), tmn)
    u_rwkv = _norm_matmul(x, g, p["w_rwkv"], zb(RWKV_IN), tmn)
    u_mla = _norm_matmul(x, g, p["w_mla"], zb(MLA_IN_PAD), tmn)

    if t % SSD_CHUNK:
        u_pad = jnp.pad(u_ssd.reshape(bsz, t, SSD_IN_PAD), ((0, 0), (0, SSD_CHUNK - t), (0, 0)))
        o_ssd, ssd_tail, ssd_h = _ssd(u_pad.reshape(bsz * SSD_CHUNK, SSD_IN_PAD), st["ssd_tail"], st["ssd_h"],
                                      p, t_valid)
        o_ssd = o_ssd.reshape(bsz, SSD_CHUNK, SSD_INNER)[:, :t].reshape(n, SSD_INNER)
    else:
        o_ssd, ssd_tail, ssd_h = _ssd(u_ssd, st["ssd_tail"], st["ssd_h"], p, SSD_CHUNK)

    tm_r = min(tm, t)
    r, lw, k2, v, al, be, gg, rwkv_tail = _rwkv_pre(u_rwkv, st["rwkv_tail"], p, tm_r, min(t_valid, tm_r))
    d = RWKV_DIM
    if t % RWKV_CHUNK:
        def chunk_pad(a_):
            a_ = a_.reshape(bsz, t, d)[:, :t_valid]
            return jnp.pad(a_, ((0, 0), (0, RWKV_CHUNK - t_valid), (0, 0))).reshape(bsz * RWKV_CHUNK, d)
        yy, rwkv_s = _rwkv_scan(*(chunk_pad(a_) for a_ in (r, lw, k2, v, al, be)), st["rwkv_s"], bsz, RWKV_CHUNK)
        yy = yy.reshape(bsz, RWKV_CHUNK, d)[:, :t].reshape(n, d)
    else:
        yy, rwkv_s = _rwkv_scan(r, lw, k2, v, al, be, st["rwkv_s"], bsz, min(t, cfg["tc_rwkv"]))
    o_rwkv = _rwkv_post(yy, r, k2, v, gg, p, tmn)

    q, k, vv, ckv, kpe = _mla_prep(u_mla, cfg["rope"], p, tmn, cfg["q_dtype"])
    o_mla = attend(q, k, vv, ckv, kpe)

    x = _merge(x, o_ssd, o_rwkv, o_mla, p, tmn)
    tm_f = min(cfg["tm_ffn"], t)
    x, ffn_tail = _ffn(x, st["ffn_tail"], p, tm_f, min(t_valid, tm_f))
    new_st = dict(ssd_tail=ssd_tail, ssd_h=ssd_h, rwkv_tail=rwkv_tail, rwkv_s=rwkv_s, ffn_tail=ffn_tail)
    return x, new_st, ckv, kpe


def kernel(x_prompt, x_sample, cache_kv_latent, cache_k_rope, page_table, state_ssm, state_ssm_conv,
           state_rwkv, state_rwkv_shift, state_ffn_conv, norm_mix_g, w_in, ssd_conv_w, ssd_conv_b,
           ssd_dt_bias, ssd_a_log, ssd_d, ssd_norm_g, rwkv_shift_mu, rwkv_w0, rwkv_w_up, rwkv_a0,
           rwkv_a_up, rwkv_g_up, rwkv_k_k, rwkv_k_a, rwkv_r_k, rwkv_ln_g, rwkv_ln_b, mla_q_a_g,
           mla_w_q_b, mla_kv_a_g, mla_w_kb, mla_w_vb, mla_q_norm_g, mla_k_norm_g, w_gate, b_gate,
           w_o_ssd, w_o_rwkv, w_o_mla, w_out, norm_ffn_g, ffn_w_up, ffn_conv_w, ffn_conv_b, ffn_w_down):
    w = dict(norm_mix_g=norm_mix_g, w_in=w_in, ssd_conv_w=ssd_conv_w, ssd_conv_b=ssd_conv_b,
             ssd_dt_bias=ssd_dt_bias, ssd_a_log=ssd_a_log, ssd_d=ssd_d, ssd_norm_g=ssd_norm_g,
             rwkv_shift_mu=rwkv_shift_mu, rwkv_w0=rwkv_w0, rwkv_w_up=rwkv_w_up, rwkv_a0=rwkv_a0,
             rwkv_a_up=rwkv_a_up, rwkv_g_up=rwkv_g_up, rwkv_k_k=rwkv_k_k, rwkv_k_a=rwkv_k_a,
             rwkv_r_k=rwkv_r_k, rwkv_ln_g=rwkv_ln_g, rwkv_ln_b=rwkv_ln_b, mla_q_a_g=mla_q_a_g,
             mla_w_q_b=mla_w_q_b, mla_kv_a_g=mla_kv_a_g, mla_w_kb=mla_w_kb, mla_w_vb=mla_w_vb,
             mla_q_norm_g=mla_q_norm_g, mla_k_norm_g=mla_k_norm_g, w_gate=w_gate, b_gate=b_gate,
             w_o_ssd=w_o_ssd, w_o_rwkv=w_o_rwkv, w_o_mla=w_o_mla, w_out=w_out, norm_ffn_g=norm_ffn_g,
             ffn_w_up=ffn_w_up, ffn_conv_w=ffn_conv_w, ffn_conv_b=ffn_conv_b, ffn_w_down=ffn_w_down)
    depth = w_in.shape[0]
    bp, tp, d = x_prompt.shape
    bs, ts, _ = x_sample.shape
    tsp = SAMPLE_T_PAD
    past_len = page_table.shape[1] * PAGE_SIZE
    seg2, seg_t2 = _constants()
    rope_pool_t = jnp.swapaxes(cache_k_rope, 2, 3)

    tm_p = min(256, tp)
    cfg_p = dict(bsz=bp, t=tp, t_valid=tp, tm=tm_p, tm_ffn=min(128, tp), tc_rwkv=256,
                 rope=_rope_tables(jnp.arange(tp)), q_dtype=BF16)
    pos_s = past_len + jnp.arange(tsp)
    rope_s = tuple(jnp.tile(tb, (bs, 1)) for tb in _rope_tables(pos_s))
    cfg_s = dict(bsz=bs, t=tsp, t_valid=ts, tm=bs * tsp, tm_ffn=tsp, tc_rwkv=RWKV_CHUNK, rope=rope_s,
                 q_dtype=F32)

    xp = x_prompt.reshape(bp * tp, d)
    xs = jnp.pad(x_sample, ((0, 0), (0, tsp - ts), (0, 0))).reshape(bs * tsp, d)

    zero_st = dict(ssd_tail=jnp.zeros((bp, SUBLANES, SSD_CONV_DIM), F32),
                   ssd_h=jnp.zeros((bp, SSD_HEADS // 2, LANES, SSD_STATE), F32),
                   rwkv_tail=jnp.zeros((bp, SUBLANES, RWKV_IN), F32),
                   rwkv_s=jnp.zeros((bp, RWKV_PAIRS, LANES, LANES), F32),
                   ffn_tail=jnp.zeros((bp, SUBLANES, 2 * D_FF), F32))

    new_p = [[] for _ in range(7)]
    new_s = [[] for _ in range(7)]
    for i in range(depth):
        p = _prep_layer(i, w)
        p["seg2"], p["seg_t2"] = seg2, seg_t2

        def prompt_attend(q, k, v, ckv, kpe):
            return _flash(q, k, v, bp, min(512, tp))

        xp, st_p, ckv_p, kpe_p = _layer(xp, zero_st, p, cfg_p, prompt_attend)

        st_in = dict(ssd_tail=_tail_block(state_ssm_conv[i], SSD_CONV - 1),
                     ssd_h=state_ssm[i].reshape(bs, SSD_HEADS // 2, LANES, SSD_STATE),
                     rwkv_tail=_tail_block(state_rwkv_shift[i][:, None, :], 1),
                     rwkv_s=_pack_rwkv_state(state_rwkv[i]),
                     ffn_tail=_tail_block(state_ffn_conv[i], FFN_CONV - 1))

        def sample_attend(q, k, v, ckv, kpe, i=i, p=p):
            new_c = jnp.pad(ckv.reshape(bs, tsp, MLA_KV_RANK), ((0, 0), (0, PAGE_SIZE - tsp), (0, 0)))
            new_r_t = jnp.pad(jnp.swapaxes(kpe.reshape(bs, tsp, LANES), 1, 2), ((0, 0), (0, 0), (0, PAGE_SIZE - tsp)))
            return _paged_attention(page_table, q, cache_kv_latent, rope_pool_t, i, new_c, new_r_t, p, ts)

        xs, st_s, ckv_s, kpe_s = _layer(xs, st_in, p, cfg_s, sample_attend)

        for lst, st, ckv, kpe, b_, t_, tv in ((new_p, st_p, ckv_p, kpe_p, bp, tp, tp),
                                              (new_s, st_s, ckv_s, kpe_s, bs, tsp, ts)):
            lst[0].append(ckv.reshape(b_, t_, MLA_KV_RANK)[:, :tv])
            lst[1].append(kpe.reshape(b_, t_, LANES)[:, :tv, :MLA_ROPE])
            lst[2].append(st["ssd_h"].reshape(b_, SSD_HEADS, SSD_HEAD_DIM, SSD_STATE))
            lst[3].append(st["ssd_tail"][:, SUBLANES - (SSD_CONV - 1):])
            lst[4].append(_unpack_rwkv_state(st["rwkv_s"]))
            lst[5].append(st["rwkv_tail"][:, SUBLANES - 1])
            lst[6].append(st["ffn_tail"][:, SUBLANES - (FFN_CONV - 1):])

    outs_p = [jnp.stack(v_, axis=0) for v_ in new_p]
    outs_s = [jnp.stack(v_, axis=0) for v_ in new_s]
    y_p = xp.reshape(bp, tp, d)
    y_s = xs.reshape(bs, tsp, d)[:, :ts]
    return (y_p, y_s, *outs_p, *outs_s)
```

```python
import functools
import math

import numpy as np
import jax
import jax.numpy as jnp
from jax import lax
from jax.experimental import pallas as pl
from jax.experimental.pallas import tpu as pltpu

F32 = jnp.float32
BF16 = jnp.bfloat16

D_MODEL = 1024
PAGE_SIZE = 128

SSD_HEAD_DIM = 64
SSD_INNER = 1024
SSD_HEADS = 16
SSD_GROUPS = 2
SSD_STATE = 128
SSD_CONV = 4
SSD_CHUNK = 128
SSD_CONV_DIM = SSD_INNER + 2 * SSD_GROUPS * SSD_STATE
SSD_IN = SSD_INNER + SSD_CONV_DIM + SSD_HEADS
SSD_IN_PAD = SSD_INNER + SSD_CONV_DIM + 128

RWKV_HEAD_DIM = 64
RWKV_DIM = 1024
RWKV_HEADS = 16
RWKV_DECAY_LORA = 64
RWKV_A_LORA = 64
RWKV_GATE_LORA = 128
RWKV_IN = 3 * RWKV_DIM + RWKV_DECAY_LORA + RWKV_A_LORA + RWKV_GATE_LORA
RWKV_LN_EPS = 64e-5

MLA_HEADS = 8
MLA_NOPE = 128
MLA_ROPE = 64
MLA_QK = MLA_NOPE + MLA_ROPE
MLA_V = 128
MLA_Q_RANK = 512
MLA_KV_RANK = 256
MLA_IN = MLA_Q_RANK + MLA_KV_RANK + MLA_ROPE
MLA_IN_PAD = MLA_Q_RANK + MLA_KV_RANK + 128
MLA_QK_PAD = 256
ROPE_BASE = 10000.0
MLA_SCALE = MLA_QK ** -0.5

N_BRANCH = 3
D_FF = 2816
FFN_CONV = 3
NORM_EPS = 1e-6

SUBLANES = 8
LANES = 128
SAMPLE_T_PAD = 8
NEG_BIG = -1e30

VMEM_LIMIT = 56 * 1024 * 1024


def _cparams(*sem):
    return pltpu.CompilerParams(dimension_semantics=sem, vmem_limit_bytes=VMEM_LIMIT)


def _full(shape):
    nd = len(shape)
    return pl.BlockSpec(shape, lambda *_: (0,) * nd)


def _rms(x, eps=NORM_EPS):
    return x * lax.rsqrt(jnp.mean(x * x, axis=-1, keepdims=True) + eps)


def _sigmoid(x):
    return 1.0 / (1.0 + jnp.exp(-x))


def _silu(x):
    return x * _sigmoid(x)


def _softplus(x):
    return jnp.maximum(x, 0.0) + jnp.log(1.0 + jnp.exp(-jnp.abs(x)))


def _dot(a, b):
    return jnp.dot(a, b, preferred_element_type=F32)


def _dot_nt(a, b):
    return lax.dot_general(a, b, (((1,), (1,)), ((), ())), preferred_element_type=F32)


def _dot_tn(a, b):
    return lax.dot_general(a, b, (((0,), (0,)), ((), ())), preferred_element_type=F32)


def _split_hi_lo(x):
    hi = x.astype(BF16)
    lo = (x - hi.astype(F32)).astype(BF16)
    return jnp.concatenate([hi, lo], axis=1)


def _in_proj(x_ref, g_ref, w_ref):
    return _dot((_rms(x_ref[...]) * g_ref[...]).astype(BF16), w_ref[...])


def _resident(shape):
    nd = len(shape)
    return pl.BlockSpec(shape, lambda *_: (0,) * nd, pipeline_mode=pl.Buffered(1))


def _ssd_kernel(x_ref, g_ref, w_ref, tail0_ref, h0_ref, cw_ref, cb_ref, dtb_ref, alog_ref, dsk_ref, ng_ref,
                o_ref, tail_ref, h_ref, buf, *, t_valid):
    L = SSD_CHUNK
    c = pl.program_id(1)

    @pl.when(c == 0)
    def _():
        buf[0:SUBLANES, :] = tail0_ref[...]
        h_ref[...] = h0_ref[...]

    u = _in_proj(x_ref, g_ref, w_ref)
    z = u[:, :SSD_INNER]
    dt_raw = u[:, SSD_INNER + SSD_CONV_DIM:]
    buf[SUBLANES:SUBLANES + L, :] = u[:, SSD_INNER:SSD_INNER + SSD_CONV_DIM]
    acc = cb_ref[...]
    for j in range(SSD_CONV):
        acc = acc + buf[pl.ds(SUBLANES - (SSD_CONV - 1) + j, L), :] * cw_ref[j:j + 1, :]
    tail = buf[pl.ds(t_valid, SUBLANES), :]
    tail_ref[...] = tail
    buf[0:SUBLANES, :] = tail
    xbc = _silu(acc)
    xs = xbc[:, :SSD_INNER]

    dt = _softplus(dt_raw + dtb_ref[...])
    if t_valid < L:
        row = lax.broadcasted_iota(jnp.int32, dt.shape, 0)
        dt = jnp.where(row < t_valid, dt, 0.0)
    a = -jnp.exp(alog_ref[...])
    da = dt * a
    ri = lax.broadcasted_iota(jnp.int32, (L, L), 0)
    ci = lax.broadcasted_iota(jnp.int32, (L, L), 1)
    causal = ri >= ci
    tri = jnp.where(causal, 1.0, 0.0).astype(F32)
    cs = jnp.dot(tri, da, preferred_element_type=F32, precision=lax.Precision.HIGHEST)
    cs_t = cs.T
    total = cs[L - 1:L, :]
    etot = jnp.exp(total)
    lane = lax.broadcasted_iota(jnp.int32, (L, LANES), 1)
    lo_half = lane < SSD_HEAD_DIM
    row_lo = lax.broadcasted_iota(jnp.int32, (LANES, LANES), 0) < SSD_HEAD_DIM

    def pair_cols(m, j0):
        return jnp.where(lo_half, m[:, j0:j0 + 1], m[:, j0 + 1:j0 + 2])

    hpg = SSD_HEADS // SSD_GROUPS
    y_parts = []
    for g in range(SSD_GROUPS):
        b_g = xbc[:, SSD_INNER + g * SSD_STATE:SSD_INNER + (g + 1) * SSD_STATE].astype(BF16)
        c0 = SSD_INNER + SSD_GROUPS * SSD_STATE + g * SSD_STATE
        c_g = xbc[:, c0:c0 + SSD_STATE].astype(BF16)
        cb = _dot_nt(c_g, b_g)
        for m in range(hpg // 2):
            q = g * (hpg // 2) + m
            j0 = 2 * q
            xs_p = xs[:, q * LANES:(q + 1) * LANES]
            xd = xs_p * pair_cols(dt, j0)
            xd_b = xd.astype(BF16)
            ys = []
            for e in range(2):
                j = j0 + e
                diff = cs[:, j:j + 1] - cs_t[j:j + 1, :]
                lm = jnp.where(causal, jnp.exp(jnp.where(causal, diff, 0.0)), 0.0)
                ys.append(_dot((cb * lm).astype(BF16), xd_b))
            y_diag = jnp.where(lo_half, ys[0], ys[1])
            hp = h_ref[q]
            cs_p = pair_cols(cs, j0)
            y_off = _dot_nt(c_g, hp.astype(BF16)) * jnp.exp(cs_p)
            tot_p = jnp.where(lo_half[0:1, :], total[:, j0:j0 + 1], total[:, j0 + 1:j0 + 2])
            dte = jnp.exp(tot_p - cs_p)
            contrib = _dot_tn((xd * dte).astype(BF16), b_g)
            decay = jnp.where(row_lo, etot[:, j0:j0 + 1], etot[:, j0 + 1:j0 + 2])
            h_ref[q] = hp * decay + contrib
            y_parts.append(y_diag + y_off + dsk_ref[:, q * LANES:(q + 1) * LANES] * xs_p)
    y = jnp.concatenate(y_parts, axis=1) * _silu(z)
    gw = SSD_INNER // SSD_GROUPS
    outs = [_rms(y[:, g * gw:(g + 1) * gw]) * ng_ref[:, g * gw:(g + 1) * gw] for g in range(SSD_GROUPS)]
    o_ref[...] = jnp.concatenate(outs, axis=1).astype(BF16)


def _ssd(x, tail0, h0, p, t_valid):
    bsz = h0.shape[0]
    n, d = x.shape
    L = SSD_CHUNK
    nch = n // bsz // L
    kern = functools.partial(_ssd_kernel, t_valid=t_valid)
    return pl.pallas_call(
        kern,
        grid=(bsz, nch),
        in_specs=[pl.BlockSpec((L, d), lambda b, c: (b * nch + c, 0)), _full((1, d)),
                  _resident((d, SSD_IN_PAD)),
                  pl.BlockSpec((None, SUBLANES, SSD_CONV_DIM), lambda b, c: (b, 0, 0)),
                  pl.BlockSpec((None, SSD_HEADS // 2, LANES, SSD_STATE), lambda b, c: (b, 0, 0, 0)),
                  _full((SSD_CONV, SSD_CONV_DIM)), _full((1, SSD_CONV_DIM)),
                  _full((1, LANES)), _full((1, LANES)), _full((1, SSD_INNER)), _full((1, SSD_INNER))],
        out_specs=[pl.BlockSpec((L, SSD_INNER), lambda b, c: (b * nch + c, 0)),
                   pl.BlockSpec((None, SUBLANES, SSD_CONV_DIM), lambda b, c: (b, 0, 0)),
                   pl.BlockSpec((None, SSD_HEADS // 2, LANES, SSD_STATE), lambda b, c: (b, 0, 0, 0))],
        out_shape=[jax.ShapeDtypeStruct((n, SSD_INNER), BF16),
                   jax.ShapeDtypeStruct((bsz, SUBLANES, SSD_CONV_DIM), F32),
                   jax.ShapeDtypeStruct((bsz, SSD_HEADS // 2, LANES, SSD_STATE), F32)],
        scratch_shapes=[pltpu.VMEM((SUBLANES + L, SSD_CONV_DIM), F32)],
        compiler_params=_cparams("parallel", "arbitrary"),
    )(x, p["norm_mix_g"], p["w_ssd"], tail0, h0, p["ssd_conv_w"], p["ssd_conv_b"], p["ssd_dt_bias"],
      p["ssd_a_log"], p["ssd_d"], p["ssd_norm_g"])


def _seg_sum_bcast(x, seg2, seg_t2):
    s16 = _dot(_split_hi_lo(x), seg2)
    return _dot(_split_hi_lo(s16), seg_t2)


def _rwkv_pre_kernel(x_ref, g_in_ref, w_in_ref, tail0_ref, mu_ref, w0_ref, wup_ref, a0_ref, aup_ref, gup_ref, kk_ref,
                     ka_ref, seg2_ref, segt2_ref,
                     r_ref, w_ref, k_ref, v_ref, al_ref, be_ref, g_ref, tail_ref, buf, *, tm, t_valid):
    j = pl.program_id(1)

    @pl.when(j == 0)
    def _():
        buf[0:SUBLANES, :] = tail0_ref[...]

    u = _in_proj(x_ref, g_in_ref, w_in_ref)
    buf[SUBLANES:SUBLANES + tm, :] = u
    prev = buf[pl.ds(SUBLANES - 1, tm), :]
    tail = buf[pl.ds(t_valid, SUBLANES), :]
    tail_ref[...] = tail
    buf[0:SUBLANES, :] = tail

    d = RWKV_DIM
    f = u + mu_ref[...] * (prev - u)
    r, k, v = f[:, :d], f[:, d:2 * d], f[:, 2 * d:3 * d]
    lo = f[:, 3 * d:3 * d + LANES]
    glo = f[:, 3 * d + LANES:]
    ww = w0_ref[...] + _dot(jnp.tanh(lo).astype(BF16), wup_ref[...])
    w_log = -_softplus(-ww) - 0.5
    a = _sigmoid(a0_ref[...] + _dot(lo.astype(BF16), aup_ref[...]))
    kk = k * kk_ref[...]
    ssb = _seg_sum_bcast(kk * kk, seg2_ref[...], segt2_ref[...])
    kkn = kk / jnp.maximum(jnp.sqrt(ssb), 1e-12)
    r_ref[...] = r
    w_ref[...] = -jnp.exp(w_log)
    k_ref[...] = k * (1.0 + (a - 1.0) * ka_ref[...])
    v_ref[...] = v
    al_ref[...] = -kkn
    be_ref[...] = kkn * a
    g_ref[...] = _dot(_sigmoid(glo).astype(BF16), gup_ref[...])


def _rwkv_pre(x, tail0, p, tm, t_valid):
    bsz = tail0.shape[0]
    n = x.shape[0]
    nt = n // bsz // tm
    d = RWKV_DIM
    kern = functools.partial(_rwkv_pre_kernel, tm=tm, t_valid=t_valid)
    row = pl.BlockSpec((tm, d), lambda b, j: (b * nt + j, 0))
    return pl.pallas_call(
        kern,
        grid=(bsz, nt),
        in_specs=[row, _full((1, D_MODEL)), _resident((D_MODEL, RWKV_IN)),
                  pl.BlockSpec((None, SUBLANES, RWKV_IN), lambda b, j: (b, 0, 0)),
                  _full((1, RWKV_IN)), _full((1, d)), _full((LANES, d)), _full((1, d)),
                  _full((LANES, d)), _full((LANES, d)), _full((1, d)), _full((1, d)),
                  _full((2 * d, LANES)), _full((2 * LANES, d))],
        out_specs=[row] * 7 + [pl.BlockSpec((None, SUBLANES, RWKV_IN), lambda b, j: (b, 0, 0))],
        out_shape=[jax.ShapeDtypeStruct((n, d), F32)] * 7
                  + [jax.ShapeDtypeStruct((bsz, SUBLANES, RWKV_IN), F32)],
        scratch_shapes=[pltpu.VMEM((SUBLANES + tm, RWKV_IN), F32)],
        compiler_params=_cparams("parallel", "arbitrary"),
    )(x, p["norm_mix_g"], p["w_rwkv"], tail0, p["rwkv_shift_mu"], p["rwkv_w0"], p["rwkv_w_up"], p["rwkv_a0"],
      p["rwkv_a_up"],
      p["rwkv_g_up"], p["rwkv_k_k"], p["rwkv_k_a"], p["seg2"], p["seg_t2"])


RWKV_CHUNK = 64
RWKV_SOLVE_BLOCK = 16
RWKV_PAIRS = RWKV_HEADS // 2


def _hi_lo(x):
    hi = x.astype(BF16)
    return hi, (x - hi.astype(F32)).astype(BF16)


def _mm(a, b):
    return _dot(a.astype(BF16), b.astype(BF16))


def _mm2(a, b):
    ah, al = _hi_lo(a)
    m = a.shape[0]
    both = _dot(jnp.concatenate([ah, al], axis=0), b.astype(BF16))
    return both[:m] + both[m:]


def _rwkv_chunk_kernel(r_ref, lw_ref, k_ref, v_ref, al_ref, be_ref, s0_ref, y_ref, s_ref, *, nchunks):
    c = pl.program_id(1)

    @pl.when(c == 0)
    def _():
        s_ref[...] = s0_ref[...]

    C, HD = RWKV_CHUNK, RWKV_HEAD_DIM
    r2 = lax.broadcasted_iota(jnp.int32, (LANES, LANES), 0)
    c2 = lax.broadcasted_iota(jnp.int32, (LANES, LANES), 1)
    eye = jnp.where(r2 == c2, 1.0, 0.0).astype(F32)
    diag_blk = (r2 // RWKV_SOLVE_BLOCK) == (c2 // RWKV_SOLVE_BLOCK)
    same_head = (r2 // HD) == (c2 // HD)
    tri = jnp.where(lax.broadcasted_iota(jnp.int32, (C, C), 0) >= lax.broadcasted_iota(jnp.int32, (C, C), 1),
                    1.0, 0.0).astype(F32)
    t_i = lax.broadcasted_iota(jnp.int32, (C, LANES), 0)
    lane = lax.broadcasted_iota(jnp.int32, (C, LANES), 1)
    lo_half = lane < HD
    hi_half = lane >= HD
    strict = (lane % HD) < t_i
    incl = (lane % HD) <= t_i
    zeros = jnp.zeros((C, LANES), F32)

    def body(ci, carry):
        rows = pl.ds(pl.multiple_of(ci * C, C), C)
        cum_all = jnp.dot(tri, lw_ref[rows, :], preferred_element_type=F32, precision=lax.Precision.HIGHEST)
        pairs = range(RWKV_PAIRS)
        sls = [slice(p * LANES, (p + 1) * LANES) for p in pairs]
        cum = [cum_all[:, sl] for sl in sls]
        be = [be_ref[rows, sl] for sl in sls]
        kk = [k_ref[rows, sl] for sl in sls]
        vv = [v_ref[rows, sl] for sl in sls]
        g_inv = [jnp.exp(-cum[p]) for p in pairs]
        ar = [jnp.concatenate([al_ref[rows, sls[p]] * jnp.exp(cum[p] - lw_ref[rows, sls[p]]),
                               r_ref[rows, sls[p]] * jnp.exp(cum[p])], axis=0) for p in pairs]
        ar_b = [ar[p].astype(BF16) for p in pairs]
        bt = [be[p] * g_inv[p] for p in pairs]
        kt = [kk[p] * g_inv[p] for p in pairs]
        x01 = [_dot_nt(ar_b[p], jnp.concatenate(
            [jnp.where(lo_half, bt[p], 0.0), jnp.where(lo_half, kt[p], 0.0),
             jnp.where(hi_half, kt[p], 0.0), jnp.where(hi_half, bt[p], 0.0)], axis=0).astype(BF16)) for p in pairs]
        x0 = [x01[p][:, :LANES] for p in pairs]
        x1 = [x01[p][:, LANES:] for p in pairs]
        nb = [jnp.concatenate([jnp.where(lo_half & strict, x0[p][:C], 0.0),
                               jnp.where(hi_half & strict, x1[p][:C], 0.0)], axis=0) for p in pairs]
        sbd = [s_ref[p] for p in pairs]

        def read_state(p):
            s_hi, s_lo = _hi_lo(sbd[p])
            return _dot_nt(jnp.concatenate([ar_b[p], ar_b[p]], axis=1),
                           jnp.concatenate([s_hi, s_lo], axis=1))

        ars = [read_state(p) for p in pairs]

        def strict_ak(p):
            lhs = jnp.concatenate([jnp.where(hi_half & strict, x0[p][:C], 0.0),
                                   jnp.where(lo_half & strict, x1[p][:C], 0.0)], axis=0)
            rhs_w = jnp.concatenate([jnp.concatenate([zeros, vv[p]], axis=0),
                                     jnp.concatenate([vv[p], zeros], axis=0)], axis=1)
            both = _mm(lhs, rhs_w)
            return jnp.where(lo_half, both[:C, :LANES], both[C:, LANES:])

        rhs = [ars[p][:C] + strict_ak(p) for p in pairs]
        rhs2 = [jnp.concatenate([rhs[p], rhs[p]], axis=0) for p in pairs]
        nd = [jnp.where(diag_blk, nb[p], 0.0) for p in pairs]
        loff = [nb[p] - nd[p] for p in pairs]
        pm = [eye + nd[p] for p in pairs]
        n2 = [_mm2(nd[p], nd[p]) for p in pairs]
        pm = [pm[p] + _mm2(pm[p], n2[p]) for p in pairs]
        n4 = [_mm(n2[p], n2[p]) for p in pairs]
        pm = [pm[p] + _mm(pm[p], n4[p]) for p in pairs]
        n8 = [_mm(n4[p], n4[p]) for p in pairs]
        td = [pm[p] + _mm(pm[p], n8[p]) for p in pairs]
        m1 = [_mm2(td[p], loff[p]) for p in pairs]
        m2 = [_mm(m1[p], m1[p]) for p in pairs]
        x_a = [_mm2(td[p], rhs2[p]) for p in pairs]
        x_b = [x_a[p] + _mm(m2[p], x_a[p]) for p in pairs]
        u2 = [x_b[p] + _mm(m1[p], x_b[p]) for p in pairs]
        up = [jnp.where(lo_half, u2[p][:C], u2[p][C:]) for p in pairs]
        for p in pairs:
            uv = jnp.concatenate([up[p], vv[p]], axis=0)
            vu = jnp.concatenate([vv[p], up[p]], axis=0)
            both = _mm(jnp.concatenate([jnp.where(incl, x0[p][C:], 0.0), jnp.where(incl, x1[p][C:], 0.0)], axis=0),
                       jnp.concatenate([uv, vu], axis=1))
            y_ref[rows, sls[p]] = ars[p][C:] + jnp.where(lo_half, both[:C, :LANES], both[C:, LANES:])
            cum_c = cum[p][C - 1:C, :]
            g_end = jnp.exp(cum_c - cum[p])
            bkh = jnp.concatenate([be[p] * g_end, kk[p] * g_end], axis=0)
            upd = _dot_tn(uv.astype(BF16), bkh.astype(BF16))
            s_ref[p] = sbd[p] * jnp.exp(cum_c) + jnp.where(same_head, upd, 0.0)
        return carry

    lax.fori_loop(0, nchunks, body, 0)


def _rwkv_scan(r, lw, k, v, al, be, s0, bsz, tc):
    n, d = r.shape
    t_total = n // bsz
    nt = t_total // tc
    kern = functools.partial(_rwkv_chunk_kernel, nchunks=tc // RWKV_CHUNK)
    row = pl.BlockSpec((tc, d), lambda b, c: (b * nt + c, 0))
    st = pl.BlockSpec((None, RWKV_PAIRS, LANES, LANES), lambda b, c: (b, 0, 0, 0))
    return pl.pallas_call(
        kern,
        grid=(bsz, nt),
        in_specs=[row] * 6 + [st],
        out_specs=[row, st],
        out_shape=[jax.ShapeDtypeStruct((n, d), F32),
                   jax.ShapeDtypeStruct((bsz, RWKV_PAIRS, LANES, LANES), F32)],
        compiler_params=_cparams("parallel", "arbitrary"),
    )(r, lw, k, v, al, be, s0)


def _rwkv_post_kernel(y_ref, r_ref, k_ref, v_ref, g_ref, lng_ref, lnb_ref, rk_ref, seg2_ref, segt2_ref, o_ref):
    seg2, seg_t2 = seg2_ref[...], segt2_ref[...]
    y = y_ref[...]
    inv = 1.0 / RWKV_HEAD_DIM
    dlt = y - _seg_sum_bcast(y, seg2, seg_t2) * inv
    var = _seg_sum_bcast(dlt * dlt, seg2, seg_t2) * inv
    yn = dlt * lax.rsqrt(var + RWKV_LN_EPS) * lng_ref[...] + lnb_ref[...]
    bonus = _seg_sum_bcast(r_ref[...] * k_ref[...] * rk_ref[...], seg2, seg_t2) * v_ref[...]
    o_ref[...] = ((yn + bonus) * g_ref[...]).astype(BF16)


def _rwkv_post(y, r, k, v, g, p, tm):
    n, d = y.shape
    row = pl.BlockSpec((tm, d), lambda i: (i, 0))
    return pl.pallas_call(
        _rwkv_post_kernel,
        grid=(n // tm,),
        in_specs=[row] * 5 + [_full((1, d))] * 3 + [_full((2 * d, LANES)), _full((2 * LANES, d))],
        out_specs=row,
        out_shape=jax.ShapeDtypeStruct((n, d), BF16),
        compiler_params=_cparams("parallel"),
    )(y, r, k, v, g, p["rwkv_ln_g"], p["rwkv_ln_b"], p["rwkv_r_k"], p["seg2"], p["seg_t2"])


def _mla_prep_kernel(x_ref, g_in_ref, w_in_ref, cos_ref, s1_ref, s2_ref, qag_ref, wqn_ref, wqr_ref, kvg_ref,
                     wkb_ref, wvb_ref,
                     gqn_ref, gqr_ref, gkn_ref, gkr_ref, q_ref, k_ref, v_ref, ckv_ref, kpe_ref):
    u = _in_proj(x_ref, g_in_ref, w_in_ref)
    cq = u[:, :MLA_Q_RANK]
    ckv_raw = u[:, MLA_Q_RANK:MLA_Q_RANK + MLA_KV_RANK]
    kr_raw = u[:, MLA_Q_RANK + MLA_KV_RANK:]
    cos, s1, s2 = cos_ref[...], s1_ref[...], s2_ref[...]

    def rope(blk):
        return (blk * cos + pltpu.roll(blk, LANES - MLA_ROPE // 2, 1) * s1
                + pltpu.roll(blk, MLA_ROPE // 2, 1) * s2)

    cqn = (_rms(cq) * qag_ref[...]).astype(BF16)
    qn = _dot(cqn, wqn_ref[...])
    qr = _dot(cqn, wqr_ref[...])
    ckv = _rms(ckv_raw) * kvg_ref[...]
    ckv_ref[...] = ckv
    kpe = rope(kr_raw)
    kpe_ref[...] = kpe
    ckb = ckv.astype(BF16)
    kn = _dot(ckb, wkb_ref[...])
    v_ref[...] = _dot(ckb, wvb_ref[...]).astype(BF16)
    ss_kpe = jnp.sum(kpe * kpe, axis=-1, keepdims=True)
    gqn, gqr, gkn, gkr = gqn_ref[...], gqr_ref[...], gkn_ref[...], gkr_ref[...]
    for h in range(MLA_HEADS):
        sl = slice(h * MLA_NOPE, (h + 1) * MLA_NOPE)
        o0 = h * MLA_QK_PAD
        qn_h = qn[:, sl]
        qr_h = rope(qr[:, sl])
        ss = jnp.sum(qn_h * qn_h, axis=-1, keepdims=True) + jnp.sum(qr_h * qr_h, axis=-1, keepdims=True)
        rn = lax.rsqrt(ss * (1.0 / MLA_QK) + NORM_EPS) * MLA_SCALE
        q_ref[:, o0:o0 + MLA_NOPE] = (qn_h * rn * gqn).astype(q_ref.dtype)
        q_ref[:, o0 + MLA_NOPE:o0 + MLA_QK_PAD] = (qr_h * rn * gqr).astype(q_ref.dtype)
        kn_h = kn[:, sl]
        ssk = jnp.sum(kn_h * kn_h, axis=-1, keepdims=True) + ss_kpe
        rnk = lax.rsqrt(ssk * (1.0 / MLA_QK) + NORM_EPS)
        k_ref[:, o0:o0 + MLA_NOPE] = (kn_h * rnk * gkn).astype(BF16)
        k_ref[:, o0 + MLA_NOPE:o0 + MLA_QK_PAD] = (kpe * rnk * gkr).astype(BF16)


def _mla_prep(x, rope_tabs, p, tm, q_dtype):
    n, d = x.shape
    ttab = rope_tabs[0].shape[0]
    ntab = ttab // tm
    hq = MLA_HEADS * MLA_QK_PAD
    hv = MLA_HEADS * MLA_V
    tab = pl.BlockSpec((tm, LANES), lambda i: (i % ntab, 0))
    return pl.pallas_call(
        _mla_prep_kernel,
        grid=(n // tm,),
        in_specs=[pl.BlockSpec((tm, d), lambda i: (i, 0)), _full((1, d)), _full((d, MLA_IN_PAD)), tab, tab, tab,
                  _full((1, MLA_Q_RANK)), _full((MLA_Q_RANK, hv)), _full((MLA_Q_RANK, hv)),
                  _full((1, MLA_KV_RANK)), _full((MLA_KV_RANK, hv)), _full((MLA_KV_RANK, hv)),
                  _full((1, LANES)), _full((1, LANES)), _full((1, LANES)), _full((1, LANES))],
        out_specs=[pl.BlockSpec((tm, hq), lambda i: (i, 0)), pl.BlockSpec((tm, hq), lambda i: (i, 0)),
                   pl.BlockSpec((tm, hv), lambda i: (i, 0)),
                   pl.BlockSpec((tm, MLA_KV_RANK), lambda i: (i, 0)),
                   pl.BlockSpec((tm, LANES), lambda i: (i, 0))],
        out_shape=[jax.ShapeDtypeStruct((n, hq), q_dtype), jax.ShapeDtypeStruct((n, hq), BF16),
                   jax.ShapeDtypeStruct((n, hv), BF16), jax.ShapeDtypeStruct((n, MLA_KV_RANK), F32),
                   jax.ShapeDtypeStruct((n, LANES), F32)],
        compiler_params=_cparams("parallel"),
    )(x, p["norm_mix_g"], p["w_mla"], *rope_tabs, p["mla_q_a_g"], p["wq_nope"], p["wq_rope"], p["mla_kv_a_g"], p["w_kb"], p["w_vb"],
      p["gq_nope"], p["gq_rope"], p["gk_nope"], p["gk_rope"])


FLASH_HEAD_GROUP = 2


def _flash_kernel(q_ref, k_ref, v_ref, o_ref, m_sc, acc_sc, *, tq):
    qi = pl.program_id(1)
    ki = pl.program_id(2)
    nblk = tq // LANES

    @pl.when(ki == 0)
    def _():
        m_sc[...] = jnp.full_like(m_sc, NEG_BIG)
        acc_sc[...] = jnp.zeros_like(acc_sc)

    ones = jnp.ones((tq, LANES), BF16)

    def update(diagonal):
        for h0 in range(0, MLA_HEADS, FLASH_HEAD_GROUP):
            heads = range(h0, h0 + FLASH_HEAD_GROUP)
            s = {h: _dot_nt(q_ref[:, h * MLA_QK_PAD:(h + 1) * MLA_QK_PAD],
                            k_ref[:, h * MLA_QK_PAD:(h + 1) * MLA_QK_PAD]) for h in heads}
            if diagonal:
                row = lax.broadcasted_iota(jnp.int32, (tq, tq), 0)
                col = lax.broadcasted_iota(jnp.int32, (tq, tq), 1)
                s = {h: jnp.where(col <= row, s[h], NEG_BIG) for h in heads}
            m_prev = {h: m_sc[h] for h in heads}
            m_new = {h: jnp.maximum(m_prev[h], jnp.max(s[h], axis=-1, keepdims=True)) for h in heads}
            pr = {h: jnp.concatenate([jnp.exp(s[h][:, j * LANES:(j + 1) * LANES] - m_new[h]) for j in range(nblk)],
                                     axis=1).astype(BF16) for h in heads}
            corr = {h: jnp.exp(m_prev[h] - m_new[h]) for h in heads}
            for h in heads:
                v_ext = jnp.concatenate([v_ref[:, h * MLA_V:(h + 1) * MLA_V], ones], axis=1)
                acc_sc[h] = acc_sc[h] * jnp.concatenate([corr[h], corr[h]], axis=1) + _dot(pr[h], v_ext)
                m_sc[h] = m_new[h]

    @pl.when(ki < qi)
    def _():
        update(False)

    @pl.when(ki == qi)
    def _():
        update(True)
        for h in range(MLA_HEADS):
            acc = acc_sc[h]
            o_ref[:, h * MLA_V:(h + 1) * MLA_V] = (acc[:, :MLA_V] / acc[:, MLA_V:]).astype(BF16)


def _flash(q, k, v, bsz, tq):
    n = q.shape[0]
    nq = n // bsz // tq
    hq = MLA_HEADS * MLA_QK_PAD
    hv = MLA_HEADS * MLA_V
    kern = functools.partial(_flash_kernel, tq=tq)
    return pl.pallas_call(
        kern,
        grid=(bsz, nq, nq),
        in_specs=[pl.BlockSpec((tq, hq), lambda b, i, j: (b * nq + i, 0)),
                  pl.BlockSpec((tq, hq), lambda b, i, j: (b * nq + jnp.minimum(i, j), 0)),
                  pl.BlockSpec((tq, hv), lambda b, i, j: (b * nq + jnp.minimum(i, j), 0))],
        out_specs=pl.BlockSpec((tq, hv), lambda b, i, j: (b * nq + i, 0)),
        out_shape=jax.ShapeDtypeStruct((n, hv), BF16),
        scratch_shapes=[pltpu.VMEM((MLA_HEADS, tq, LANES), F32),
                        pltpu.VMEM((MLA_HEADS, tq, MLA_V + LANES), F32)],
        compiler_params=_cparams("parallel", "parallel", "arbitrary"),
    )(q, k, v)


PAGES_PER_STEP = 16
PAGED_ROWS = MLA_HEADS * SAMPLE_T_PAD


def _paged_kernel(pt_ref, q_ref, *refs, t_new, npp):
    lat_refs = refs[:npp]
    rope_refs = refs[npp:2 * npp]
    (newc_ref, newr_ref, wkbt_ref, wvb_ref, gkn_ref, gkr_ref, o_ref,
     m_sc, l_sc, acc_sc, wq_sc, qr_sc) = refs[2 * npp:]
    step = pl.program_id(1)
    nkn = MLA_HEADS * MLA_NOPE

    @pl.when(step == 0)
    def _():
        m_sc[...] = jnp.full_like(m_sc, NEG_BIG)
        l_sc[...] = jnp.zeros_like(l_sc)
        acc_sc[...] = jnp.zeros_like(acc_sc)
        q = q_ref[...]
        gkn, gkr = gkn_ref[...], gkr_ref[...]
        wkbt = wkbt_ref[...]
        wq_sc[0:nkn, :] = wkbt
        for h in range(MLA_HEADS):
            o0 = h * MLA_QK_PAD
            rows = slice(h * SAMPLE_T_PAD, (h + 1) * SAMPLE_T_PAD)
            qn = (q[:, o0:o0 + MLA_NOPE] * gkn).astype(BF16)
            wq_sc[nkn + h * SAMPLE_T_PAD:nkn + (h + 1) * SAMPLE_T_PAD, :] = _dot(
                qn, wkbt[h * MLA_NOPE:(h + 1) * MLA_NOPE, :]).astype(BF16)
            qr_sc[rows, :] = (q[:, o0 + MLA_NOPE:o0 + MLA_QK_PAD] * gkr).astype(BF16)

    wq = wq_sc[...]
    qr = qr_sc[...]

    def scores(cbs, krs):
        n = len(cbs)
        big = [_dot_nt(wq, cbs[i]) for i in range(n)]
        rope = [_dot(qr, krs[i].astype(BF16)) for i in range(n)]
        ssr = [jnp.sum(krs[i] * krs[i], axis=0, keepdims=True) for i in range(n)]
        out = []
        for i in range(n):
            rn = []
            for h in range(MLA_HEADS):
                kn_h = big[i][h * MLA_NOPE:(h + 1) * MLA_NOPE]
                ss = jnp.sum(kn_h * kn_h, axis=0, keepdims=True) + ssr[i]
                rn.append(jnp.broadcast_to(lax.rsqrt(ss * (1.0 / MLA_QK) + NORM_EPS),
                                           (SAMPLE_T_PAD, ss.shape[1])))
            out.append((big[i][nkn:] + rope[i]) * jnp.concatenate(rn, axis=0))
        return out

    def softmax_update(s_list, cb_list):
        m_prev = m_sc[...]
        m_new = m_prev
        for s in s_list:
            m_new = jnp.maximum(m_new, jnp.max(s, axis=-1, keepdims=True))
        corr = jnp.exp(m_prev - m_new)
        l_new = l_sc[...] * corr
        acc = acc_sc[...] * corr
        for s, cb in zip(s_list, cb_list):
            pr = jnp.exp(s - m_new)
            l_new = l_new + jnp.sum(pr, axis=-1, keepdims=True)
            acc = acc + _dot(pr.astype(BF16), cb)
        l_sc[...] = l_new
        acc_sc[...] = acc
        m_sc[...] = m_new

    grp = 2 if npp % 2 == 0 else 1
    rope_pad = jnp.zeros((LANES - MLA_ROPE, grp * PAGE_SIZE), F32)
    cbs, krs = [], []
    for i in range(0, npp, grp):
        cbs.append(jnp.concatenate([lat_refs[i + j][...] for j in range(grp)], axis=0).astype(BF16))
        krs.append(jnp.concatenate([jnp.concatenate([rope_refs[i + j][...] for j in range(grp)], axis=1),
                                    rope_pad], axis=0))
    softmax_update(scores(cbs, krs), cbs)

    @pl.when(step == pl.num_programs(1) - 1)
    def _():
        cb = newc_ref[...].astype(BF16)
        s = scores([cb], [newr_ref[...]])[0]
        key = lax.broadcasted_iota(jnp.int32, s.shape, 1)
        qry = lax.broadcasted_iota(jnp.int32, s.shape, 0) % SAMPLE_T_PAD
        softmax_update([jnp.where((key <= qry) & (key < t_new), s, NEG_BIG)], [cb])
        o_lat = (acc_sc[...] / l_sc[...]).astype(BF16)
        wvb = wvb_ref[...]
        for h in range(MLA_HEADS):
            o_ref[:, h * MLA_V:(h + 1) * MLA_V] = _dot(
                o_lat[h * SAMPLE_T_PAD:(h + 1) * SAMPLE_T_PAD],
                wvb[:, h * MLA_V:(h + 1) * MLA_V])


def _paged_attention(page_table, q, lat_pool, rope_pool_t, layer, new_c, new_r_t, p, t_new):
    bsz, n_pages = page_table.shape
    npp = math.gcd(PAGES_PER_STEP, n_pages)
    nsteps = n_pages // npp
    hq = MLA_HEADS * MLA_QK_PAD
    hv = MLA_HEADS * MLA_V

    def lat_spec(i):
        return pl.BlockSpec((None, None, PAGE_SIZE, MLA_KV_RANK),
                            lambda b, s, pt: (layer, pt[b, s * npp + i], 0, 0))

    def rope_spec(i):
        return pl.BlockSpec((None, None, MLA_ROPE, PAGE_SIZE),
                            lambda b, s, pt: (layer, pt[b, s * npp + i], 0, 0))

    grid_spec = pltpu.PrefetchScalarGridSpec(
        num_scalar_prefetch=1,
        grid=(bsz, nsteps),
        in_specs=[pl.BlockSpec((SAMPLE_T_PAD, hq), lambda b, s, pt: (b, 0))]
                 + [lat_spec(i) for i in range(npp)] + [rope_spec(i) for i in range(npp)]
                 + [pl.BlockSpec((None, PAGE_SIZE, MLA_KV_RANK), lambda b, s, pt: (b, 0, 0)),
                    pl.BlockSpec((None, LANES, PAGE_SIZE), lambda b, s, pt: (b, 0, 0)),
                    pl.BlockSpec((hv, MLA_KV_RANK), lambda b, s, pt: (0, 0)),
                    pl.BlockSpec((MLA_KV_RANK, hv), lambda b, s, pt: (0, 0)),
                    pl.BlockSpec((1, LANES), lambda b, s, pt: (0, 0)),
                    pl.BlockSpec((1, LANES), lambda b, s, pt: (0, 0))],
        out_specs=pl.BlockSpec((SAMPLE_T_PAD, hv), lambda b, s, pt: (b, 0)),
        scratch_shapes=[pltpu.VMEM((PAGED_ROWS, 1), F32),
                        pltpu.VMEM((PAGED_ROWS, 1), F32),
                        pltpu.VMEM((PAGED_ROWS, MLA_KV_RANK), F32),
                        pltpu.VMEM((hv + PAGED_ROWS, MLA_KV_RANK), BF16),
                        pltpu.VMEM((PAGED_ROWS, LANES), BF16)],
    )
    kern = functools.partial(_paged_kernel, t_new=t_new, npp=npp)
    return pl.pallas_call(
        kern,
        grid_spec=grid_spec,
        out_shape=jax.ShapeDtypeStruct((bsz * SAMPLE_T_PAD, hv), F32),
        compiler_params=_cparams("parallel", "arbitrary"),
    )(page_table, q, *([lat_pool] * npp), *([rope_pool_t] * npp), new_c, new_r_t,
      p["w_kb_t"], p["w_vb"], p["gk_nope"], p["gk_rope"])


def _merge_kernel(x_ref, g_ref, wg_ref, bg_ref, os_ref, or_ref, om_ref, ws_ref, wr_ref, wm_ref, wo_ref, o_ref):
    h = (_rms(x_ref[...]) * g_ref[...]).astype(BF16)
    gl = _dot(h, wg_ref[...]) + bg_ref[...]
    d = D_MODEL
    merged = (_sigmoid(gl[:, :d]) * _dot(os_ref[...], ws_ref[...])
              + _sigmoid(gl[:, d:2 * d]) * _dot(or_ref[...], wr_ref[...])
              + _sigmoid(gl[:, 2 * d:]) * _dot(om_ref[...].astype(BF16), wm_ref[...]))
    o_ref[...] = x_ref[...] + _dot(merged.astype(BF16), wo_ref[...])


def _merge(x, o_ssd, o_rwkv, o_mla, p, tm):
    n, d = x.shape
    row = pl.BlockSpec((tm, d), lambda i: (i, 0))
    return pl.pallas_call(
        _merge_kernel,
        grid=(n // tm,),
        in_specs=[row, _full((1, d)), _full((d, N_BRANCH * d)), _full((1, N_BRANCH * d)), row, row, row]
                 + [_full((d, d))] * 4,
        out_specs=row,
        out_shape=jax.ShapeDtypeStruct((n, d), F32),
        compiler_params=_cparams("parallel"),
    )(x, p["norm_mix_g"], p["w_gate"], p["b_gate"], o_ssd, o_rwkv, o_mla,
      p["w_o_ssd"], p["w_o_rwkv"], p["w_o_mla"], p["w_out"])


def _ffn_kernel(x_ref, tail0_ref, g_ref, wup_ref, cw_ref, cb_ref, wdn_ref, o_ref, tail_ref, buf,
                *, tm, t_valid, nb):
    j = pl.program_id(1)

    @pl.when(j == 0)
    def _():
        buf[:, 0:SUBLANES, :] = tail0_ref[...]

    x = x_ref[...]
    h2 = (_rms(x) * g_ref[...]).astype(BF16)
    up = _dot(h2, wup_ref[...])
    acts = []
    for i in range(nb):
        buf[i, SUBLANES:SUBLANES + tm, :] = up[i * tm:(i + 1) * tm]
        conv = cb_ref[...]
        for c in range(FFN_CONV):
            conv = conv + buf[i, pl.ds(SUBLANES - (FFN_CONV - 1) + c, tm), :] * cw_ref[c:c + 1, :]
        tail = buf[i, pl.ds(t_valid, SUBLANES), :]
        tail_ref[i] = tail
        buf[i, 0:SUBLANES, :] = tail
        acts.append(_silu(conv[:, :D_FF]) * conv[:, D_FF:])
    act = (jnp.concatenate(acts, axis=0) if nb > 1 else acts[0]).astype(BF16)
    o_ref[...] = x + _dot(act, wdn_ref[...])


def _ffn(x, tail0, p, tm, t_valid, nb):
    bsz = tail0.shape[0]
    n, d = x.shape
    nt = n // bsz // tm
    assert nb == 1 or nt == 1
    kern = functools.partial(_ffn_kernel, tm=tm, t_valid=t_valid, nb=nb)
    return pl.pallas_call(
        kern,
        grid=(bsz // nb, nt),
        in_specs=[pl.BlockSpec((nb * tm, d), lambda b, j: (b * nt + j, 0)),
                  pl.BlockSpec((nb, SUBLANES, 2 * D_FF), lambda b, j: (b, 0, 0)),
                  _full((1, d)),
                  pl.BlockSpec((d, 2 * D_FF), lambda b, j: (0, 0), pipeline_mode=pl.Buffered(1)),
                  _full((FFN_CONV, 2 * D_FF)), _full((1, 2 * D_FF)),
                  pl.BlockSpec((D_FF, d), lambda b, j: (0, 0), pipeline_mode=pl.Buffered(1))],
        out_specs=[pl.BlockSpec((nb * tm, d), lambda b, j: (b * nt + j, 0)),
                   pl.BlockSpec((nb, SUBLANES, 2 * D_FF), lambda b, j: (b, 0, 0))],
        out_shape=[jax.ShapeDtypeStruct((n, d), F32),
                   jax.ShapeDtypeStruct((bsz, SUBLANES, 2 * D_FF), F32)],
        scratch_shapes=[pltpu.VMEM((nb, SUBLANES + tm, 2 * D_FF), F32)],
        compiler_params=_cparams("parallel", "arbitrary"),
    )(x, tail0, p["norm_ffn_g"], p["ffn_w_up"], p["ffn_conv_w"], p["ffn_conv_b"], p["ffn_w_down"])


def _pad_cols(w, width):
    return jnp.pad(w, ((0, 0), (0, width - w.shape[1])))


def _row(v):
    return v.reshape(1, -1)


def _prep_layer(i, w):
    d = RWKV_DIM
    w_in = w["w_in"][i]
    o_r = SSD_IN
    o_m = SSD_IN + RWKV_IN
    p = {}
    p["norm_mix_g"] = _row(w["norm_mix_g"][i])
    p["w_ssd"] = _pad_cols(w_in[:, :o_r], SSD_IN_PAD).astype(BF16)
    p["w_rwkv"] = w_in[:, o_r:o_m].astype(BF16)
    p["w_mla"] = _pad_cols(w_in[:, o_m:], MLA_IN_PAD).astype(BF16)
    p["w_gate"] = w["w_gate"][i].astype(BF16)
    p["b_gate"] = _row(w["b_gate"][i])
    p["ssd_conv_w"] = w["ssd_conv_w"][i]
    p["ssd_conv_b"] = _row(w["ssd_conv_b"][i])
    p["ssd_dt_bias"] = _pad_cols(_row(w["ssd_dt_bias"][i]), LANES)
    p["ssd_a_log"] = _pad_cols(_row(w["ssd_a_log"][i]), LANES)
    p["ssd_d"] = _row(jnp.repeat(w["ssd_d"][i], SSD_HEAD_DIM))
    p["ssd_norm_g"] = _row(w["ssd_norm_g"][i])
    p["rwkv_shift_mu"] = _row(w["rwkv_shift_mu"][i])
    p["rwkv_w0"] = _row(w["rwkv_w0"][i])
    zeros_lora = jnp.zeros((RWKV_DECAY_LORA, d), F32)
    p["rwkv_w_up"] = jnp.concatenate([w["rwkv_w_up"][i], zeros_lora], axis=0).astype(BF16)
    p["rwkv_a0"] = _row(w["rwkv_a0"][i])
    p["rwkv_a_up"] = jnp.concatenate([zeros_lora, w["rwkv_a_up"][i]], axis=0).astype(BF16)
    p["rwkv_g_up"] = w["rwkv_g_up"][i].astype(BF16)
    p["rwkv_k_k"] = _row(w["rwkv_k_k"][i])
    p["rwkv_k_a"] = _row(w["rwkv_k_a"][i])
    p["rwkv_r_k"] = _row(w["rwkv_r_k"][i])
    p["rwkv_ln_g"] = _row(w["rwkv_ln_g"][i])
    p["rwkv_ln_b"] = _row(w["rwkv_ln_b"][i])
    wq = w["mla_w_q_b"][i]
    hv = MLA_HEADS * MLA_NOPE
    p["mla_q_a_g"] = _row(w["mla_q_a_g"][i])
    p["wq_nope"] = wq[:, :, :MLA_NOPE].reshape(MLA_Q_RANK, hv).astype(BF16)
    p["wq_rope"] = jnp.pad(wq[:, :, MLA_NOPE:], ((0, 0), (0, 0), (0, MLA_NOPE - MLA_ROPE))
                           ).reshape(MLA_Q_RANK, hv).astype(BF16)
    p["mla_kv_a_g"] = _row(w["mla_kv_a_g"][i])
    p["w_kb"] = w["mla_w_kb"][i].reshape(MLA_KV_RANK, hv).astype(BF16)
    p["w_kb_t"] = p["w_kb"].T
    p["w_vb"] = w["mla_w_vb"][i].reshape(MLA_KV_RANK, MLA_HEADS * MLA_V).astype(BF16)
    gq, gk = w["mla_q_norm_g"][i], w["mla_k_norm_g"][i]
    p["gq_nope"] = _row(gq[:MLA_NOPE])
    p["gq_rope"] = _pad_cols(_row(gq[MLA_NOPE:]), LANES)
    p["gk_nope"] = _row(gk[:MLA_NOPE])
    p["gk_rope"] = _pad_cols(_row(gk[MLA_NOPE:]), LANES)
    for name in ("w_o_ssd", "w_o_rwkv", "w_o_mla", "w_out"):
        p[name] = w[name][i].astype(BF16)
    p["norm_ffn_g"] = _row(w["norm_ffn_g"][i])
    p["ffn_w_up"] = w["ffn_w_up"][i].astype(BF16)
    p["ffn_conv_w"] = w["ffn_conv_w"][i]
    p["ffn_conv_b"] = _row(w["ffn_conv_b"][i])
    p["ffn_w_down"] = w["ffn_w_down"][i].astype(BF16)
    return p


def _constants():
    ch = np.arange(RWKV_DIM)
    seg = (ch[:, None] // RWKV_HEAD_DIM == np.arange(LANES)[None, :]).astype(np.float32)
    seg2 = np.concatenate([seg, seg], axis=0)
    seg_t2 = np.concatenate([seg.T, seg.T], axis=0)
    return jnp.asarray(seg2, BF16), jnp.asarray(seg_t2, BF16)


def _rope_tables(pos):
    half = MLA_ROPE // 2
    inv = ROPE_BASE ** (-jnp.arange(half, dtype=F32) / half)
    ang = pos.astype(F32)[:, None] * inv[None, :]
    cos, sin = jnp.cos(ang), jnp.sin(ang)
    z = jnp.zeros_like(cos)
    z2 = jnp.zeros((pos.shape[0], LANES - MLA_ROPE), F32)
    return (jnp.concatenate([cos, cos, z2], axis=1),
            jnp.concatenate([-sin, z, z2], axis=1),
            jnp.concatenate([z, sin, z2], axis=1))


def _tail_block(state, rows):
    return jnp.pad(state, ((0, 0), (SUBLANES - rows, 0), (0, 0)))


def _pack_rwkv_state(s):
    b = s.shape[0]
    s5 = s.reshape(b, RWKV_PAIRS, 2, RWKV_HEAD_DIM, RWKV_HEAD_DIM)
    z = jnp.zeros_like(s5[:, :, 0])
    top = jnp.concatenate([s5[:, :, 0], z], axis=-1)
    bot = jnp.concatenate([z, s5[:, :, 1]], axis=-1)
    return jnp.concatenate([top, bot], axis=-2)


def _unpack_rwkv_state(s):
    b = s.shape[0]
    hd = RWKV_HEAD_DIM
    return jnp.stack([s[:, :, :hd, :hd], s[:, :, hd:, hd:]], axis=2).reshape(b, RWKV_HEADS, hd, hd)


def _layer(x, st, p, cfg, attend):
    bsz, t, t_valid, tm = cfg["bsz"], cfg["t"], cfg["t_valid"], cfg["tm"]
    n = x.shape[0]
    tmn = min(tm, n)

    if t % SSD_CHUNK:
        x_pad = jnp.pad(x.reshape(bsz, t, D_MODEL), ((0, 0), (0, SSD_CHUNK - t), (0, 0)))
        o_ssd, ssd_tail, ssd_h = _ssd(x_pad.reshape(bsz * SSD_CHUNK, D_MODEL), st["ssd_tail"], st["ssd_h"],
                                      p, t_valid)
        o_ssd = o_ssd.reshape(bsz, SSD_CHUNK, SSD_INNER)[:, :t].reshape(n, SSD_INNER)
    else:
        o_ssd, ssd_tail, ssd_h = _ssd(x, st["ssd_tail"], st["ssd_h"], p, SSD_CHUNK)

    tm_r = min(tm, t)
    r, lw, k2, v, al, be, gg, rwkv_tail = _rwkv_pre(x, st["rwkv_tail"], p, tm_r, min(t_valid, tm_r))
    d = RWKV_DIM
    if t % RWKV_CHUNK:
        def chunk_pad(a_):
            a_ = a_.reshape(bsz, t, d)[:, :t_valid]
            return jnp.pad(a_, ((0, 0), (0, RWKV_CHUNK - t_valid), (0, 0))).reshape(bsz * RWKV_CHUNK, d)
        yy, rwkv_s = _rwkv_scan(*(chunk_pad(a_) for a_ in (r, lw, k2, v, al, be)), st["rwkv_s"], bsz, RWKV_CHUNK)
        yy = yy.reshape(bsz, RWKV_CHUNK, d)[:, :t].reshape(n, d)
    else:
        yy, rwkv_s = _rwkv_scan(r, lw, k2, v, al, be, st["rwkv_s"], bsz, min(t, cfg["tc_rwkv"]))
    o_rwkv = _rwkv_post(yy, r, k2, v, gg, p, tmn)

    q, k, vv, ckv, kpe = _mla_prep(x, cfg["rope"], p, tmn, cfg["q_dtype"])
    o_mla = attend(q, k, vv, ckv, kpe)

    x = _merge(x, o_ssd, o_rwkv, o_mla, p, tmn)
    tm_f = min(cfg["tm_ffn"], t)
    x, ffn_tail = _ffn(x, st["ffn_tail"], p, tm_f, min(t_valid, tm_f), cfg["nb_ffn"])
    new_st = dict(ssd_tail=ssd_tail, ssd_h=ssd_h, rwkv_tail=rwkv_tail, rwkv_s=rwkv_s, ffn_tail=ffn_tail)
    return x, new_st, ckv, kpe


def kernel(x_prompt, x_sample, cache_kv_latent, cache_k_rope, page_table, state_ssm, state_ssm_conv,
           state_rwkv, state_rwkv_shift, state_ffn_conv, norm_mix_g, w_in, ssd_conv_w, ssd_conv_b,
           ssd_dt_bias, ssd_a_log, ssd_d, ssd_norm_g, rwkv_shift_mu, rwkv_w0, rwkv_w_up, rwkv_a0,
           rwkv_a_up, rwkv_g_up, rwkv_k_k, rwkv_k_a, rwkv_r_k, rwkv_ln_g, rwkv_ln_b, mla_q_a_g,
           mla_w_q_b, mla_kv_a_g, mla_w_kb, mla_w_vb, mla_q_norm_g, mla_k_norm_g, w_gate, b_gate,
           w_o_ssd, w_o_rwkv, w_o_mla, w_out, norm_ffn_g, ffn_w_up, ffn_conv_w, ffn_conv_b, ffn_w_down):
    w = dict(norm_mix_g=norm_mix_g, w_in=w_in, ssd_conv_w=ssd_conv_w, ssd_conv_b=ssd_conv_b,
             ssd_dt_bias=ssd_dt_bias, ssd_a_log=ssd_a_log, ssd_d=ssd_d, ssd_norm_g=ssd_norm_g,
             rwkv_shift_mu=rwkv_shift_mu, rwkv_w0=rwkv_w0, rwkv_w_up=rwkv_w_up, rwkv_a0=rwkv_a0,
             rwkv_a_up=rwkv_a_up, rwkv_g_up=rwkv_g_up, rwkv_k_k=rwkv_k_k, rwkv_k_a=rwkv_k_a,
             rwkv_r_k=rwkv_r_k, rwkv_ln_g=rwkv_ln_g, rwkv_ln_b=rwkv_ln_b, mla_q_a_g=mla_q_a_g,
             mla_w_q_b=mla_w_q_b, mla_kv_a_g=mla_kv_a_g, mla_w_kb=mla_w_kb, mla_w_vb=mla_w_vb,
             mla_q_norm_g=mla_q_norm_g, mla_k_norm_g=mla_k_norm_g, w_gate=w_gate, b_gate=b_gate,
             w_o_ssd=w_o_ssd, w_o_rwkv=w_o_rwkv, w_o_mla=w_o_mla, w_out=w_out, norm_ffn_g=norm_ffn_g,
             ffn_w_up=ffn_w_up, ffn_conv_w=ffn_conv_w, ffn_conv_b=ffn_conv_b, ffn_w_down=ffn_w_down)
    depth = w_in.shape[0]
    bp, tp, d = x_prompt.shape
    bs, ts, _ = x_sample.shape
    tsp = SAMPLE_T_PAD
    past_len = page_table.shape[1] * PAGE_SIZE
    seg2, seg_t2 = _constants()
    rope_pool_t = jnp.swapaxes(cache_k_rope, 2, 3)

    tm_p = min(256, tp)
    cfg_p = dict(bsz=bp, t=tp, t_valid=tp, tm=tm_p, tm_ffn=min(256, tp), nb_ffn=1, tc_rwkv=256,
                 rope=_rope_tables(jnp.arange(tp)), q_dtype=BF16)
    pos_s = past_len + jnp.arange(tsp)
    rope_s = tuple(jnp.tile(tb, (bs, 1)) for tb in _rope_tables(pos_s))
    cfg_s = dict(bsz=bs, t=tsp, t_valid=ts, tm=bs * tsp, tm_ffn=tsp, nb_ffn=math.gcd(bs, 16), tc_rwkv=RWKV_CHUNK, rope=rope_s,
                 q_dtype=F32)

    xp = x_prompt.reshape(bp * tp, d)
    xs = jnp.pad(x_sample, ((0, 0), (0, tsp - ts), (0, 0))).reshape(bs * tsp, d)

    zero_st = dict(ssd_tail=jnp.zeros((bp, SUBLANES, SSD_CONV_DIM), F32),
                   ssd_h=jnp.zeros((bp, SSD_HEADS // 2, LANES, SSD_STATE), F32),
                   rwkv_tail=jnp.zeros((bp, SUBLANES, RWKV_IN), F32),
                   rwkv_s=jnp.zeros((bp, RWKV_PAIRS, LANES, LANES), F32),
                   ffn_tail=jnp.zeros((bp, SUBLANES, 2 * D_FF), F32))

    new_p = [[] for _ in range(7)]
    new_s = [[] for _ in range(7)]
    for i in range(depth):
        p = _prep_layer(i, w)
        p["seg2"], p["seg_t2"] = seg2, seg_t2

        def prompt_attend(q, k, v, ckv, kpe):
            return _flash(q, k, v, bp, min(512, tp))

        xp, st_p, ckv_p, kpe_p = _layer(xp, zero_st, p, cfg_p, prompt_attend)

        st_in = dict(ssd_tail=_tail_block(state_ssm_conv[i], SSD_CONV - 1),
                     ssd_h=state_ssm[i].reshape(bs, SSD_HEADS // 2, LANES, SSD_STATE),
                     rwkv_tail=_tail_block(state_rwkv_shift[i][:, None, :], 1),
                     rwkv_s=_pack_rwkv_state(state_rwkv[i]),
                     ffn_tail=_tail_block(state_ffn_conv[i], FFN_CONV - 1))

        def sample_attend(q, k, v, ckv, kpe, i=i, p=p):
            new_c = jnp.pad(ckv.reshape(bs, tsp, MLA_KV_RANK), ((0, 0), (0, PAGE_SIZE - tsp), (0, 0)))
            new_r_t = jnp.pad(jnp.swapaxes(kpe.reshape(bs, tsp, LANES), 1, 2), ((0, 0), (0, 0), (0, PAGE_SIZE - tsp)))
            return _paged_attention(page_table, q, cache_kv_latent, rope_pool_t, i, new_c, new_r_t, p, ts)

        xs, st_s, ckv_s, kpe_s = _layer(xs, st_in, p, cfg_s, sample_attend)

        for lst, st, ckv, kpe, b_, t_, tv in ((new_p, st_p, ckv_p, kpe_p, bp, tp, tp),
                                              (new_s, st_s, ckv_s, kpe_s, bs, tsp, ts)):
            lst[0].append(ckv.reshape(b_, t_, MLA_KV_RANK)[:, :tv])
            lst[1].append(kpe.reshape(b_, t_, LANES)[:, :tv, :MLA_ROPE])
            lst[2].append(st["ssd_h"].reshape(b_, SSD_HEADS, SSD_HEAD_DIM, SSD_STATE))
            lst[3].append(st["ssd_tail"][:, SUBLANES - (SSD_CONV - 1):])
            lst[4].append(_unpack_rwkv_state(st["rwkv_s"]))
            lst[5].append(st["rwkv_tail"][:, SUBLANES - 1])
            lst[6].append(st["ffn_tail"][:, SUBLANES - (FFN_CONV - 1):])

    outs_p = [jnp.stack(v_, axis=0) for v_ in new_p]
    outs_s = [jnp.stack(v_, axis=0) for v_ in new_s]
    y_p = xp.reshape(bp, tp, d)
    y_s = xs.reshape(bs, tsp, d)[:, :ts]
    return (y_p, y_s, *outs_p, *outs_s)
```

```python
import functools
import math

import numpy as np
import jax
import jax.numpy as jnp
from jax import lax
from jax.experimental import pallas as pl
from jax.experimental.pallas import tpu as pltpu

F32 = jnp.float32
BF16 = jnp.bfloat16

D_MODEL = 1024
PAGE_SIZE = 128

SSD_HEAD_DIM = 64
SSD_INNER = 1024
SSD_HEADS = 16
SSD_GROUPS = 2
SSD_STATE = 128
SSD_CONV = 4
SSD_CHUNK = 128
SSD_SHORT_CHUNK = 16
SSD_CONV_DIM = SSD_INNER + 2 * SSD_GROUPS * SSD_STATE
SSD_IN = SSD_INNER + SSD_CONV_DIM + SSD_HEADS
SSD_IN_PAD = SSD_INNER + SSD_CONV_DIM + 128

RWKV_HEAD_DIM = 64
RWKV_DIM = 1024
RWKV_HEADS = 16
RWKV_DECAY_LORA = 64
RWKV_A_LORA = 64
RWKV_GATE_LORA = 128
RWKV_IN = 3 * RWKV_DIM + RWKV_DECAY_LORA + RWKV_A_LORA + RWKV_GATE_LORA
RWKV_LN_EPS = 64e-5

MLA_HEADS = 8
MLA_NOPE = 128
MLA_ROPE = 64
MLA_QK = MLA_NOPE + MLA_ROPE
MLA_V = 128
MLA_Q_RANK = 512
MLA_KV_RANK = 256
MLA_IN = MLA_Q_RANK + MLA_KV_RANK + MLA_ROPE
MLA_IN_PAD = MLA_Q_RANK + MLA_KV_RANK + 128
MLA_QK_PAD = 256
ROPE_BASE = 10000.0
MLA_SCALE = MLA_QK ** -0.5

N_BRANCH = 3
D_FF = 2816
FFN_CONV = 3
NORM_EPS = 1e-6

SUBLANES = 8
LANES = 128
SAMPLE_T_PAD = 8
NEG_BIG = -1e30

VMEM_LIMIT = 56 * 1024 * 1024


def _cparams(*sem):
    return pltpu.CompilerParams(dimension_semantics=sem, vmem_limit_bytes=VMEM_LIMIT)


def _full(shape):
    nd = len(shape)
    return pl.BlockSpec(shape, lambda *_: (0,) * nd)


def _rms(x, eps=NORM_EPS):
    return x * lax.rsqrt(jnp.mean(x * x, axis=-1, keepdims=True) + eps)


def _sigmoid(x):
    return 1.0 / (1.0 + jnp.exp(-x))


def _silu(x):
    return x * _sigmoid(x)


def _softplus(x):
    return jnp.maximum(x, 0.0) + jnp.log(1.0 + jnp.exp(-jnp.abs(x)))


def _dot(a, b):
    return jnp.dot(a, b, preferred_element_type=F32)


def _dot_nt(a, b):
    return lax.dot_general(a, b, (((1,), (1,)), ((), ())), preferred_element_type=F32)


def _dot_tn(a, b):
    return lax.dot_general(a, b, (((0,), (0,)), ((), ())), preferred_element_type=F32)


def _split_hi_lo(x):
    hi = x.astype(BF16)
    lo = (x - hi.astype(F32)).astype(BF16)
    return jnp.concatenate([hi, lo], axis=1)


def _in_proj(x_ref, g_ref, w_ref):
    return _dot((_rms(x_ref[...]) * g_ref[...]).astype(BF16), w_ref[...])


def _resident(shape):
    nd = len(shape)
    return pl.BlockSpec(shape, lambda *_: (0,) * nd, pipeline_mode=pl.Buffered(1))


def _ssd_kernel(x_ref, g_ref, w_ref, tail0_ref, h0_ref, cw_ref, cb_ref, dtb_ref, alog_ref, dsk_ref, ng_ref,
                o_ref, tail_ref, h_ref, buf, *, t_valid, L, nb):
    c = pl.program_id(1)

    @pl.when(c == 0)
    def _():
        buf[:, 0:SUBLANES, :] = tail0_ref[...]
        h_ref[...] = h0_ref[...]

    u_all = _in_proj(x_ref, g_ref, w_ref)
    for i in range(nb):
        rows = slice(i * L, (i + 1) * L)
        o_ref[rows, :] = _ssd_chunk(u_all[rows], i, cw_ref, cb_ref, dtb_ref, alog_ref, dsk_ref, ng_ref,
                                    tail_ref, h_ref, buf, t_valid, L)


def _ssd_chunk(u, i, cw_ref, cb_ref, dtb_ref, alog_ref, dsk_ref, ng_ref, tail_ref, h_ref, buf, t_valid, L):
    z = u[:, :SSD_INNER]
    dt_raw = u[:, SSD_INNER + SSD_CONV_DIM:]
    buf[i, SUBLANES:SUBLANES + L, :] = u[:, SSD_INNER:SSD_INNER + SSD_CONV_DIM]
    acc = cb_ref[...]
    for j in range(SSD_CONV):
        acc = acc + buf[i, pl.ds(SUBLANES - (SSD_CONV - 1) + j, L), :] * cw_ref[j:j + 1, :]
    tail = buf[i, pl.ds(t_valid, SUBLANES), :]
    tail_ref[i] = tail
    buf[i, 0:SUBLANES, :] = tail
    xbc = _silu(acc)
    xs = xbc[:, :SSD_INNER]

    dt = _softplus(dt_raw + dtb_ref[...])
    if t_valid < L:
        row = lax.broadcasted_iota(jnp.int32, dt.shape, 0)
        dt = jnp.where(row < t_valid, dt, 0.0)
    a = -jnp.exp(alog_ref[...])
    da = dt * a
    ri = lax.broadcasted_iota(jnp.int32, (L, L), 0)
    ci = lax.broadcasted_iota(jnp.int32, (L, L), 1)
    causal = ri >= ci
    tri = jnp.where(causal, 1.0, 0.0).astype(F32)
    cs = jnp.dot(tri, da, preferred_element_type=F32, precision=lax.Precision.HIGHEST)
    cs_t = cs.T
    total = cs[L - 1:L, :]
    etot = jnp.exp(total)
    lane = lax.broadcasted_iota(jnp.int32, (L, LANES), 1)
    lo_half = lane < SSD_HEAD_DIM
    row_lo = lax.broadcasted_iota(jnp.int32, (LANES, LANES), 0) < SSD_HEAD_DIM

    def pair_cols(m, j0):
        return jnp.where(lo_half, m[:, j0:j0 + 1], m[:, j0 + 1:j0 + 2])

    hpg = SSD_HEADS // SSD_GROUPS
    y_parts = []
    for g in range(SSD_GROUPS):
        b_g = xbc[:, SSD_INNER + g * SSD_STATE:SSD_INNER + (g + 1) * SSD_STATE].astype(BF16)
        c0 = SSD_INNER + SSD_GROUPS * SSD_STATE + g * SSD_STATE
        c_g = xbc[:, c0:c0 + SSD_STATE].astype(BF16)
        cb = _dot_nt(c_g, b_g)
        for m in range(hpg // 2):
            q = g * (hpg // 2) + m
            j0 = 2 * q
            xs_p = xs[:, q * LANES:(q + 1) * LANES]
            xd = xs_p * pair_cols(dt, j0)
            xd_b = xd.astype(BF16)
            ys = []
            for e in range(2):
                j = j0 + e
                diff = cs[:, j:j + 1] - cs_t[j:j + 1, :]
                lm = jnp.where(causal, jnp.exp(jnp.where(causal, diff, 0.0)), 0.0)
                ys.append(_dot((cb * lm).astype(BF16), xd_b))
            y_diag = jnp.where(lo_half, ys[0], ys[1])
            hp = h_ref[i, q]
            cs_p = pair_cols(cs, j0)
            y_off = _dot_nt(c_g, hp.astype(BF16)) * jnp.exp(cs_p)
            tot_p = jnp.where(lo_half[0:1, :], total[:, j0:j0 + 1], total[:, j0 + 1:j0 + 2])
            dte = jnp.exp(tot_p - cs_p)
            contrib = _dot_tn((xd * dte).astype(BF16), b_g)
            decay = jnp.where(row_lo, etot[:, j0:j0 + 1], etot[:, j0 + 1:j0 + 2])
            h_ref[i, q] = hp * decay + contrib
            y_parts.append(y_diag + y_off + dsk_ref[:, q * LANES:(q + 1) * LANES] * xs_p)
    y = jnp.concatenate(y_parts, axis=1) * _silu(z)
    gw = SSD_INNER // SSD_GROUPS
    outs = [_rms(y[:, g * gw:(g + 1) * gw]) * ng_ref[:, g * gw:(g + 1) * gw] for g in range(SSD_GROUPS)]
    return jnp.concatenate(outs, axis=1).astype(BF16)


def _ssd(x, tail0, h0, p, t_valid, L, nb):
    bsz = h0.shape[0]
    n, d = x.shape
    nch = n // bsz // L
    assert nb == 1 or nch == 1
    kern = functools.partial(_ssd_kernel, t_valid=t_valid, L=L, nb=nb)
    return pl.pallas_call(
        kern,
        grid=(bsz // nb, nch),
        in_specs=[pl.BlockSpec((nb * L, d), lambda b, c: (b * nch + c, 0)), _full((1, d)),
                  _resident((d, SSD_IN_PAD)),
                  pl.BlockSpec((nb, SUBLANES, SSD_CONV_DIM), lambda b, c: (b, 0, 0)),
                  pl.BlockSpec((nb, SSD_HEADS // 2, LANES, SSD_STATE), lambda b, c: (b, 0, 0, 0)),
                  _full((SSD_CONV, SSD_CONV_DIM)), _full((1, SSD_CONV_DIM)),
                  _full((1, LANES)), _full((1, LANES)), _full((1, SSD_INNER)), _full((1, SSD_INNER))],
        out_specs=[pl.BlockSpec((nb * L, SSD_INNER), lambda b, c: (b * nch + c, 0)),
                   pl.BlockSpec((nb, SUBLANES, SSD_CONV_DIM), lambda b, c: (b, 0, 0)),
                   pl.BlockSpec((nb, SSD_HEADS // 2, LANES, SSD_STATE), lambda b, c: (b, 0, 0, 0))],
        out_shape=[jax.ShapeDtypeStruct((n, SSD_INNER), BF16),
                   jax.ShapeDtypeStruct((bsz, SUBLANES, SSD_CONV_DIM), F32),
                   jax.ShapeDtypeStruct((bsz, SSD_HEADS // 2, LANES, SSD_STATE), F32)],
        scratch_shapes=[pltpu.VMEM((nb, SUBLANES + L, SSD_CONV_DIM), F32)],
        compiler_params=_cparams("parallel", "arbitrary"),
    )(x, p["norm_mix_g"], p["w_ssd"], tail0, h0, p["ssd_conv_w"], p["ssd_conv_b"], p["ssd_dt_bias"],
      p["ssd_a_log"], p["ssd_d"], p["ssd_norm_g"])


def _seg_sum_bcast(x, seg2, seg_t2):
    s16 = _dot(_split_hi_lo(x), seg2)
    return _dot(_split_hi_lo(s16), seg_t2)


def _rwkv_pre_kernel(x_ref, g_in_ref, w_in_ref, tail0_ref, mu_ref, w0_ref, wup_ref, a0_ref, aup_ref, gup_ref, kk_ref,
                     ka_ref, seg2_ref, segt2_ref,
                     r_ref, w_ref, k_ref, v_ref, al_ref, be_ref, g_ref, tail_ref, buf, *, tm, t_valid, nb):
    j = pl.program_id(1)

    @pl.when(j == 0)
    def _():
        buf[:, 0:SUBLANES, :] = tail0_ref[...]

    u = _in_proj(x_ref, g_in_ref, w_in_ref)
    prevs = []
    for i in range(nb):
        buf[i, SUBLANES:SUBLANES + tm, :] = u[i * tm:(i + 1) * tm]
        prevs.append(buf[i, pl.ds(SUBLANES - 1, tm), :])
        tail = buf[i, pl.ds(t_valid, SUBLANES), :]
        tail_ref[i] = tail
        buf[i, 0:SUBLANES, :] = tail
    prev = jnp.concatenate(prevs, axis=0) if nb > 1 else prevs[0]

    d = RWKV_DIM
    f = u + mu_ref[...] * (prev - u)
    r, k, v = f[:, :d], f[:, d:2 * d], f[:, 2 * d:3 * d]
    lo = f[:, 3 * d:3 * d + LANES]
    glo = f[:, 3 * d + LANES:]
    ww = w0_ref[...] + _dot(jnp.tanh(lo).astype(BF16), wup_ref[...])
    w_log = -_softplus(-ww) - 0.5
    a = _sigmoid(a0_ref[...] + _dot(lo.astype(BF16), aup_ref[...]))
    kk = k * kk_ref[...]
    ssb = _seg_sum_bcast(kk * kk, seg2_ref[...], segt2_ref[...])
    kkn = kk / jnp.maximum(jnp.sqrt(ssb), 1e-12)
    r_ref[...] = r
    w_ref[...] = -jnp.exp(w_log)
    k_ref[...] = k * (1.0 + (a - 1.0) * ka_ref[...])
    v_ref[...] = v
    al_ref[...] = -kkn
    be_ref[...] = kkn * a
    g_ref[...] = _dot(_sigmoid(glo).astype(BF16), gup_ref[...])


def _rwkv_pre(x, tail0, p, tm, t_valid, nb):
    bsz = tail0.shape[0]
    n = x.shape[0]
    nt = n // bsz // tm
    assert nb == 1 or nt == 1
    d = RWKV_DIM
    kern = functools.partial(_rwkv_pre_kernel, tm=tm, t_valid=t_valid, nb=nb)
    row = pl.BlockSpec((nb * tm, d), lambda b, j: (b * nt + j, 0))
    return pl.pallas_call(
        kern,
        grid=(bsz // nb, nt),
        in_specs=[row, _full((1, D_MODEL)), _resident((D_MODEL, RWKV_IN)),
                  pl.BlockSpec((nb, SUBLANES, RWKV_IN), lambda b, j: (b, 0, 0)),
                  _full((1, RWKV_IN)), _full((1, d)), _full((LANES, d)), _full((1, d)),
                  _full((LANES, d)), _full((LANES, d)), _full((1, d)), _full((1, d)),
                  _full((2 * d, LANES)), _full((2 * LANES, d))],
        out_specs=[row] * 7 + [pl.BlockSpec((nb, SUBLANES, RWKV_IN), lambda b, j: (b, 0, 0))],
        out_shape=[jax.ShapeDtypeStruct((n, d), F32)] * 7
                  + [jax.ShapeDtypeStruct((bsz, SUBLANES, RWKV_IN), F32)],
        scratch_shapes=[pltpu.VMEM((nb, SUBLANES + tm, RWKV_IN), F32)],
        compiler_params=_cparams("parallel", "arbitrary"),
    )(x, p["norm_mix_g"], p["w_rwkv"], tail0, p["rwkv_shift_mu"], p["rwkv_w0"], p["rwkv_w_up"], p["rwkv_a0"],
      p["rwkv_a_up"],
      p["rwkv_g_up"], p["rwkv_k_k"], p["rwkv_k_a"], p["seg2"], p["seg_t2"])


RWKV_CHUNK = 64
RWKV_SOLVE_BLOCK = 16
RWKV_PAIRS = RWKV_HEADS // 2


def _hi_lo(x):
    hi = x.astype(BF16)
    return hi, (x - hi.astype(F32)).astype(BF16)


def _mm(a, b):
    return _dot(a.astype(BF16), b.astype(BF16))


def _mm2(a, b):
    ah, al = _hi_lo(a)
    m = a.shape[0]
    both = _dot(jnp.concatenate([ah, al], axis=0), b.astype(BF16))
    return both[:m] + both[m:]


def _rwkv_chunk_kernel(r_ref, lw_ref, k_ref, v_ref, al_ref, be_ref, s0_ref, y_ref, s_ref, *, nchunks):
    c = pl.program_id(1)

    @pl.when(c == 0)
    def _():
        s_ref[...] = s0_ref[...]

    C, HD = RWKV_CHUNK, RWKV_HEAD_DIM
    r2 = lax.broadcasted_iota(jnp.int32, (LANES, LANES), 0)
    c2 = lax.broadcasted_iota(jnp.int32, (LANES, LANES), 1)
    eye = jnp.where(r2 == c2, 1.0, 0.0).astype(F32)
    diag_blk = (r2 // RWKV_SOLVE_BLOCK) == (c2 // RWKV_SOLVE_BLOCK)
    same_head = (r2 // HD) == (c2 // HD)
    tri = jnp.where(lax.broadcasted_iota(jnp.int32, (C, C), 0) >= lax.broadcasted_iota(jnp.int32, (C, C), 1),
                    1.0, 0.0).astype(F32)
    t_i = lax.broadcasted_iota(jnp.int32, (C, LANES), 0)
    lane = lax.broadcasted_iota(jnp.int32, (C, LANES), 1)
    lo_half = lane < HD
    hi_half = lane >= HD
    strict = (lane % HD) < t_i
    incl = (lane % HD) <= t_i
    zeros = jnp.zeros((C, LANES), F32)

    def body(ci, carry):
        rows = pl.ds(pl.multiple_of(ci * C, C), C)
        cum_all = jnp.dot(tri, lw_ref[rows, :], preferred_element_type=F32, precision=lax.Precision.HIGHEST)
        pairs = range(RWKV_PAIRS)
        sls = [slice(p * LANES, (p + 1) * LANES) for p in pairs]
        cum = [cum_all[:, sl] for sl in sls]
        be = [be_ref[rows, sl] for sl in sls]
        kk = [k_ref[rows, sl] for sl in sls]
        vv = [v_ref[rows, sl] for sl in sls]
        g_inv = [jnp.exp(-cum[p]) for p in pairs]
        ar = [jnp.concatenate([al_ref[rows, sls[p]] * jnp.exp(cum[p] - lw_ref[rows, sls[p]]),
                               r_ref[rows, sls[p]] * jnp.exp(cum[p])], axis=0) for p in pairs]
        ar_b = [ar[p].astype(BF16) for p in pairs]
        bt = [be[p] * g_inv[p] for p in pairs]
        kt = [kk[p] * g_inv[p] for p in pairs]
        x01 = [_dot_nt(ar_b[p], jnp.concatenate(
            [jnp.where(lo_half, bt[p], 0.0), jnp.where(lo_half, kt[p], 0.0),
             jnp.where(hi_half, kt[p], 0.0), jnp.where(hi_half, bt[p], 0.0)], axis=0).astype(BF16)) for p in pairs]
        x0 = [x01[p][:, :LANES] for p in pairs]
        x1 = [x01[p][:, LANES:] for p in pairs]
        nb = [jnp.concatenate([jnp.where(lo_half & strict, x0[p][:C], 0.0),
                               jnp.where(hi_half & strict, x1[p][:C], 0.0)], axis=0) for p in pairs]
        sbd = [s_ref[p] for p in pairs]

        def read_state(p):
            s_hi, s_lo = _hi_lo(sbd[p])
            return _dot_nt(jnp.concatenate([ar_b[p], ar_b[p]], axis=1),
                           jnp.concatenate([s_hi, s_lo], axis=1))

        ars = [read_state(p) for p in pairs]

        def strict_ak(p):
            lhs = jnp.concatenate([jnp.where(hi_half & strict, x0[p][:C], 0.0),
                                   jnp.where(lo_half & strict, x1[p][:C], 0.0)], axis=0)
            rhs_w = jnp.concatenate([jnp.concatenate([zeros, vv[p]], axis=0),
                                     jnp.concatenate([vv[p], zeros], axis=0)], axis=1)
            both = _mm(lhs, rhs_w)
            return jnp.where(lo_half, both[:C, :LANES], both[C:, LANES:])

        rhs = [ars[p][:C] + strict_ak(p) for p in pairs]
        rhs2 = [jnp.concatenate([rhs[p], rhs[p]], axis=0) for p in pairs]
        nd = [jnp.where(diag_blk, nb[p], 0.0) for p in pairs]
        loff = [nb[p] - nd[p] for p in pairs]
        pm = [eye + nd[p] for p in pairs]
        wide = lambda a_, b_: jnp.concatenate([a_, b_], axis=1)
        n2 = [_mm2(nd[p], nd[p]) for p in pairs]
        t2 = [_mm2(n2[p], wide(pm[p], n2[p])) for p in pairs]
        pm = [pm[p] + t2[p][:, :LANES] for p in pairs]
        t4 = [_mm(t2[p][:, LANES:], wide(pm[p], t2[p][:, LANES:])) for p in pairs]
        pm = [pm[p] + t4[p][:, :LANES] for p in pairs]
        td = [pm[p] + _mm(t4[p][:, LANES:], pm[p]) for p in pairs]
        t1 = [_mm2(td[p], wide(loff[p], rhs2[p])) for p in pairs]
        m1 = [t1[p][:, :LANES] for p in pairs]
        x_a = [t1[p][:, LANES:] for p in pairs]
        m2 = [_mm(m1[p], m1[p]) for p in pairs]
        x_b = [x_a[p] + _mm(m2[p], x_a[p]) for p in pairs]
        u2 = [x_b[p] + _mm(m1[p], x_b[p]) for p in pairs]
        up = [jnp.where(lo_half, u2[p][:C], u2[p][C:]) for p in pairs]
        for p in pairs:
            uv = jnp.concatenate([up[p], vv[p]], axis=0)
            vu = jnp.concatenate([vv[p], up[p]], axis=0)
            both = _mm(jnp.concatenate([jnp.where(incl, x0[p][C:], 0.0), jnp.where(incl, x1[p][C:], 0.0)], axis=0),
                       jnp.concatenate([uv, vu], axis=1))
            y_ref[rows, sls[p]] = ars[p][C:] + jnp.where(lo_half, both[:C, :LANES], both[C:, LANES:])
            cum_c = cum[p][C - 1:C, :]
            g_end = jnp.exp(cum_c - cum[p])
            bkh = jnp.concatenate([be[p] * g_end, kk[p] * g_end], axis=0)
            upd = _dot_tn(uv.astype(BF16), bkh.astype(BF16))
            s_ref[p] = sbd[p] * jnp.exp(cum_c) + jnp.where(same_head, upd, 0.0)
        return carry

    lax.fori_loop(0, nchunks, body, 0)


def _rwkv_scan(r, lw, k, v, al, be, s0, bsz, tc):
    n, d = r.shape
    t_total = n // bsz
    nt = t_total // tc
    kern = functools.partial(_rwkv_chunk_kernel, nchunks=tc // RWKV_CHUNK)
    row = pl.BlockSpec((tc, d), lambda b, c: (b * nt + c, 0))
    st = pl.BlockSpec((None, RWKV_PAIRS, LANES, LANES), lambda b, c: (b, 0, 0, 0))
    return pl.pallas_call(
        kern,
        grid=(bsz, nt),
        in_specs=[row] * 6 + [st],
        out_specs=[row, st],
        out_shape=[jax.ShapeDtypeStruct((n, d), F32),
                   jax.ShapeDtypeStruct((bsz, RWKV_PAIRS, LANES, LANES), F32)],
        compiler_params=_cparams("parallel", "arbitrary"),
    )(r, lw, k, v, al, be, s0)


def _rwkv_post_kernel(y_ref, r_ref, k_ref, v_ref, g_ref, lng_ref, lnb_ref, rk_ref, seg2_ref, segt2_ref, o_ref):
    seg2, seg_t2 = seg2_ref[...], segt2_ref[...]
    y = y_ref[...]
    inv = 1.0 / RWKV_HEAD_DIM
    dlt = y - _seg_sum_bcast(y, seg2, seg_t2) * inv
    var = _seg_sum_bcast(dlt * dlt, seg2, seg_t2) * inv
    yn = dlt * lax.rsqrt(var + RWKV_LN_EPS) * lng_ref[...] + lnb_ref[...]
    bonus = _seg_sum_bcast(r_ref[...] * k_ref[...] * rk_ref[...], seg2, seg_t2) * v_ref[...]
    o_ref[...] = ((yn + bonus) * g_ref[...]).astype(BF16)


def _rwkv_post(y, r, k, v, g, p, tm):
    n, d = y.shape
    row = pl.BlockSpec((tm, d), lambda i: (i, 0))
    return pl.pallas_call(
        _rwkv_post_kernel,
        grid=(n // tm,),
        in_specs=[row] * 5 + [_full((1, d))] * 3 + [_full((2 * d, LANES)), _full((2 * LANES, d))],
        out_specs=row,
        out_shape=jax.ShapeDtypeStruct((n, d), BF16),
        compiler_params=_cparams("parallel"),
    )(y, r, k, v, g, p["rwkv_ln_g"], p["rwkv_ln_b"], p["rwkv_r_k"], p["seg2"], p["seg_t2"])


def _mla_prep_kernel(x_ref, g_in_ref, w_in_ref, cos_ref, s1_ref, s2_ref, qag_ref, wqn_ref, wqr_ref, kvg_ref,
                     wkb_ref, wvb_ref,
                     gqn_ref, gqr_ref, gkn_ref, gkr_ref, q_ref, k_ref, v_ref, ckv_ref, kpe_ref):
    u = _in_proj(x_ref, g_in_ref, w_in_ref)
    cq = u[:, :MLA_Q_RANK]
    ckv_raw = u[:, MLA_Q_RANK:MLA_Q_RANK + MLA_KV_RANK]
    kr_raw = u[:, MLA_Q_RANK + MLA_KV_RANK:]
    cos, s1, s2 = cos_ref[...], s1_ref[...], s2_ref[...]

    def rope(blk):
        return (blk * cos + pltpu.roll(blk, LANES - MLA_ROPE // 2, 1) * s1
                + pltpu.roll(blk, MLA_ROPE // 2, 1) * s2)

    cqn = (_rms(cq) * qag_ref[...]).astype(BF16)
    qn = _dot(cqn, wqn_ref[...])
    qr = _dot(cqn, wqr_ref[...])
    ckv = _rms(ckv_raw) * kvg_ref[...]
    ckv_ref[...] = ckv
    kpe = rope(kr_raw)
    kpe_ref[...] = kpe
    ckb = ckv.astype(BF16)
    kn = _dot(ckb, wkb_ref[...])
    v_ref[...] = _dot(ckb, wvb_ref[...]).astype(BF16)
    ss_kpe = jnp.sum(kpe * kpe, axis=-1, keepdims=True)
    gqn, gqr, gkn, gkr = gqn_ref[...], gqr_ref[...], gkn_ref[...], gkr_ref[...]
    for h in range(MLA_HEADS):
        sl = slice(h * MLA_NOPE, (h + 1) * MLA_NOPE)
        o0 = h * MLA_QK_PAD
        qn_h = qn[:, sl]
        qr_h = rope(qr[:, sl])
        ss = jnp.sum(qn_h * qn_h, axis=-1, keepdims=True) + jnp.sum(qr_h * qr_h, axis=-1, keepdims=True)
        rn = lax.rsqrt(ss * (1.0 / MLA_QK) + NORM_EPS) * MLA_SCALE
        q_ref[:, o0:o0 + MLA_NOPE] = (qn_h * rn * gqn).astype(q_ref.dtype)
        q_ref[:, o0 + MLA_NOPE:o0 + MLA_QK_PAD] = (qr_h * rn * gqr).astype(q_ref.dtype)
        kn_h = kn[:, sl]
        ssk = jnp.sum(kn_h * kn_h, axis=-1, keepdims=True) + ss_kpe
        rnk = lax.rsqrt(ssk * (1.0 / MLA_QK) + NORM_EPS)
        k_ref[:, o0:o0 + MLA_NOPE] = (kn_h * rnk * gkn).astype(BF16)
        k_ref[:, o0 + MLA_NOPE:o0 + MLA_QK_PAD] = (kpe * rnk * gkr).astype(BF16)


def _mla_prep(x, rope_tabs, p, tm, q_dtype):
    n, d = x.shape
    ttab = rope_tabs[0].shape[0]
    ntab = ttab // tm
    hq = MLA_HEADS * MLA_QK_PAD
    hv = MLA_HEADS * MLA_V
    tab = pl.BlockSpec((tm, LANES), lambda i: (i % ntab, 0))
    return pl.pallas_call(
        _mla_prep_kernel,
        grid=(n // tm,),
        in_specs=[pl.BlockSpec((tm, d), lambda i: (i, 0)), _full((1, d)), _full((d, MLA_IN_PAD)), tab, tab, tab,
                  _full((1, MLA_Q_RANK)), _full((MLA_Q_RANK, hv)), _full((MLA_Q_RANK, hv)),
                  _full((1, MLA_KV_RANK)), _full((MLA_KV_RANK, hv)), _full((MLA_KV_RANK, hv)),
                  _full((1, LANES)), _full((1, LANES)), _full((1, LANES)), _full((1, LANES))],
        out_specs=[pl.BlockSpec((tm, hq), lambda i: (i, 0)), pl.BlockSpec((tm, hq), lambda i: (i, 0)),
                   pl.BlockSpec((tm, hv), lambda i: (i, 0)),
                   pl.BlockSpec((tm, MLA_KV_RANK), lambda i: (i, 0)),
                   pl.BlockSpec((tm, LANES), lambda i: (i, 0))],
        out_shape=[jax.ShapeDtypeStruct((n, hq), q_dtype), jax.ShapeDtypeStruct((n, hq), BF16),
                   jax.ShapeDtypeStruct((n, hv), BF16), jax.ShapeDtypeStruct((n, MLA_KV_RANK), F32),
                   jax.ShapeDtypeStruct((n, LANES), F32)],
        compiler_params=_cparams("parallel"),
    )(x, p["norm_mix_g"], p["w_mla"], *rope_tabs, p["mla_q_a_g"], p["wq_nope"], p["wq_rope"], p["mla_kv_a_g"], p["w_kb"], p["w_vb"],
      p["gq_nope"], p["gq_rope"], p["gk_nope"], p["gk_rope"])


FLASH_HEAD_GROUP = 2


def _flash_kernel(q_ref, k_ref, v_ref, o_ref, m_sc, acc_sc, *, tq):
    qi = pl.program_id(1)
    ki = pl.program_id(2)
    nblk = tq // LANES

    @pl.when(ki == 0)
    def _():
        m_sc[...] = jnp.full_like(m_sc, NEG_BIG)
        acc_sc[...] = jnp.zeros_like(acc_sc)

    ones = jnp.ones((tq, LANES), BF16)

    def update(diagonal):
        for h0 in range(0, MLA_HEADS, FLASH_HEAD_GROUP):
            heads = range(h0, h0 + FLASH_HEAD_GROUP)
            s = {h: _dot_nt(q_ref[:, h * MLA_QK_PAD:(h + 1) * MLA_QK_PAD],
                            k_ref[:, h * MLA_QK_PAD:(h + 1) * MLA_QK_PAD]) for h in heads}
            if diagonal:
                row = lax.broadcasted_iota(jnp.int32, (tq, tq), 0)
                col = lax.broadcasted_iota(jnp.int32, (tq, tq), 1)
                s = {h: jnp.where(col <= row, s[h], NEG_BIG) for h in heads}
            m_prev = {h: m_sc[h] for h in heads}
            m_new = {h: jnp.maximum(m_prev[h], jnp.max(s[h], axis=-1, keepdims=True)) for h in heads}
            pr = {h: jnp.concatenate([jnp.exp(s[h][:, j * LANES:(j + 1) * LANES] - m_new[h]) for j in range(nblk)],
                                     axis=1).astype(BF16) for h in heads}
            corr = {h: jnp.exp(m_prev[h] - m_new[h]) for h in heads}
            for h in heads:
                v_ext = jnp.concatenate([v_ref[:, h * MLA_V:(h + 1) * MLA_V], ones], axis=1)
                acc_sc[h] = acc_sc[h] * jnp.concatenate([corr[h], corr[h]], axis=1) + _dot(pr[h], v_ext)
                m_sc[h] = m_new[h]

    @pl.when(ki < qi)
    def _():
        update(False)

    @pl.when(ki == qi)
    def _():
        update(True)
        for h in range(MLA_HEADS):
            acc = acc_sc[h]
            o_ref[:, h * MLA_V:(h + 1) * MLA_V] = (acc[:, :MLA_V] / acc[:, MLA_V:]).astype(BF16)


def _flash(q, k, v, bsz, tq):
    n = q.shape[0]
    nq = n // bsz // tq
    hq = MLA_HEADS * MLA_QK_PAD
    hv = MLA_HEADS * MLA_V
    kern = functools.partial(_flash_kernel, tq=tq)
    return pl.pallas_call(
        kern,
        grid=(bsz, nq, nq),
        in_specs=[pl.BlockSpec((tq, hq), lambda b, i, j: (b * nq + i, 0)),
                  pl.BlockSpec((tq, hq), lambda b, i, j: (b * nq + jnp.minimum(i, j), 0)),
                  pl.BlockSpec((tq, hv), lambda b, i, j: (b * nq + jnp.minimum(i, j), 0))],
        out_specs=pl.BlockSpec((tq, hv), lambda b, i, j: (b * nq + i, 0)),
        out_shape=jax.ShapeDtypeStruct((n, hv), BF16),
        scratch_shapes=[pltpu.VMEM((MLA_HEADS, tq, LANES), F32),
                        pltpu.VMEM((MLA_HEADS, tq, MLA_V + LANES), F32)],
        compiler_params=_cparams("parallel", "parallel", "arbitrary"),
    )(q, k, v)


PAGES_PER_STEP = 32
PAGED_ROWS = MLA_HEADS * SAMPLE_T_PAD


def _paged_kernel(pt_ref, q_ref, *refs, t_new, npp):
    lat_refs = refs[:npp]
    rope_refs = refs[npp:2 * npp]
    (newc_ref, newr_ref, wkbt_ref, wvb_ref, gkn_ref, gkr_ref, o_ref,
     m_sc, l_sc, acc_sc, wq_sc, qr_sc) = refs[2 * npp:]
    step = pl.program_id(1)
    nkn = MLA_HEADS * MLA_NOPE

    @pl.when(step == 0)
    def _():
        m_sc[...] = jnp.full_like(m_sc, NEG_BIG)
        l_sc[...] = jnp.zeros_like(l_sc)
        acc_sc[...] = jnp.zeros_like(acc_sc)
        q = q_ref[...]
        gkn, gkr = gkn_ref[...], gkr_ref[...]
        wkbt = wkbt_ref[...]
        wq_sc[0:nkn, :] = wkbt
        for h in range(MLA_HEADS):
            o0 = h * MLA_QK_PAD
            rows = slice(h * SAMPLE_T_PAD, (h + 1) * SAMPLE_T_PAD)
            qn = (q[:, o0:o0 + MLA_NOPE] * gkn).astype(BF16)
            wq_sc[nkn + h * SAMPLE_T_PAD:nkn + (h + 1) * SAMPLE_T_PAD, :] = _dot(
                qn, wkbt[h * MLA_NOPE:(h + 1) * MLA_NOPE, :]).astype(BF16)
            qr_sc[rows, :] = (q[:, o0 + MLA_NOPE:o0 + MLA_QK_PAD] * gkr).astype(BF16)

    wq = wq_sc[...]
    qr = qr_sc[...]

    def scores(cbs, krs):
        n = len(cbs)
        big = [_dot_nt(wq, cbs[i]) for i in range(n)]
        rope = [_dot(qr, krs[i].astype(BF16)) for i in range(n)]
        ssr = [jnp.sum(krs[i] * krs[i], axis=0, keepdims=True) for i in range(n)]
        out = []
        for i in range(n):
            rn = []
            for h in range(MLA_HEADS):
                kn_h = big[i][h * MLA_NOPE:(h + 1) * MLA_NOPE]
                ss = jnp.sum(kn_h * kn_h, axis=0, keepdims=True) + ssr[i]
                rn.append(jnp.broadcast_to(lax.rsqrt(ss * (1.0 / MLA_QK) + NORM_EPS),
                                           (SAMPLE_T_PAD, ss.shape[1])))
            out.append((big[i][nkn:] + rope[i]) * jnp.concatenate(rn, axis=0))
        return out

    def softmax_update(s_list, cb_list):
        m_prev = m_sc[...]
        m_new = m_prev
        for s in s_list:
            m_new = jnp.maximum(m_new, jnp.max(s, axis=-1, keepdims=True))
        corr = jnp.exp(m_prev - m_new)
        l_new = l_sc[...] * corr
        acc = acc_sc[...] * corr
        for s, cb in zip(s_list, cb_list):
            pr = jnp.exp(s - m_new)
            l_new = l_new + jnp.sum(pr, axis=-1, keepdims=True)
            acc = acc + _dot(pr.astype(BF16), cb)
        l_sc[...] = l_new
        acc_sc[...] = acc
        m_sc[...] = m_new

    grp = 2 if npp % 2 == 0 else 1
    rope_pad = jnp.zeros((LANES - MLA_ROPE, grp * PAGE_SIZE), F32)
    cbs, krs = [], []
    for i in range(0, npp, grp):
        cbs.append(jnp.concatenate([lat_refs[i + j][...] for j in range(grp)], axis=0).astype(BF16))
        krs.append(jnp.concatenate([jnp.concatenate([rope_refs[i + j][...] for j in range(grp)], axis=1),
                                    rope_pad], axis=0))
    softmax_update(scores(cbs, krs), cbs)

    @pl.when(step == pl.num_programs(1) - 1)
    def _():
        cb = newc_ref[...].astype(BF16)
        s = scores([cb], [newr_ref[...]])[0]
        key = lax.broadcasted_iota(jnp.int32, s.shape, 1)
        qry = lax.broadcasted_iota(jnp.int32, s.shape, 0) % SAMPLE_T_PAD
        softmax_update([jnp.where((key <= qry) & (key < t_new), s, NEG_BIG)], [cb])
        o_lat = (acc_sc[...] / l_sc[...]).astype(BF16)
        wvb = wvb_ref[...]
        for h in range(MLA_HEADS):
            o_ref[:, h * MLA_V:(h + 1) * MLA_V] = _dot(
                o_lat[h * SAMPLE_T_PAD:(h + 1) * SAMPLE_T_PAD],
                wvb[:, h * MLA_V:(h + 1) * MLA_V])


def _paged_attention(page_table, q, lat_pool, rope_pool_t, layer, new_c, new_r_t, p, t_new):
    bsz, n_pages = page_table.shape
    npp = math.gcd(PAGES_PER_STEP, n_pages)
    nsteps = n_pages // npp
    hq = MLA_HEADS * MLA_QK_PAD
    hv = MLA_HEADS * MLA_V

    def lat_spec(i):
        return pl.BlockSpec((None, None, PAGE_SIZE, MLA_KV_RANK),
                            lambda b, s, pt: (layer, pt[b, s * npp + i], 0, 0))

    def rope_spec(i):
        return pl.BlockSpec((None, None, MLA_ROPE, PAGE_SIZE),
                            lambda b, s, pt: (layer, pt[b, s * npp + i], 0, 0))

    grid_spec = pltpu.PrefetchScalarGridSpec(
        num_scalar_prefetch=1,
        grid=(bsz, nsteps),
        in_specs=[pl.BlockSpec((SAMPLE_T_PAD, hq), lambda b, s, pt: (b, 0))]
                 + [lat_spec(i) for i in range(npp)] + [rope_spec(i) for i in range(npp)]
                 + [pl.BlockSpec((None, PAGE_SIZE, MLA_KV_RANK), lambda b, s, pt: (b, 0, 0)),
                    pl.BlockSpec((None, LANES, PAGE_SIZE), lambda b, s, pt: (b, 0, 0)),
                    pl.BlockSpec((hv, MLA_KV_RANK), lambda b, s, pt: (0, 0)),
                    pl.BlockSpec((MLA_KV_RANK, hv), lambda b, s, pt: (0, 0)),
                    pl.BlockSpec((1, LANES), lambda b, s, pt: (0, 0)),
                    pl.BlockSpec((1, LANES), lambda b, s, pt: (0, 0))],
        out_specs=pl.BlockSpec((SAMPLE_T_PAD, hv), lambda b, s, pt: (b, 0)),
        scratch_shapes=[pltpu.VMEM((PAGED_ROWS, 1), F32),
                        pltpu.VMEM((PAGED_ROWS, 1), F32),
                        pltpu.VMEM((PAGED_ROWS, MLA_KV_RANK), F32),
                        pltpu.VMEM((hv + PAGED_ROWS, MLA_KV_RANK), BF16),
                        pltpu.VMEM((PAGED_ROWS, LANES), BF16)],
    )
    kern = functools.partial(_paged_kernel, t_new=t_new, npp=npp)
    return pl.pallas_call(
        kern,
        grid_spec=grid_spec,
        out_shape=jax.ShapeDtypeStruct((bsz * SAMPLE_T_PAD, hv), F32),
        compiler_params=_cparams("parallel", "arbitrary"),
    )(page_table, q, *([lat_pool] * npp), *([rope_pool_t] * npp), new_c, new_r_t,
      p["w_kb_t"], p["w_vb"], p["gk_nope"], p["gk_rope"])


def _merge_kernel(x_ref, g_ref, wg_ref, bg_ref, os_ref, or_ref, om_ref, ws_ref, wr_ref, wm_ref, wo_ref, o_ref):
    h = (_rms(x_ref[...]) * g_ref[...]).astype(BF16)
    gl = _dot(h, wg_ref[...]) + bg_ref[...]
    d = D_MODEL
    merged = (_sigmoid(gl[:, :d]) * _dot(os_ref[...], ws_ref[...])
              + _sigmoid(gl[:, d:2 * d]) * _dot(or_ref[...], wr_ref[...])
              + _sigmoid(gl[:, 2 * d:]) * _dot(om_ref[...].astype(BF16), wm_ref[...]))
    o_ref[...] = x_ref[...] + _dot(merged.astype(BF16), wo_ref[...])


def _merge(x, o_ssd, o_rwkv, o_mla, p, tm):
    n, d = x.shape
    row = pl.BlockSpec((tm, d), lambda i: (i, 0))
    return pl.pallas_call(
        _merge_kernel,
        grid=(n // tm,),
        in_specs=[row, _full((1, d)), _full((d, N_BRANCH * d)), _full((1, N_BRANCH * d)), row, row, row]
                 + [_full((d, d))] * 4,
        out_specs=row,
        out_shape=jax.ShapeDtypeStruct((n, d), F32),
        compiler_params=_cparams("parallel"),
    )(x, p["norm_mix_g"], p["w_gate"], p["b_gate"], o_ssd, o_rwkv, o_mla,
      p["w_o_ssd"], p["w_o_rwkv"], p["w_o_mla"], p["w_out"])


def _ffn_kernel(x_ref, tail0_ref, g_ref, wup_ref, cw_ref, cb_ref, wdn_ref, o_ref, tail_ref, buf,
                *, tm, t_valid, nb):
    j = pl.program_id(1)

    @pl.when(j == 0)
    def _():
        buf[:, 0:SUBLANES, :] = tail0_ref[...]

    x = x_ref[...]
    h2 = (_rms(x) * g_ref[...]).astype(BF16)
    up = _dot(h2, wup_ref[...])
    acts = []
    for i in range(nb):
        buf[i, SUBLANES:SUBLANES + tm, :] = up[i * tm:(i + 1) * tm]
        conv = cb_ref[...]
        for c in range(FFN_CONV):
            conv = conv + buf[i, pl.ds(SUBLANES - (FFN_CONV - 1) + c, tm), :] * cw_ref[c:c + 1, :]
        tail = buf[i, pl.ds(t_valid, SUBLANES), :]
        tail_ref[i] = tail
        buf[i, 0:SUBLANES, :] = tail
        acts.append(_silu(conv[:, :D_FF]) * conv[:, D_FF:])
    act = (jnp.concatenate(acts, axis=0) if nb > 1 else acts[0]).astype(BF16)
    o_ref[...] = x + _dot(act, wdn_ref[...])


def _ffn(x, tail0, p, tm, t_valid, nb):
    bsz = tail0.shape[0]
    n, d = x.shape
    nt = n // bsz // tm
    assert nb == 1 or nt == 1
    kern = functools.partial(_ffn_kernel, tm=tm, t_valid=t_valid, nb=nb)
    return pl.pallas_call(
        kern,
        grid=(bsz // nb, nt),
        in_specs=[pl.BlockSpec((nb * tm, d), lambda b, j: (b * nt + j, 0)),
                  pl.BlockSpec((nb, SUBLANES, 2 * D_FF), lambda b, j: (b, 0, 0)),
                  _full((1, d)),
                  pl.BlockSpec((d, 2 * D_FF), lambda b, j: (0, 0), pipeline_mode=pl.Buffered(1)),
                  _full((FFN_CONV, 2 * D_FF)), _full((1, 2 * D_FF)),
                  pl.BlockSpec((D_FF, d), lambda b, j: (0, 0), pipeline_mode=pl.Buffered(1))],
        out_specs=[pl.BlockSpec((nb * tm, d), lambda b, j: (b * nt + j, 0)),
                   pl.BlockSpec((nb, SUBLANES, 2 * D_FF), lambda b, j: (b, 0, 0))],
        out_shape=[jax.ShapeDtypeStruct((n, d), F32),
                   jax.ShapeDtypeStruct((bsz, SUBLANES, 2 * D_FF), F32)],
        scratch_shapes=[pltpu.VMEM((nb, SUBLANES + tm, 2 * D_FF), F32)],
        compiler_params=_cparams("parallel", "arbitrary"),
    )(x, tail0, p["norm_ffn_g"], p["ffn_w_up"], p["ffn_conv_w"], p["ffn_conv_b"], p["ffn_w_down"])


def _pad_cols(w, width):
    return jnp.pad(w, ((0, 0), (0, width - w.shape[1])))


def _row(v):
    return v.reshape(1, -1)


def _prep_layer(i, w):
    d = RWKV_DIM
    w_in = w["w_in"][i]
    o_r = SSD_IN
    o_m = SSD_IN + RWKV_IN
    p = {}
    p["norm_mix_g"] = _row(w["norm_mix_g"][i])
    p["w_ssd"] = _pad_cols(w_in[:, :o_r], SSD_IN_PAD).astype(BF16)
    p["w_rwkv"] = w_in[:, o_r:o_m].astype(BF16)
    p["w_mla"] = _pad_cols(w_in[:, o_m:], MLA_IN_PAD).astype(BF16)
    p["w_gate"] = w["w_gate"][i].astype(BF16)
    p["b_gate"] = _row(w["b_gate"][i])
    p["ssd_conv_w"] = w["ssd_conv_w"][i]
    p["ssd_conv_b"] = _row(w["ssd_conv_b"][i])
    p["ssd_dt_bias"] = _pad_cols(_row(w["ssd_dt_bias"][i]), LANES)
    p["ssd_a_log"] = _pad_cols(_row(w["ssd_a_log"][i]), LANES)
    p["ssd_d"] = _row(jnp.repeat(w["ssd_d"][i], SSD_HEAD_DIM))
    p["ssd_norm_g"] = _row(w["ssd_norm_g"][i])
    p["rwkv_shift_mu"] = _row(w["rwkv_shift_mu"][i])
    p["rwkv_w0"] = _row(w["rwkv_w0"][i])
    zeros_lora = jnp.zeros((RWKV_DECAY_LORA, d), F32)
    p["rwkv_w_up"] = jnp.concatenate([w["rwkv_w_up"][i], zeros_lora], axis=0).astype(BF16)
    p["rwkv_a0"] = _row(w["rwkv_a0"][i])
    p["rwkv_a_up"] = jnp.concatenate([zeros_lora, w["rwkv_a_up"][i]], axis=0).astype(BF16)
    p["rwkv_g_up"] = w["rwkv_g_up"][i].astype(BF16)
    p["rwkv_k_k"] = _row(w["rwkv_k_k"][i])
    p["rwkv_k_a"] = _row(w["rwkv_k_a"][i])
    p["rwkv_r_k"] = _row(w["rwkv_r_k"][i])
    p["rwkv_ln_g"] = _row(w["rwkv_ln_g"][i])
    p["rwkv_ln_b"] = _row(w["rwkv_ln_b"][i])
    wq = w["mla_w_q_b"][i]
    hv = MLA_HEADS * MLA_NOPE
    p["mla_q_a_g"] = _row(w["mla_q_a_g"][i])
    p["wq_nope"] = wq[:, :, :MLA_NOPE].reshape(MLA_Q_RANK, hv).astype(BF16)
    p["wq_rope"] = jnp.pad(wq[:, :, MLA_NOPE:], ((0, 0), (0, 0), (0, MLA_NOPE - MLA_ROPE))
                           ).reshape(MLA_Q_RANK, hv).astype(BF16)
    p["mla_kv_a_g"] = _row(w["mla_kv_a_g"][i])
    p["w_kb"] = w["mla_w_kb"][i].reshape(MLA_KV_RANK, hv).astype(BF16)
    p["w_kb_t"] = p["w_kb"].T
    p["w_vb"] = w["mla_w_vb"][i].reshape(MLA_KV_RANK, MLA_HEADS * MLA_V).astype(BF16)
    gq, gk = w["mla_q_norm_g"][i], w["mla_k_norm_g"][i]
    p["gq_nope"] = _row(gq[:MLA_NOPE])
    p["gq_rope"] = _pad_cols(_row(gq[MLA_NOPE:]), LANES)
    p["gk_nope"] = _row(gk[:MLA_NOPE])
    p["gk_rope"] = _pad_cols(_row(gk[MLA_NOPE:]), LANES)
    for name in ("w_o_ssd", "w_o_rwkv", "w_o_mla", "w_out"):
        p[name] = w[name][i].astype(BF16)
    p["norm_ffn_g"] = _row(w["norm_ffn_g"][i])
    p["ffn_w_up"] = w["ffn_w_up"][i].astype(BF16)
    p["ffn_conv_w"] = w["ffn_conv_w"][i]
    p["ffn_conv_b"] = _row(w["ffn_conv_b"][i])
    p["ffn_w_down"] = w["ffn_w_down"][i].astype(BF16)
    return p


def _constants():
    ch = np.arange(RWKV_DIM)
    seg = (ch[:, None] // RWKV_HEAD_DIM == np.arange(LANES)[None, :]).astype(np.float32)
    seg2 = np.concatenate([seg, seg], axis=0)
    seg_t2 = np.concatenate([seg.T, seg.T], axis=0)
    return jnp.asarray(seg2, BF16), jnp.asarray(seg_t2, BF16)


def _rope_tables(pos):
    half = MLA_ROPE // 2
    inv = ROPE_BASE ** (-jnp.arange(half, dtype=F32) / half)
    ang = pos.astype(F32)[:, None] * inv[None, :]
    cos, sin = jnp.cos(ang), jnp.sin(ang)
    z = jnp.zeros_like(cos)
    z2 = jnp.zeros((pos.shape[0], LANES - MLA_ROPE), F32)
    return (jnp.concatenate([cos, cos, z2], axis=1),
            jnp.concatenate([-sin, z, z2], axis=1),
            jnp.concatenate([z, sin, z2], axis=1))


def _tail_block(state, rows):
    return jnp.pad(state, ((0, 0), (SUBLANES - rows, 0), (0, 0)))


def _pack_rwkv_state(s):
    b = s.shape[0]
    s5 = s.reshape(b, RWKV_PAIRS, 2, RWKV_HEAD_DIM, RWKV_HEAD_DIM)
    z = jnp.zeros_like(s5[:, :, 0])
    top = jnp.concatenate([s5[:, :, 0], z], axis=-1)
    bot = jnp.concatenate([z, s5[:, :, 1]], axis=-1)
    return jnp.concatenate([top, bot], axis=-2)


def _unpack_rwkv_state(s):
    b = s.shape[0]
    hd = RWKV_HEAD_DIM
    return jnp.stack([s[:, :, :hd, :hd], s[:, :, hd:, hd:]], axis=2).reshape(b, RWKV_HEADS, hd, hd)


def _layer(x, st, p, cfg, attend):
    bsz, t, t_valid, tm = cfg["bsz"], cfg["t"], cfg["t_valid"], cfg["tm"]
    n = x.shape[0]
    tmn = min(tm, n)

    if t % SSD_CHUNK:
        lc = SSD_SHORT_CHUNK
        x_pad = jnp.pad(x.reshape(bsz, t, D_MODEL), ((0, 0), (0, lc - t), (0, 0)))
        o_ssd, ssd_tail, ssd_h = _ssd(x_pad.reshape(bsz * lc, D_MODEL), st["ssd_tail"], st["ssd_h"], p, t_valid, lc,
                                      math.gcd(bsz, 8))
        o_ssd = o_ssd.reshape(bsz, lc, SSD_INNER)[:, :t].reshape(n, SSD_INNER)
    else:
        o_ssd, ssd_tail, ssd_h = _ssd(x, st["ssd_tail"], st["ssd_h"], p, SSD_CHUNK, SSD_CHUNK, 1)

    tm_r = min(tm, t)
    nb_r = math.gcd(bsz, 16) if t <= SAMPLE_T_PAD else 1
    r, lw, k2, v, al, be, gg, rwkv_tail = _rwkv_pre(x, st["rwkv_tail"], p, tm_r, min(t_valid, tm_r), nb_r)
    d = RWKV_DIM
    if t % RWKV_CHUNK:
        def chunk_pad(a_):
            a_ = a_.reshape(bsz, t, d)[:, :t_valid]
            return jnp.pad(a_, ((0, 0), (0, RWKV_CHUNK - t_valid), (0, 0))).reshape(bsz * RWKV_CHUNK, d)
        yy, rwkv_s = _rwkv_scan(*(chunk_pad(a_) for a_ in (r, lw, k2, v, al, be)), st["rwkv_s"], bsz, RWKV_CHUNK)
        yy = yy.reshape(bsz, RWKV_CHUNK, d)[:, :t].reshape(n, d)
    else:
        yy, rwkv_s = _rwkv_scan(r, lw, k2, v, al, be, st["rwkv_s"], bsz, min(t, cfg["tc_rwkv"]))
    o_rwkv = _rwkv_post(yy, r, k2, v, gg, p, tmn)

    q, k, vv, ckv, kpe = _mla_prep(x, cfg["rope"], p, tmn, cfg["q_dtype"])
    o_mla = attend(q, k, vv, ckv, kpe)

    x = _merge(x, o_ssd, o_rwkv, o_mla, p, tmn)
    tm_f = min(cfg["tm_ffn"], t)
    x, ffn_tail = _ffn(x, st["ffn_tail"], p, tm_f, min(t_valid, tm_f), cfg["nb_ffn"])
    new_st = dict(ssd_tail=ssd_tail, ssd_h=ssd_h, rwkv_tail=rwkv_tail, rwkv_s=rwkv_s, ffn_tail=ffn_tail)
    return x, new_st, ckv, kpe


def kernel(x_prompt, x_sample, cache_kv_latent, cache_k_rope, page_table, state_ssm, state_ssm_conv,
           state_rwkv, state_rwkv_shift, state_ffn_conv, norm_mix_g, w_in, ssd_conv_w, ssd_conv_b,
           ssd_dt_bias, ssd_a_log, ssd_d, ssd_norm_g, rwkv_shift_mu, rwkv_w0, rwkv_w_up, rwkv_a0,
           rwkv_a_up, rwkv_g_up, rwkv_k_k, rwkv_k_a, rwkv_r_k, rwkv_ln_g, rwkv_ln_b, mla_q_a_g,
           mla_w_q_b, mla_kv_a_g, mla_w_kb, mla_w_vb, mla_q_norm_g, mla_k_norm_g, w_gate, b_gate,
           w_o_ssd, w_o_rwkv, w_o_mla, w_out, norm_ffn_g, ffn_w_up, ffn_conv_w, ffn_conv_b, ffn_w_down):
    w = dict(norm_mix_g=norm_mix_g, w_in=w_in, ssd_conv_w=ssd_conv_w, ssd_conv_b=ssd_conv_b,
             ssd_dt_bias=ssd_dt_bias, ssd_a_log=ssd_a_log, ssd_d=ssd_d, ssd_norm_g=ssd_norm_g,
             rwkv_shift_mu=rwkv_shift_mu, rwkv_w0=rwkv_w0, rwkv_w_up=rwkv_w_up, rwkv_a0=rwkv_a0,
             rwkv_a_up=rwkv_a_up, rwkv_g_up=rwkv_g_up, rwkv_k_k=rwkv_k_k, rwkv_k_a=rwkv_k_a,
             rwkv_r_k=rwkv_r_k, rwkv_ln_g=rwkv_ln_g, rwkv_ln_b=rwkv_ln_b, mla_q_a_g=mla_q_a_g,
             mla_w_q_b=mla_w_q_b, mla_kv_a_g=mla_kv_a_g, mla_w_kb=mla_w_kb, mla_w_vb=mla_w_vb,
             mla_q_norm_g=mla_q_norm_g, mla_k_norm_g=mla_k_norm_g, w_gate=w_gate, b_gate=b_gate,
             w_o_ssd=w_o_ssd, w_o_rwkv=w_o_rwkv, w_o_mla=w_o_mla, w_out=w_out, norm_ffn_g=norm_ffn_g,
             ffn_w_up=ffn_w_up, ffn_conv_w=ffn_conv_w, ffn_conv_b=ffn_conv_b, ffn_w_down=ffn_w_down)
    depth = w_in.shape[0]
    bp, tp, d = x_prompt.shape
    bs, ts, _ = x_sample.shape
    tsp = SAMPLE_T_PAD
    past_len = page_table.shape[1] * PAGE_SIZE
    seg2, seg_t2 = _constants()
    rope_pool_t = jnp.swapaxes(cache_k_rope, 2, 3)

    tm_p = min(256, tp)
    cfg_p = dict(bsz=bp, t=tp, t_valid=tp, tm=tm_p, tm_ffn=min(256, tp), nb_ffn=1, tc_rwkv=256,
                 rope=_rope_tables(jnp.arange(tp)), q_dtype=BF16)
    pos_s = past_len + jnp.arange(tsp)
    rope_s = tuple(jnp.tile(tb, (bs, 1)) for tb in _rope_tables(pos_s))
    cfg_s = dict(bsz=bs, t=tsp, t_valid=ts, tm=bs * tsp, tm_ffn=tsp, nb_ffn=math.gcd(bs, 16), tc_rwkv=RWKV_CHUNK, rope=rope_s,
                 q_dtype=F32)

    xp = x_prompt.reshape(bp * tp, d)
    xs = jnp.pad(x_sample, ((0, 0), (0, tsp - ts), (0, 0))).reshape(bs * tsp, d)

    zero_st = dict(ssd_tail=jnp.zeros((bp, SUBLANES, SSD_CONV_DIM), F32),
                   ssd_h=jnp.zeros((bp, SSD_HEADS // 2, LANES, SSD_STATE), F32),
                   rwkv_tail=jnp.zeros((bp, SUBLANES, RWKV_IN), F32),
                   rwkv_s=jnp.zeros((bp, RWKV_PAIRS, LANES, LANES), F32),
                   ffn_tail=jnp.zeros((bp, SUBLANES, 2 * D_FF), F32))

    new_p = [[] for _ in range(7)]
    new_s = [[] for _ in range(7)]
    for i in range(depth):
        p = _prep_layer(i, w)
        p["seg2"], p["seg_t2"] = seg2, seg_t2

        def prompt_attend(q, k, v, ckv, kpe):
            return _flash(q, k, v, bp, min(512, tp))

        xp, st_p, ckv_p, kpe_p = _layer(xp, zero_st, p, cfg_p, prompt_attend)

        st_in = dict(ssd_tail=_tail_block(state_ssm_conv[i], SSD_CONV - 1),
                     ssd_h=state_ssm[i].reshape(bs, SSD_HEADS // 2, LANES, SSD_STATE),
                     rwkv_tail=_tail_block(state_rwkv_shift[i][:, None, :], 1),
                     rwkv_s=_pack_rwkv_state(state_rwkv[i]),
                     ffn_tail=_tail_block(state_ffn_conv[i], FFN_CONV - 1))

        def sample_attend(q, k, v, ckv, kpe, i=i, p=p):
            new_c = jnp.pad(ckv.reshape(bs, tsp, MLA_KV_RANK), ((0, 0), (0, PAGE_SIZE - tsp), (0, 0)))
            new_r_t = jnp.pad(jnp.swapaxes(kpe.reshape(bs, tsp, LANES), 1, 2), ((0, 0), (0, 0), (0, PAGE_SIZE - tsp)))
            return _paged_attention(page_table, q, cache_kv_latent, rope_pool_t, i, new_c, new_r_t, p, ts)

        xs, st_s, ckv_s, kpe_s = _layer(xs, st_in, p, cfg_s, sample_attend)

        for lst, st, ckv, kpe, b_, t_, tv in ((new_p, st_p, ckv_p, kpe_p, bp, tp, tp),
                                              (new_s, st_s, ckv_s, kpe_s, bs, tsp, ts)):
            lst[0].append(ckv.reshape(b_, t_, MLA_KV_RANK)[:, :tv])
            lst[1].append(kpe.reshape(b_, t_, LANES)[:, :tv, :MLA_ROPE])
            lst[2].append(st["ssd_h"].reshape(b_, SSD_HEADS, SSD_HEAD_DIM, SSD_STATE))
            lst[3].append(st["ssd_tail"][:, SUBLANES - (SSD_CONV - 1):])
            lst[4].append(_unpack_rwkv_state(st["rwkv_s"]))
            lst[5].append(st["rwkv_tail"][:, SUBLANES - 1])
            lst[6].append(st["ffn_tail"][:, SUBLANES - (FFN_CONV - 1):])

    outs_p = [jnp.stack(v_, axis=0) for v_ in new_p]
    outs_s = [jnp.stack(v_, axis=0) for v_ in new_s]
    y_p = xp.reshape(bp, tp, d)
    y_s = xs.reshape(bs, tsp, d)[:, :ts]
    return (y_p, y_s, *outs_p, *outs_s)
```

```python
import functools
import math

import numpy as np
import jax
import jax.numpy as jnp
from jax import lax
from jax.experimental import pallas as pl
from jax.experimental.pallas import tpu as pltpu

F32 = jnp.float32
BF16 = jnp.bfloat16

D_MODEL = 1024
PAGE_SIZE = 128

SSD_HEAD_DIM = 64
SSD_INNER = 1024
SSD_HEADS = 16
SSD_GROUPS = 2
SSD_STATE = 128
SSD_CONV = 4
SSD_CHUNK = 128
SSD_SHORT_CHUNK = 16
SSD_CONV_DIM = SSD_INNER + 2 * SSD_GROUPS * SSD_STATE
SSD_IN = SSD_INNER + SSD_CONV_DIM + SSD_HEADS
SSD_IN_PAD = SSD_INNER + SSD_CONV_DIM + 128

RWKV_HEAD_DIM = 64
RWKV_DIM = 1024
RWKV_HEADS = 16
RWKV_DECAY_LORA = 64
RWKV_A_LORA = 64
RWKV_GATE_LORA = 128
RWKV_IN = 3 * RWKV_DIM + RWKV_DECAY_LORA + RWKV_A_LORA + RWKV_GATE_LORA
RWKV_LN_EPS = 64e-5

MLA_HEADS = 8
MLA_NOPE = 128
MLA_ROPE = 64
MLA_QK = MLA_NOPE + MLA_ROPE
MLA_V = 128
MLA_Q_RANK = 512
MLA_KV_RANK = 256
MLA_IN = MLA_Q_RANK + MLA_KV_RANK + MLA_ROPE
MLA_IN_PAD = MLA_Q_RANK + MLA_KV_RANK + 128
MLA_QK_PAD = 256
ROPE_BASE = 10000.0
MLA_SCALE = MLA_QK ** -0.5

N_BRANCH = 3
D_FF = 2816
FFN_CONV = 3
NORM_EPS = 1e-6

SUBLANES = 8
LANES = 128
SAMPLE_T_PAD = 8
NEG_BIG = -1e30

V7X_VMEM_BYTES = 64 * 1024 * 1024
VMEM_LIMIT = V7X_VMEM_BYTES // 8 * 7


def _cparams(*sem):
    return pltpu.CompilerParams(dimension_semantics=sem, vmem_limit_bytes=VMEM_LIMIT)


def _full(shape):
    nd = len(shape)
    return pl.BlockSpec(shape, lambda *_: (0,) * nd)


def _rms(x, eps=NORM_EPS):
    return x * lax.rsqrt(jnp.mean(x * x, axis=-1, keepdims=True) + eps)


def _sigmoid(x):
    return 1.0 / (1.0 + jnp.exp(-x))


def _silu(x):
    return x * _sigmoid(x)


def _softplus(x):
    return jnp.maximum(x, 0.0) + jnp.log(1.0 + jnp.exp(-jnp.abs(x)))


def _dot(a, b):
    return jnp.dot(a, b, preferred_element_type=F32)


def _dot_nt(a, b):
    return lax.dot_general(a, b, (((1,), (1,)), ((), ())), preferred_element_type=F32)


def _dot_tn(a, b):
    return lax.dot_general(a, b, (((0,), (0,)), ((), ())), preferred_element_type=F32)


def _hi_lo(x):
    hi = x.astype(BF16)
    return hi, (x - hi.astype(F32)).astype(BF16)


def _split_hi_lo(x):
    return jnp.concatenate(_hi_lo(x), axis=1)


def _in_proj(x_ref, g_ref, w_ref):
    return _dot((_rms(x_ref[...]) * g_ref[...]).astype(BF16), w_ref[...])


def _resident(shape):
    nd = len(shape)
    return pl.BlockSpec(shape, lambda *_: (0,) * nd, pipeline_mode=pl.Buffered(1))


def _ssd_kernel(x_ref, g_ref, w_ref, tail0_ref, h0_ref, cw_ref, cb_ref, dtb_ref, alog_ref, dsk_ref, ng_ref,
                o_ref, tail_ref, h_ref, buf, *, t_valid, L, nb):
    c = pl.program_id(1)

    @pl.when(c == 0)
    def _():
        buf[:, 0:SUBLANES, :] = tail0_ref[...]
        h_ref[...] = h0_ref[...]

    u_all = _in_proj(x_ref, g_ref, w_ref)
    for i in range(nb):
        rows = slice(i * L, (i + 1) * L)
        o_ref[rows, :] = _ssd_chunk(u_all[rows], i, cw_ref, cb_ref, dtb_ref, alog_ref, dsk_ref, ng_ref,
                                    tail_ref, h_ref, buf, t_valid, L)


def _ssd_chunk(u, i, cw_ref, cb_ref, dtb_ref, alog_ref, dsk_ref, ng_ref, tail_ref, h_ref, buf, t_valid, L):
    z = u[:, :SSD_INNER]
    dt_raw = u[:, SSD_INNER + SSD_CONV_DIM:]
    buf[i, SUBLANES:SUBLANES + L, :] = u[:, SSD_INNER:SSD_INNER + SSD_CONV_DIM]
    acc = cb_ref[...]
    for j in range(SSD_CONV):
        acc = acc + buf[i, pl.ds(SUBLANES - (SSD_CONV - 1) + j, L), :] * cw_ref[j:j + 1, :]
    tail = buf[i, pl.ds(t_valid, SUBLANES), :]
    tail_ref[i] = tail
    buf[i, 0:SUBLANES, :] = tail
    xbc = _silu(acc)
    xs = xbc[:, :SSD_INNER]

    dt = _softplus(dt_raw + dtb_ref[...])
    if t_valid < L:
        row = lax.broadcasted_iota(jnp.int32, dt.shape, 0)
        dt = jnp.where(row < t_valid, dt, 0.0)
    a = -jnp.exp(alog_ref[...])
    da = dt * a
    ri = lax.broadcasted_iota(jnp.int32, (L, L), 0)
    ci = lax.broadcasted_iota(jnp.int32, (L, L), 1)
    causal = ri >= ci
    tri = jnp.where(causal, 1.0, 0.0).astype(F32)
    cs = jnp.dot(tri, da, preferred_element_type=F32, precision=lax.Precision.HIGHEST)
    cs_t = cs.T
    total = cs[L - 1:L, :]
    etot = jnp.exp(total)
    lane = lax.broadcasted_iota(jnp.int32, (L, LANES), 1)
    lo_half = lane < SSD_HEAD_DIM
    row_lo = lax.broadcasted_iota(jnp.int32, (LANES, LANES), 0) < SSD_HEAD_DIM

    def pair_cols(m, j0):
        return jnp.where(lo_half, m[:, j0:j0 + 1], m[:, j0 + 1:j0 + 2])

    hpg = SSD_HEADS // SSD_GROUPS
    y_parts = []
    for g in range(SSD_GROUPS):
        b_g = xbc[:, SSD_INNER + g * SSD_STATE:SSD_INNER + (g + 1) * SSD_STATE].astype(BF16)
        c0 = SSD_INNER + SSD_GROUPS * SSD_STATE + g * SSD_STATE
        c_g = xbc[:, c0:c0 + SSD_STATE].astype(BF16)
        cb = _dot_nt(c_g, b_g)
        for m in range(hpg // 2):
            q = g * (hpg // 2) + m
            j0 = 2 * q
            xs_p = xs[:, q * LANES:(q + 1) * LANES]
            xd = xs_p * pair_cols(dt, j0)
            xd_b = xd.astype(BF16)
            ys = []
            for e in range(2):
                j = j0 + e
                diff = cs[:, j:j + 1] - cs_t[j:j + 1, :]
                lm = jnp.where(causal, jnp.exp(jnp.where(causal, diff, 0.0)), 0.0)
                ys.append(_dot((cb * lm).astype(BF16), xd_b))
            y_diag = jnp.where(lo_half, ys[0], ys[1])
            hp = h_ref[i, q]
            cs_p = pair_cols(cs, j0)
            y_off = _dot_nt(c_g, hp.astype(BF16)) * jnp.exp(cs_p)
            tot_p = jnp.where(lo_half[0:1, :], total[:, j0:j0 + 1], total[:, j0 + 1:j0 + 2])
            dte = jnp.exp(tot_p - cs_p)
            contrib = _dot_tn((xd * dte).astype(BF16), b_g)
            decay = jnp.where(row_lo, etot[:, j0:j0 + 1], etot[:, j0 + 1:j0 + 2])
            h_ref[i, q] = hp * decay + contrib
            y_parts.append(y_diag + y_off + dsk_ref[:, q * LANES:(q + 1) * LANES] * xs_p)
    y = jnp.concatenate(y_parts, axis=1) * _silu(z)
    gw = SSD_INNER // SSD_GROUPS
    outs = [_rms(y[:, g * gw:(g + 1) * gw]) * ng_ref[:, g * gw:(g + 1) * gw] for g in range(SSD_GROUPS)]
    return jnp.concatenate(outs, axis=1).astype(BF16)


def _ssd(x, tail0, h0, p, t_valid, L, nb):
    bsz = h0.shape[0]
    n, d = x.shape
    nch = n // bsz // L
    assert nb == 1 or nch == 1
    kern = functools.partial(_ssd_kernel, t_valid=t_valid, L=L, nb=nb)
    return pl.pallas_call(
        kern,
        grid=(bsz // nb, nch),
        in_specs=[pl.BlockSpec((nb * L, d), lambda b, c: (b * nch + c, 0)), _full((1, d)),
                  _resident((d, SSD_IN_PAD)),
                  pl.BlockSpec((nb, SUBLANES, SSD_CONV_DIM), lambda b, c: (b, 0, 0)),
                  pl.BlockSpec((nb, SSD_HEADS // 2, LANES, SSD_STATE), lambda b, c: (b, 0, 0, 0)),
                  _full((SSD_CONV, SSD_CONV_DIM)), _full((1, SSD_CONV_DIM)),
                  _full((1, LANES)), _full((1, LANES)), _full((1, SSD_INNER)), _full((1, SSD_INNER))],
        out_specs=[pl.BlockSpec((nb * L, SSD_INNER), lambda b, c: (b * nch + c, 0)),
                   pl.BlockSpec((nb, SUBLANES, SSD_CONV_DIM), lambda b, c: (b, 0, 0)),
                   pl.BlockSpec((nb, SSD_HEADS // 2, LANES, SSD_STATE), lambda b, c: (b, 0, 0, 0))],
        out_shape=[jax.ShapeDtypeStruct((n, SSD_INNER), BF16),
                   jax.ShapeDtypeStruct((bsz, SUBLANES, SSD_CONV_DIM), F32),
                   jax.ShapeDtypeStruct((bsz, SSD_HEADS // 2, LANES, SSD_STATE), F32)],
        scratch_shapes=[pltpu.VMEM((nb, SUBLANES + L, SSD_CONV_DIM), F32)],
        compiler_params=_cparams("parallel", "arbitrary"),
    )(x, p["norm_mix_g"], p["w_ssd"], tail0, h0, p["ssd_conv_w"], p["ssd_conv_b"], p["ssd_dt_bias"],
      p["ssd_a_log"], p["ssd_d"], p["ssd_norm_g"])


def _seg_sum_bcast(x, seg2, seg_t2):
    s16 = _dot(_split_hi_lo(x), seg2)
    return _dot(_split_hi_lo(s16), seg_t2)


def _rwkv_pre_kernel(x_ref, g_in_ref, w_in_ref, tail0_ref, mu_ref, w0_ref, wup_ref, a0_ref, aup_ref, gup_ref, kk_ref,
                     ka_ref, seg2_ref, segt2_ref,
                     r_ref, w_ref, k_ref, v_ref, al_ref, be_ref, g_ref, tail_ref, buf, *, tm, t_valid, nb):
    j = pl.program_id(1)

    @pl.when(j == 0)
    def _():
        buf[:, 0:SUBLANES, :] = tail0_ref[...]

    u = _in_proj(x_ref, g_in_ref, w_in_ref)
    prevs = []
    for i in range(nb):
        buf[i, SUBLANES:SUBLANES + tm, :] = u[i * tm:(i + 1) * tm]
        prevs.append(buf[i, pl.ds(SUBLANES - 1, tm), :])
        tail = buf[i, pl.ds(t_valid, SUBLANES), :]
        tail_ref[i] = tail
        buf[i, 0:SUBLANES, :] = tail
    prev = jnp.concatenate(prevs, axis=0) if nb > 1 else prevs[0]

    d = RWKV_DIM
    f = u + mu_ref[...] * (prev - u)
    r, k, v = f[:, :d], f[:, d:2 * d], f[:, 2 * d:3 * d]
    lo = f[:, 3 * d:3 * d + LANES]
    glo = f[:, 3 * d + LANES:]
    ww = w0_ref[...] + _dot(jnp.tanh(lo).astype(BF16), wup_ref[...])
    w_log = -_softplus(-ww) - 0.5
    a = _sigmoid(a0_ref[...] + _dot(lo.astype(BF16), aup_ref[...]))
    kk = k * kk_ref[...]
    ssb = _seg_sum_bcast(kk * kk, seg2_ref[...], segt2_ref[...])
    kkn = kk / jnp.maximum(jnp.sqrt(ssb), 1e-12)
    r_ref[...] = r
    w_ref[...] = -jnp.exp(w_log)
    k_ref[...] = k * (1.0 + (a - 1.0) * ka_ref[...])
    v_ref[...] = v
    al_ref[...] = -kkn
    be_ref[...] = kkn * a
    g_ref[...] = _dot(_sigmoid(glo).astype(BF16), gup_ref[...])


def _rwkv_pre(x, tail0, p, tm, t_valid, nb):
    bsz = tail0.shape[0]
    n = x.shape[0]
    nt = n // bsz // tm
    assert nb == 1 or nt == 1
    d = RWKV_DIM
    kern = functools.partial(_rwkv_pre_kernel, tm=tm, t_valid=t_valid, nb=nb)
    row = pl.BlockSpec((nb * tm, d), lambda b, j: (b * nt + j, 0))
    return pl.pallas_call(
        kern,
        grid=(bsz // nb, nt),
        in_specs=[row, _full((1, D_MODEL)), _resident((D_MODEL, RWKV_IN)),
                  pl.BlockSpec((nb, SUBLANES, RWKV_IN), lambda b, j: (b, 0, 0)),
                  _full((1, RWKV_IN)), _full((1, d)), _full((LANES, d)), _full((1, d)),
                  _full((LANES, d)), _full((LANES, d)), _full((1, d)), _full((1, d)),
                  _full((2 * d, LANES)), _full((2 * LANES, d))],
        out_specs=[row] * 7 + [pl.BlockSpec((nb, SUBLANES, RWKV_IN), lambda b, j: (b, 0, 0))],
        out_shape=[jax.ShapeDtypeStruct((n, d), F32)] * 7
                  + [jax.ShapeDtypeStruct((bsz, SUBLANES, RWKV_IN), F32)],
        scratch_shapes=[pltpu.VMEM((nb, SUBLANES + tm, RWKV_IN), F32)],
        compiler_params=_cparams("parallel", "arbitrary"),
    )(x, p["norm_mix_g"], p["w_rwkv"], tail0, p["rwkv_shift_mu"], p["rwkv_w0"], p["rwkv_w_up"], p["rwkv_a0"],
      p["rwkv_a_up"],
      p["rwkv_g_up"], p["rwkv_k_k"], p["rwkv_k_a"], p["seg2"], p["seg_t2"])


RWKV_CHUNK = 64
RWKV_SOLVE_BLOCK = 16
RWKV_PAIRS = RWKV_HEADS // 2


def _mm(a, b):
    return _dot(a.astype(BF16), b.astype(BF16))


def _mm2(a, b):
    ah, al = _hi_lo(a)
    m = a.shape[0]
    both = _dot(jnp.concatenate([ah, al], axis=0), b.astype(BF16))
    return both[:m] + both[m:]


def _rwkv_chunk_kernel(r_ref, lw_ref, k_ref, v_ref, al_ref, be_ref, s0_ref, y_ref, s_ref,
                       *, nchunks, single_block):
    c = pl.program_id(1)

    @pl.when(c == 0)
    def _():
        s_ref[...] = s0_ref[...]

    C, HD = RWKV_CHUNK, RWKV_HEAD_DIM
    r2 = lax.broadcasted_iota(jnp.int32, (LANES, LANES), 0)
    c2 = lax.broadcasted_iota(jnp.int32, (LANES, LANES), 1)
    eye = jnp.where(r2 == c2, 1.0, 0.0).astype(F32)
    diag_blk = (r2 // RWKV_SOLVE_BLOCK) == (c2 // RWKV_SOLVE_BLOCK)
    same_head = (r2 // HD) == (c2 // HD)
    tri = jnp.where(lax.broadcasted_iota(jnp.int32, (C, C), 0) >= lax.broadcasted_iota(jnp.int32, (C, C), 1),
                    1.0, 0.0).astype(F32)
    t_i = lax.broadcasted_iota(jnp.int32, (C, LANES), 0)
    lane = lax.broadcasted_iota(jnp.int32, (C, LANES), 1)
    lo_half = lane < HD
    hi_half = lane >= HD
    strict = (lane % HD) < t_i
    incl = (lane % HD) <= t_i
    zeros = jnp.zeros((C, LANES), F32)

    def body(ci, carry):
        rows = pl.ds(pl.multiple_of(ci * C, C), C)
        cum_all = jnp.dot(tri, lw_ref[rows, :], preferred_element_type=F32, precision=lax.Precision.HIGHEST)
        pairs = range(RWKV_PAIRS)
        sls = [slice(p * LANES, (p + 1) * LANES) for p in pairs]
        cum = [cum_all[:, sl] for sl in sls]
        be = [be_ref[rows, sl] for sl in sls]
        kk = [k_ref[rows, sl] for sl in sls]
        vv = [v_ref[rows, sl] for sl in sls]
        g_inv = [jnp.exp(-cum[p]) for p in pairs]
        ar = [jnp.concatenate([al_ref[rows, sls[p]] * jnp.exp(cum[p] - lw_ref[rows, sls[p]]),
                               r_ref[rows, sls[p]] * jnp.exp(cum[p])], axis=0) for p in pairs]
        ar_b = [ar[p].astype(BF16) for p in pairs]
        bt = [be[p] * g_inv[p] for p in pairs]
        kt = [kk[p] * g_inv[p] for p in pairs]
        x01 = [_dot_nt(ar_b[p], jnp.concatenate(
            [jnp.where(lo_half, bt[p], 0.0), jnp.where(lo_half, kt[p], 0.0),
             jnp.where(hi_half, kt[p], 0.0), jnp.where(hi_half, bt[p], 0.0)], axis=0).astype(BF16)) for p in pairs]
        x0 = [x01[p][:, :LANES] for p in pairs]
        x1 = [x01[p][:, LANES:] for p in pairs]
        nb = [jnp.concatenate([jnp.where(lo_half & strict, x0[p][:C], 0.0),
                               jnp.where(hi_half & strict, x1[p][:C], 0.0)], axis=0) for p in pairs]
        sbd = [s_ref[p] for p in pairs]

        def read_state(p):
            s_hi, s_lo = _hi_lo(sbd[p])
            return _dot_nt(jnp.concatenate([ar_b[p], ar_b[p]], axis=1),
                           jnp.concatenate([s_hi, s_lo], axis=1))

        ars = [read_state(p) for p in pairs]

        def strict_ak(p):
            lhs = jnp.concatenate([jnp.where(hi_half & strict, x0[p][:C], 0.0),
                                   jnp.where(lo_half & strict, x1[p][:C], 0.0)], axis=0)
            rhs_w = jnp.concatenate([jnp.concatenate([zeros, vv[p]], axis=0),
                                     jnp.concatenate([vv[p], zeros], axis=0)], axis=1)
            both = _mm(lhs, rhs_w)
            return jnp.where(lo_half, both[:C, :LANES], both[C:, LANES:])

        rhs = [ars[p][:C] + strict_ak(p) for p in pairs]
        rhs2 = [jnp.concatenate([rhs[p], rhs[p]], axis=0) for p in pairs]
        nd = [jnp.where(diag_blk, nb[p], 0.0) for p in pairs]
        loff = [nb[p] - nd[p] for p in pairs]
        pm = [eye + nd[p] for p in pairs]
        wide = lambda a_, b_: jnp.concatenate([a_, b_], axis=1)
        n2 = [_mm2(nd[p], nd[p]) for p in pairs]
        t2 = [_mm2(n2[p], wide(pm[p], n2[p])) for p in pairs]
        pm = [pm[p] + t2[p][:, :LANES] for p in pairs]
        t4 = [_mm(t2[p][:, LANES:], wide(pm[p], t2[p][:, LANES:])) for p in pairs]
        pm = [pm[p] + t4[p][:, :LANES] for p in pairs]
        td = [pm[p] + _mm(t4[p][:, LANES:], pm[p]) for p in pairs]
        if single_block:
            u2 = [_mm2(td[p], rhs2[p]) for p in pairs]
        else:
            t1 = [_mm2(td[p], wide(loff[p], rhs2[p])) for p in pairs]
            m1 = [t1[p][:, :LANES] for p in pairs]
            x_a = [t1[p][:, LANES:] for p in pairs]
            m2 = [_mm(m1[p], m1[p]) for p in pairs]
            x_b = [x_a[p] + _mm(m2[p], x_a[p]) for p in pairs]
            u2 = [x_b[p] + _mm(m1[p], x_b[p]) for p in pairs]
        up = [jnp.where(lo_half, u2[p][:C], u2[p][C:]) for p in pairs]
        for p in pairs:
            uv = jnp.concatenate([up[p], vv[p]], axis=0)
            vu = jnp.concatenate([vv[p], up[p]], axis=0)
            both = _mm(jnp.concatenate([jnp.where(incl, x0[p][C:], 0.0), jnp.where(incl, x1[p][C:], 0.0)], axis=0),
                       jnp.concatenate([uv, vu], axis=1))
            y_ref[rows, sls[p]] = ars[p][C:] + jnp.where(lo_half, both[:C, :LANES], both[C:, LANES:])
            cum_c = cum[p][C - 1:C, :]
            g_end = jnp.exp(cum_c - cum[p])
            bkh = jnp.concatenate([be[p] * g_end, kk[p] * g_end], axis=0)
            upd = _dot_tn(uv.astype(BF16), bkh.astype(BF16))
            s_ref[p] = sbd[p] * jnp.exp(cum_c) + jnp.where(same_head, upd, 0.0)
        return carry

    lax.fori_loop(0, nchunks, body, 0)


def _rwkv_scan(r, lw, k, v, al, be, s0, bsz, tc, single_block):
    n, d = r.shape
    t_total = n // bsz
    nt = t_total // tc
    kern = functools.partial(_rwkv_chunk_kernel, nchunks=tc // RWKV_CHUNK, single_block=single_block)
    row = pl.BlockSpec((tc, d), lambda b, c: (b * nt + c, 0))
    st = pl.BlockSpec((None, RWKV_PAIRS, LANES, LANES), lambda b, c: (b, 0, 0, 0))
    return pl.pallas_call(
        kern,
        grid=(bsz, nt),
        in_specs=[row] * 6 + [st],
        out_specs=[row, st],
        out_shape=[jax.ShapeDtypeStruct((n, d), F32),
                   jax.ShapeDtypeStruct((bsz, RWKV_PAIRS, LANES, LANES), F32)],
        compiler_params=_cparams("parallel", "arbitrary"),
    )(r, lw, k, v, al, be, s0)


def _rwkv_post_kernel(y_ref, r_ref, k_ref, v_ref, g_ref, lng_ref, lnb_ref, rk_ref, seg2_ref, segt2_ref, o_ref):
    seg2, seg_t2 = seg2_ref[...], segt2_ref[...]
    y = y_ref[...]
    inv = 1.0 / RWKV_HEAD_DIM
    dlt = y - _seg_sum_bcast(y, seg2, seg_t2) * inv
    var = _seg_sum_bcast(dlt * dlt, seg2, seg_t2) * inv
    yn = dlt * lax.rsqrt(var + RWKV_LN_EPS) * lng_ref[...] + lnb_ref[...]
    bonus = _seg_sum_bcast(r_ref[...] * k_ref[...] * rk_ref[...], seg2, seg_t2) * v_ref[...]
    o_ref[...] = ((yn + bonus) * g_ref[...]).astype(BF16)


def _rwkv_post(y, r, k, v, g, p, tm):
    n, d = y.shape
    row = pl.BlockSpec((tm, d), lambda i: (i, 0))
    return pl.pallas_call(
        _rwkv_post_kernel,
        grid=(n // tm,),
        in_specs=[row] * 5 + [_full((1, d))] * 3 + [_full((2 * d, LANES)), _full((2 * LANES, d))],
        out_specs=row,
        out_shape=jax.ShapeDtypeStruct((n, d), BF16),
        compiler_params=_cparams("parallel"),
    )(y, r, k, v, g, p["rwkv_ln_g"], p["rwkv_ln_b"], p["rwkv_r_k"], p["seg2"], p["seg_t2"])


def _mla_prep_kernel(x_ref, g_in_ref, w_in_ref, cos_ref, s1_ref, s2_ref, qag_ref, wqn_ref, wqr_ref, kvg_ref,
                     wkb_ref, wvb_ref,
                     gqn_ref, gqr_ref, gkn_ref, gkr_ref, q_ref, k_ref, v_ref, ckv_ref, kpe_ref):
    u = _in_proj(x_ref, g_in_ref, w_in_ref)
    cq = u[:, :MLA_Q_RANK]
    ckv_raw = u[:, MLA_Q_RANK:MLA_Q_RANK + MLA_KV_RANK]
    kr_raw = u[:, MLA_Q_RANK + MLA_KV_RANK:]
    cos, s1, s2 = cos_ref[...], s1_ref[...], s2_ref[...]

    def rope(blk):
        return (blk * cos + pltpu.roll(blk, LANES - MLA_ROPE // 2, 1) * s1
                + pltpu.roll(blk, MLA_ROPE // 2, 1) * s2)

    cqn = (_rms(cq) * qag_ref[...]).astype(BF16)
    qn = _dot(cqn, wqn_ref[...])
    qr = _dot(cqn, wqr_ref[...])
    ckv = _rms(ckv_raw) * kvg_ref[...]
    ckv_ref[...] = ckv
    kpe = rope(kr_raw)
    kpe_ref[...] = kpe
    ckb = ckv.astype(BF16)
    kn = _dot(ckb, wkb_ref[...])
    v_ref[...] = _dot(ckb, wvb_ref[...]).astype(BF16)
    ss_kpe = jnp.sum(kpe * kpe, axis=-1, keepdims=True)
    gqn, gqr, gkn, gkr = gqn_ref[...], gqr_ref[...], gkn_ref[...], gkr_ref[...]
    for h in range(MLA_HEADS):
        sl = slice(h * MLA_NOPE, (h + 1) * MLA_NOPE)
        o0 = h * MLA_QK_PAD
        qn_h = qn[:, sl]
        qr_h = rope(qr[:, sl])
        ss = jnp.sum(qn_h * qn_h, axis=-1, keepdims=True) + jnp.sum(qr_h * qr_h, axis=-1, keepdims=True)
        rn = lax.rsqrt(ss * (1.0 / MLA_QK) + NORM_EPS) * MLA_SCALE
        q_ref[:, o0:o0 + MLA_NOPE] = (qn_h * rn * gqn).astype(q_ref.dtype)
        q_ref[:, o0 + MLA_NOPE:o0 + MLA_QK_PAD] = (qr_h * rn * gqr).astype(q_ref.dtype)
        kn_h = kn[:, sl]
        ssk = jnp.sum(kn_h * kn_h, axis=-1, keepdims=True) + ss_kpe
        rnk = lax.rsqrt(ssk * (1.0 / MLA_QK) + NORM_EPS)
        k_ref[:, o0:o0 + MLA_NOPE] = (kn_h * rnk * gkn).astype(BF16)
        k_ref[:, o0 + MLA_NOPE:o0 + MLA_QK_PAD] = (kpe * rnk * gkr).astype(BF16)


def _mla_prep(x, rope_tabs, p, tm, q_dtype):
    n, d = x.shape
    ttab = rope_tabs[0].shape[0]
    ntab = ttab // tm
    hq = MLA_HEADS * MLA_QK_PAD
    hv = MLA_HEADS * MLA_V
    tab = pl.BlockSpec((tm, LANES), lambda i: (i % ntab, 0))
    return pl.pallas_call(
        _mla_prep_kernel,
        grid=(n // tm,),
        in_specs=[pl.BlockSpec((tm, d), lambda i: (i, 0)), _full((1, d)), _full((d, MLA_IN_PAD)), tab, tab, tab,
                  _full((1, MLA_Q_RANK)), _full((MLA_Q_RANK, hv)), _full((MLA_Q_RANK, hv)),
                  _full((1, MLA_KV_RANK)), _full((MLA_KV_RANK, hv)), _full((MLA_KV_RANK, hv)),
                  _full((1, LANES)), _full((1, LANES)), _full((1, LANES)), _full((1, LANES))],
        out_specs=[pl.BlockSpec((tm, hq), lambda i: (i, 0)), pl.BlockSpec((tm, hq), lambda i: (i, 0)),
                   pl.BlockSpec((tm, hv), lambda i: (i, 0)),
                   pl.BlockSpec((tm, MLA_KV_RANK), lambda i: (i, 0)),
                   pl.BlockSpec((tm, LANES), lambda i: (i, 0))],
        out_shape=[jax.ShapeDtypeStruct((n, hq), q_dtype), jax.ShapeDtypeStruct((n, hq), BF16),
                   jax.ShapeDtypeStruct((n, hv), BF16), jax.ShapeDtypeStruct((n, MLA_KV_RANK), F32),
                   jax.ShapeDtypeStruct((n, LANES), F32)],
        compiler_params=_cparams("parallel"),
    )(x, p["norm_mix_g"], p["w_mla"], *rope_tabs, p["mla_q_a_g"], p["wq_nope"], p["wq_rope"], p["mla_kv_a_g"], p["w_kb"], p["w_vb"],
      p["gq_nope"], p["gq_rope"], p["gk_nope"], p["gk_rope"])


FLASH_HEAD_GROUP = 2


def _flash_kernel(q_ref, k_ref, v_ref, o_ref, m_sc, acc_sc, *, tq):
    qi = pl.program_id(1)
    ki = pl.program_id(2)
    nblk = tq // LANES

    @pl.when(ki == 0)
    def _():
        m_sc[...] = jnp.full_like(m_sc, NEG_BIG)
        acc_sc[...] = jnp.zeros_like(acc_sc)

    ones = jnp.ones((tq, LANES), BF16)

    def update(diagonal):
        for h0 in range(0, MLA_HEADS, FLASH_HEAD_GROUP):
            heads = range(h0, h0 + FLASH_HEAD_GROUP)
            s = {h: _dot_nt(q_ref[:, h * MLA_QK_PAD:(h + 1) * MLA_QK_PAD],
                            k_ref[:, h * MLA_QK_PAD:(h + 1) * MLA_QK_PAD]) for h in heads}
            if diagonal:
                row = lax.broadcasted_iota(jnp.int32, (tq, tq), 0)
                col = lax.broadcasted_iota(jnp.int32, (tq, tq), 1)
                s = {h: jnp.where(col <= row, s[h], NEG_BIG) for h in heads}
            m_prev = {h: m_sc[h] for h in heads}
            m_new = {h: jnp.maximum(m_prev[h], jnp.max(s[h], axis=-1, keepdims=True)) for h in heads}
            pr = {h: jnp.concatenate([jnp.exp(s[h][:, j * LANES:(j + 1) * LANES] - m_new[h]) for j in range(nblk)],
                                     axis=1).astype(BF16) for h in heads}
            corr = {h: jnp.exp(m_prev[h] - m_new[h]) for h in heads}
            for h in heads:
                v_ext = jnp.concatenate([v_ref[:, h * MLA_V:(h + 1) * MLA_V], ones], axis=1)
                acc_sc[h] = acc_sc[h] * jnp.concatenate([corr[h], corr[h]], axis=1) + _dot(pr[h], v_ext)
                m_sc[h] = m_new[h]

    @pl.when(ki < qi)
    def _():
        update(False)

    @pl.when(ki == qi)
    def _():
        update(True)
        for h in range(MLA_HEADS):
            acc = acc_sc[h]
            o_ref[:, h * MLA_V:(h + 1) * MLA_V] = (acc[:, :MLA_V] / acc[:, MLA_V:]).astype(BF16)


def _flash(q, k, v, bsz, tq):
    n = q.shape[0]
    nq = n // bsz // tq
    hq = MLA_HEADS * MLA_QK_PAD
    hv = MLA_HEADS * MLA_V
    kern = functools.partial(_flash_kernel, tq=tq)
    return pl.pallas_call(
        kern,
        grid=(bsz, nq, nq),
        in_specs=[pl.BlockSpec((tq, hq), lambda b, i, j: (b * nq + i, 0)),
                  pl.BlockSpec((tq, hq), lambda b, i, j: (b * nq + jnp.minimum(i, j), 0)),
                  pl.BlockSpec((tq, hv), lambda b, i, j: (b * nq + jnp.minimum(i, j), 0))],
        out_specs=pl.BlockSpec((tq, hv), lambda b, i, j: (b * nq + i, 0)),
        out_shape=jax.ShapeDtypeStruct((n, hv), BF16),
        scratch_shapes=[pltpu.VMEM((MLA_HEADS, tq, LANES), F32),
                        pltpu.VMEM((MLA_HEADS, tq, MLA_V + LANES), F32)],
        compiler_params=_cparams("parallel", "parallel", "arbitrary"),
    )(q, k, v)


PAGES_PER_STEP = 32
PAGED_ROWS = MLA_HEADS * SAMPLE_T_PAD


def _paged_kernel(pt_ref, q_ref, *refs, t_new, npp):
    lat_refs = refs[:npp]
    rope_refs = refs[npp:2 * npp]
    (newc_ref, newr_ref, wkbt_ref, wvb_ref, gkn_ref, gkr_ref, o_ref,
     m_sc, l_sc, acc_sc, wq_sc, qr_sc) = refs[2 * npp:]
    step = pl.program_id(1)
    nkn = MLA_HEADS * MLA_NOPE

    @pl.when(step == 0)
    def _():
        m_sc[...] = jnp.full_like(m_sc, NEG_BIG)
        l_sc[...] = jnp.zeros_like(l_sc)
        acc_sc[...] = jnp.zeros_like(acc_sc)
        q = q_ref[...]
        gkn, gkr = gkn_ref[...], gkr_ref[...]
        wkbt = wkbt_ref[...]
        wq_sc[0:nkn, :] = wkbt
        for h in range(MLA_HEADS):
            o0 = h * MLA_QK_PAD
            rows = slice(h * SAMPLE_T_PAD, (h + 1) * SAMPLE_T_PAD)
            qn = (q[:, o0:o0 + MLA_NOPE] * gkn).astype(BF16)
            wq_sc[nkn + h * SAMPLE_T_PAD:nkn + (h + 1) * SAMPLE_T_PAD, :] = _dot(
                qn, wkbt[h * MLA_NOPE:(h + 1) * MLA_NOPE, :]).astype(BF16)
            qr_sc[rows, :] = (q[:, o0 + MLA_NOPE:o0 + MLA_QK_PAD] * gkr).astype(BF16)

    wq = wq_sc[...]
    qr = qr_sc[...]

    def scores(cbs, krs):
        n = len(cbs)
        big = [_dot_nt(wq, cbs[i]) for i in range(n)]
        rope = [_dot(qr, krs[i].astype(BF16)) for i in range(n)]
        ssr = [jnp.sum(krs[i] * krs[i], axis=0, keepdims=True) for i in range(n)]
        out = []
        for i in range(n):
            rn = []
            for h in range(MLA_HEADS):
                kn_h = big[i][h * MLA_NOPE:(h + 1) * MLA_NOPE]
                ss = jnp.sum(kn_h * kn_h, axis=0, keepdims=True) + ssr[i]
                rn.append(jnp.broadcast_to(lax.rsqrt(ss * (1.0 / MLA_QK) + NORM_EPS),
                                           (SAMPLE_T_PAD, ss.shape[1])))
            out.append((big[i][nkn:] + rope[i]) * jnp.concatenate(rn, axis=0))
        return out

    def softmax_update(s_list, cb_list):
        m_prev = m_sc[...]
        m_new = m_prev
        for s in s_list:
            m_new = jnp.maximum(m_new, jnp.max(s, axis=-1, keepdims=True))
        corr = jnp.exp(m_prev - m_new)
        l_new = l_sc[...] * corr
        acc = acc_sc[...] * corr
        for s, cb in zip(s_list, cb_list):
            pr = jnp.exp(s - m_new)
            l_new = l_new + jnp.sum(pr, axis=-1, keepdims=True)
            acc = acc + _dot(pr.astype(BF16), cb)
        l_sc[...] = l_new
        acc_sc[...] = acc
        m_sc[...] = m_new

    grp = 2 if npp % 2 == 0 else 1
    rope_pad = jnp.zeros((LANES - MLA_ROPE, grp * PAGE_SIZE), F32)
    cbs, krs = [], []
    for i in range(0, npp, grp):
        cbs.append(jnp.concatenate([lat_refs[i + j][...] for j in range(grp)], axis=0).astype(BF16))
        krs.append(jnp.concatenate([jnp.concatenate([rope_refs[i + j][...] for j in range(grp)], axis=1),
                                    rope_pad], axis=0))
    softmax_update(scores(cbs, krs), cbs)

    @pl.when(step == pl.num_programs(1) - 1)
    def _():
        cb = newc_ref[...].astype(BF16)
        s = scores([cb], [newr_ref[...]])[0]
        key = lax.broadcasted_iota(jnp.int32, s.shape, 1)
        qry = lax.broadcasted_iota(jnp.int32, s.shape, 0) % SAMPLE_T_PAD
        softmax_update([jnp.where((key <= qry) & (key < t_new), s, NEG_BIG)], [cb])
        o_lat = (acc_sc[...] / l_sc[...]).astype(BF16)
        wvb = wvb_ref[...]
        for h in range(MLA_HEADS):
            o_ref[:, h * MLA_V:(h + 1) * MLA_V] = _dot(
                o_lat[h * SAMPLE_T_PAD:(h + 1) * SAMPLE_T_PAD],
                wvb[:, h * MLA_V:(h + 1) * MLA_V])


def _paged_attention(page_table, q, lat_pool, rope_pool_t, layer, new_c, new_r_t, p, t_new):
    bsz, n_pages = page_table.shape
    npp = math.gcd(PAGES_PER_STEP, n_pages)
    nsteps = n_pages // npp
    hq = MLA_HEADS * MLA_QK_PAD
    hv = MLA_HEADS * MLA_V

    def lat_spec(i):
        return pl.BlockSpec((None, None, PAGE_SIZE, MLA_KV_RANK),
                            lambda b, s, pt: (layer, pt[b, s * npp + i], 0, 0))

    def rope_spec(i):
        return pl.BlockSpec((None, None, MLA_ROPE, PAGE_SIZE),
                            lambda b, s, pt: (layer, pt[b, s * npp + i], 0, 0))

    grid_spec = pltpu.PrefetchScalarGridSpec(
        num_scalar_prefetch=1,
        grid=(bsz, nsteps),
        in_specs=[pl.BlockSpec((SAMPLE_T_PAD, hq), lambda b, s, pt: (b, 0))]
                 + [lat_spec(i) for i in range(npp)] + [rope_spec(i) for i in range(npp)]
                 + [pl.BlockSpec((None, PAGE_SIZE, MLA_KV_RANK), lambda b, s, pt: (b, 0, 0)),
                    pl.BlockSpec((None, LANES, PAGE_SIZE), lambda b, s, pt: (b, 0, 0)),
                    pl.BlockSpec((hv, MLA_KV_RANK), lambda b, s, pt: (0, 0)),
                    pl.BlockSpec((MLA_KV_RANK, hv), lambda b, s, pt: (0, 0)),
                    pl.BlockSpec((1, LANES), lambda b, s, pt: (0, 0)),
                    pl.BlockSpec((1, LANES), lambda b, s, pt: (0, 0))],
        out_specs=pl.BlockSpec((SAMPLE_T_PAD, hv), lambda b, s, pt: (b, 0)),
        scratch_shapes=[pltpu.VMEM((PAGED_ROWS, 1), F32),
                        pltpu.VMEM((PAGED_ROWS, 1), F32),
                        pltpu.VMEM((PAGED_ROWS, MLA_KV_RANK), F32),
                        pltpu.VMEM((hv + PAGED_ROWS, MLA_KV_RANK), BF16),
                        pltpu.VMEM((PAGED_ROWS, LANES), BF16)],
    )
    kern = functools.partial(_paged_kernel, t_new=t_new, npp=npp)
    return pl.pallas_call(
        kern,
        grid_spec=grid_spec,
        out_shape=jax.ShapeDtypeStruct((bsz * SAMPLE_T_PAD, hv), F32),
        compiler_params=_cparams("parallel", "arbitrary"),
    )(page_table, q, *([lat_pool] * npp), *([rope_pool_t] * npp), new_c, new_r_t,
      p["w_kb_t"], p["w_vb"], p["gk_nope"], p["gk_rope"])


def _merge_kernel(x_ref, g_ref, wg_ref, bg_ref, os_ref, or_ref, om_ref, ws_ref, wr_ref, wm_ref, wo_ref, o_ref):
    h = (_rms(x_ref[...]) * g_ref[...]).astype(BF16)
    gl = _dot(h, wg_ref[...]) + bg_ref[...]
    d = D_MODEL
    merged = (_sigmoid(gl[:, :d]) * _dot(os_ref[...], ws_ref[...])
              + _sigmoid(gl[:, d:2 * d]) * _dot(or_ref[...], wr_ref[...])
              + _sigmoid(gl[:, 2 * d:]) * _dot(om_ref[...].astype(BF16), wm_ref[...]))
    o_ref[...] = x_ref[...] + _dot(merged.astype(BF16), wo_ref[...])


def _merge(x, o_ssd, o_rwkv, o_mla, p, tm):
    n, d = x.shape
    row = pl.BlockSpec((tm, d), lambda i: (i, 0))
    return pl.pallas_call(
        _merge_kernel,
        grid=(n // tm,),
        in_specs=[row, _full((1, d)), _full((d, N_BRANCH * d)), _full((1, N_BRANCH * d)), row, row, row]
                 + [_full((d, d))] * 4,
        out_specs=row,
        out_shape=jax.ShapeDtypeStruct((n, d), F32),
        compiler_params=_cparams("parallel"),
    )(x, p["norm_mix_g"], p["w_gate"], p["b_gate"], o_ssd, o_rwkv, o_mla,
      p["w_o_ssd"], p["w_o_rwkv"], p["w_o_mla"], p["w_out"])


def _ffn_kernel(x_ref, tail0_ref, g_ref, wup_ref, cw_ref, cb_ref, wdn_ref, o_ref, tail_ref, buf,
                *, tm, t_valid, nb):
    j = pl.program_id(1)

    @pl.when(j == 0)
    def _():
        buf[:, 0:SUBLANES, :] = tail0_ref[...]

    x = x_ref[...]
    h2 = (_rms(x) * g_ref[...]).astype(BF16)
    up = _dot(h2, wup_ref[...])
    acts = []
    for i in range(nb):
        buf[i, SUBLANES:SUBLANES + tm, :] = up[i * tm:(i + 1) * tm]
        conv = cb_ref[...]
        for c in range(FFN_CONV):
            conv = conv + buf[i, pl.ds(SUBLANES - (FFN_CONV - 1) + c, tm), :] * cw_ref[c:c + 1, :]
        tail = buf[i, pl.ds(t_valid, SUBLANES), :]
        tail_ref[i] = tail
        buf[i, 0:SUBLANES, :] = tail
        acts.append(_silu(conv[:, :D_FF]) * conv[:, D_FF:])
    act = (jnp.concatenate(acts, axis=0) if nb > 1 else acts[0]).astype(BF16)
    o_ref[...] = x + _dot(act, wdn_ref[...])


def _ffn(x, tail0, p, tm, t_valid, nb):
    bsz = tail0.shape[0]
    n, d = x.shape
    nt = n // bsz // tm
    assert nb == 1 or nt == 1
    kern = functools.partial(_ffn_kernel, tm=tm, t_valid=t_valid, nb=nb)
    return pl.pallas_call(
        kern,
        grid=(bsz // nb, nt),
        in_specs=[pl.BlockSpec((nb * tm, d), lambda b, j: (b * nt + j, 0)),
                  pl.BlockSpec((nb, SUBLANES, 2 * D_FF), lambda b, j: (b, 0, 0)),
                  _full((1, d)),
                  pl.BlockSpec((d, 2 * D_FF), lambda b, j: (0, 0), pipeline_mode=pl.Buffered(1)),
                  _full((FFN_CONV, 2 * D_FF)), _full((1, 2 * D_FF)),
                  pl.BlockSpec((D_FF, d), lambda b, j: (0, 0), pipeline_mode=pl.Buffered(1))],
        out_specs=[pl.BlockSpec((nb * tm, d), lambda b, j: (b * nt + j, 0)),
                   pl.BlockSpec((nb, SUBLANES, 2 * D_FF), lambda b, j: (b, 0, 0))],
        out_shape=[jax.ShapeDtypeStruct((n, d), F32),
                   jax.ShapeDtypeStruct((bsz, SUBLANES, 2 * D_FF), F32)],
        scratch_shapes=[pltpu.VMEM((nb, SUBLANES + tm, 2 * D_FF), F32)],
        compiler_params=_cparams("parallel", "arbitrary"),
    )(x, tail0, p["norm_ffn_g"], p["ffn_w_up"], p["ffn_conv_w"], p["ffn_conv_b"], p["ffn_w_down"])


def _pad_cols(w, width):
    return jnp.pad(w, ((0, 0), (0, width - w.shape[1])))


def _row(v):
    return v.reshape(1, -1)


def _prep_layer(i, w):
    d = RWKV_DIM
    w_in = w["w_in"][i]
    o_r = SSD_IN
    o_m = SSD_IN + RWKV_IN
    p = {}
    p["norm_mix_g"] = _row(w["norm_mix_g"][i])
    p["w_ssd"] = _pad_cols(w_in[:, :o_r], SSD_IN_PAD).astype(BF16)
    p["w_rwkv"] = w_in[:, o_r:o_m].astype(BF16)
    p["w_mla"] = _pad_cols(w_in[:, o_m:], MLA_IN_PAD).astype(BF16)
    p["w_gate"] = w["w_gate"][i].astype(BF16)
    p["b_gate"] = _row(w["b_gate"][i])
    p["ssd_conv_w"] = w["ssd_conv_w"][i]
    p["ssd_conv_b"] = _row(w["ssd_conv_b"][i])
    p["ssd_dt_bias"] = _pad_cols(_row(w["ssd_dt_bias"][i]), LANES)
    p["ssd_a_log"] = _pad_cols(_row(w["ssd_a_log"][i]), LANES)
    p["ssd_d"] = _row(jnp.repeat(w["ssd_d"][i], SSD_HEAD_DIM))
    p["ssd_norm_g"] = _row(w["ssd_norm_g"][i])
    p["rwkv_shift_mu"] = _row(w["rwkv_shift_mu"][i])
    p["rwkv_w0"] = _row(w["rwkv_w0"][i])
    zeros_lora = jnp.zeros((RWKV_DECAY_LORA, d), F32)
    p["rwkv_w_up"] = jnp.concatenate([w["rwkv_w_up"][i], zeros_lora], axis=0).astype(BF16)
    p["rwkv_a0"] = _row(w["rwkv_a0"][i])
    p["rwkv_a_up"] = jnp.concatenate([zeros_lora, w["rwkv_a_up"][i]], axis=0).astype(BF16)
    p["rwkv_g_up"] = w["rwkv_g_up"][i].astype(BF16)
    p["rwkv_k_k"] = _row(w["rwkv_k_k"][i])
    p["rwkv_k_a"] = _row(w["rwkv_k_a"][i])
    p["rwkv_r_k"] = _row(w["rwkv_r_k"][i])
    p["rwkv_ln_g"] = _row(w["rwkv_ln_g"][i])
    p["rwkv_ln_b"] = _row(w["rwkv_ln_b"][i])
    wq = w["mla_w_q_b"][i]
    hv = MLA_HEADS * MLA_NOPE
    p["mla_q_a_g"] = _row(w["mla_q_a_g"][i])
    p["wq_nope"] = wq[:, :, :MLA_NOPE].reshape(MLA_Q_RANK, hv).astype(BF16)
    p["wq_rope"] = jnp.pad(wq[:, :, MLA_NOPE:], ((0, 0), (0, 0), (0, MLA_NOPE - MLA_ROPE))
                           ).reshape(MLA_Q_RANK, hv).astype(BF16)
    p["mla_kv_a_g"] = _row(w["mla_kv_a_g"][i])
    p["w_kb"] = w["mla_w_kb"][i].reshape(MLA_KV_RANK, hv).astype(BF16)
    p["w_kb_t"] = p["w_kb"].T
    p["w_vb"] = w["mla_w_vb"][i].reshape(MLA_KV_RANK, MLA_HEADS * MLA_V).astype(BF16)
    gq, gk = w["mla_q_norm_g"][i], w["mla_k_norm_g"][i]
    p["gq_nope"] = _row(gq[:MLA_NOPE])
    p["gq_rope"] = _pad_cols(_row(gq[MLA_NOPE:]), LANES)
    p["gk_nope"] = _row(gk[:MLA_NOPE])
    p["gk_rope"] = _pad_cols(_row(gk[MLA_NOPE:]), LANES)
    for name in ("w_o_ssd", "w_o_rwkv", "w_o_mla", "w_out"):
        p[name] = w[name][i].astype(BF16)
    p["norm_ffn_g"] = _row(w["norm_ffn_g"][i])
    p["ffn_w_up"] = w["ffn_w_up"][i].astype(BF16)
    p["ffn_conv_w"] = w["ffn_conv_w"][i]
    p["ffn_conv_b"] = _row(w["ffn_conv_b"][i])
    p["ffn_w_down"] = w["ffn_w_down"][i].astype(BF16)
    return p


def _constants():
    ch = np.arange(RWKV_DIM)
    seg = (ch[:, None] // RWKV_HEAD_DIM == np.arange(LANES)[None, :]).astype(np.float32)
    seg2 = np.concatenate([seg, seg], axis=0)
    seg_t2 = np.concatenate([seg.T, seg.T], axis=0)
    return jnp.asarray(seg2, BF16), jnp.asarray(seg_t2, BF16)


def _rope_tables(pos):
    half = MLA_ROPE // 2
    inv = ROPE_BASE ** (-jnp.arange(half, dtype=F32) / half)
    ang = pos.astype(F32)[:, None] * inv[None, :]
    cos, sin = jnp.cos(ang), jnp.sin(ang)
    z = jnp.zeros_like(cos)
    z2 = jnp.zeros((pos.shape[0], LANES - MLA_ROPE), F32)
    return (jnp.concatenate([cos, cos, z2], axis=1),
            jnp.concatenate([-sin, z, z2], axis=1),
            jnp.concatenate([z, sin, z2], axis=1))


def _tail_block(state, rows):
    return jnp.pad(state, ((0, 0), (SUBLANES - rows, 0), (0, 0)))


def _pack_rwkv_state(s):
    b = s.shape[0]
    s5 = s.reshape(b, RWKV_PAIRS, 2, RWKV_HEAD_DIM, RWKV_HEAD_DIM)
    z = jnp.zeros_like(s5[:, :, 0])
    top = jnp.concatenate([s5[:, :, 0], z], axis=-1)
    bot = jnp.concatenate([z, s5[:, :, 1]], axis=-1)
    return jnp.concatenate([top, bot], axis=-2)


def _unpack_rwkv_state(s):
    b = s.shape[0]
    hd = RWKV_HEAD_DIM
    return jnp.stack([s[:, :, :hd, :hd], s[:, :, hd:, hd:]], axis=2).reshape(b, RWKV_HEADS, hd, hd)


def _layer(x, st, p, cfg, attend):
    bsz, t, t_valid, tm = cfg["bsz"], cfg["t"], cfg["t_valid"], cfg["tm"]
    n = x.shape[0]
    tmn = min(tm, n)

    if t % SSD_CHUNK:
        lc = SSD_SHORT_CHUNK
        x_pad = jnp.pad(x.reshape(bsz, t, D_MODEL), ((0, 0), (0, lc - t), (0, 0)))
        o_ssd, ssd_tail, ssd_h = _ssd(x_pad.reshape(bsz * lc, D_MODEL), st["ssd_tail"], st["ssd_h"], p, t_valid, lc,
                                      math.gcd(bsz, 8))
        o_ssd = o_ssd.reshape(bsz, lc, SSD_INNER)[:, :t].reshape(n, SSD_INNER)
    else:
        o_ssd, ssd_tail, ssd_h = _ssd(x, st["ssd_tail"], st["ssd_h"], p, SSD_CHUNK, SSD_CHUNK, 1)

    tm_r = min(tm, t)
    nb_r = math.gcd(bsz, 16) if t <= SAMPLE_T_PAD else 1
    r, lw, k2, v, al, be, gg, rwkv_tail = _rwkv_pre(x, st["rwkv_tail"], p, tm_r, min(t_valid, tm_r), nb_r)
    d = RWKV_DIM
    if t % RWKV_CHUNK:
        def chunk_pad(a_):
            a_ = a_.reshape(bsz, t, d)[:, :t_valid]
            return jnp.pad(a_, ((0, 0), (0, RWKV_CHUNK - t_valid), (0, 0))).reshape(bsz * RWKV_CHUNK, d)
        yy, rwkv_s = _rwkv_scan(*(chunk_pad(a_) for a_ in (r, lw, k2, v, al, be)), st["rwkv_s"], bsz, RWKV_CHUNK,
                                t_valid <= RWKV_SOLVE_BLOCK)
        yy = yy.reshape(bsz, RWKV_CHUNK, d)[:, :t].reshape(n, d)
    else:
        yy, rwkv_s = _rwkv_scan(r, lw, k2, v, al, be, st["rwkv_s"], bsz, min(t, cfg["tc_rwkv"]), False)
    o_rwkv = _rwkv_post(yy, r, k2, v, gg, p, tmn)

    q, k, vv, ckv, kpe = _mla_prep(x, cfg["rope"], p, tmn, cfg["q_dtype"])
    o_mla = attend(q, k, vv, ckv, kpe)

    x = _merge(x, o_ssd, o_rwkv, o_mla, p, tmn)
    tm_f = min(cfg["tm_ffn"], t)
    x, ffn_tail = _ffn(x, st["ffn_tail"], p, tm_f, min(t_valid, tm_f), cfg["nb_ffn"])
    new_st = dict(ssd_tail=ssd_tail, ssd_h=ssd_h, rwkv_tail=rwkv_tail, rwkv_s=rwkv_s, ffn_tail=ffn_tail)
    return x, new_st, ckv, kpe


def kernel(x_prompt, x_sample, cache_kv_latent, cache_k_rope, page_table, state_ssm, state_ssm_conv,
           state_rwkv, state_rwkv_shift, state_ffn_conv, norm_mix_g, w_in, ssd_conv_w, ssd_conv_b,
           ssd_dt_bias, ssd_a_log, ssd_d, ssd_norm_g, rwkv_shift_mu, rwkv_w0, rwkv_w_up, rwkv_a0,
           rwkv_a_up, rwkv_g_up, rwkv_k_k, rwkv_k_a, rwkv_r_k, rwkv_ln_g, rwkv_ln_b, mla_q_a_g,
           mla_w_q_b, mla_kv_a_g, mla_w_kb, mla_w_vb, mla_q_norm_g, mla_k_norm_g, w_gate, b_gate,
           w_o_ssd, w_o_rwkv, w_o_mla, w_out, norm_ffn_g, ffn_w_up, ffn_conv_w, ffn_conv_b, ffn_w_down):
    w = dict(norm_mix_g=norm_mix_g, w_in=w_in, ssd_conv_w=ssd_conv_w, ssd_conv_b=ssd_conv_b,
             ssd_dt_bias=ssd_dt_bias, ssd_a_log=ssd_a_log, ssd_d=ssd_d, ssd_norm_g=ssd_norm_g,
             rwkv_shift_mu=rwkv_shift_mu, rwkv_w0=rwkv_w0, rwkv_w_up=rwkv_w_up, rwkv_a0=rwkv_a0,
             rwkv_a_up=rwkv_a_up, rwkv_g_up=rwkv_g_up, rwkv_k_k=rwkv_k_k, rwkv_k_a=rwkv_k_a,
             rwkv_r_k=rwkv_r_k, rwkv_ln_g=rwkv_ln_g, rwkv_ln_b=rwkv_ln_b, mla_q_a_g=mla_q_a_g,
             mla_w_q_b=mla_w_q_b, mla_kv_a_g=mla_kv_a_g, mla_w_kb=mla_w_kb, mla_w_vb=mla_w_vb,
             mla_q_norm_g=mla_q_norm_g, mla_k_norm_g=mla_k_norm_g, w_gate=w_gate, b_gate=b_gate,
             w_o_ssd=w_o_ssd, w_o_rwkv=w_o_rwkv, w_o_mla=w_o_mla, w_out=w_out, norm_ffn_g=norm_ffn_g,
             ffn_w_up=ffn_w_up, ffn_conv_w=ffn_conv_w, ffn_conv_b=ffn_conv_b, ffn_w_down=ffn_w_down)
    depth = w_in.shape[0]
    bp, tp, d = x_prompt.shape
    bs, ts, _ = x_sample.shape
    tsp = SAMPLE_T_PAD
    past_len = page_table.shape[1] * PAGE_SIZE
    seg2, seg_t2 = _constants()
    rope_pool_t = jnp.swapaxes(cache_k_rope, 2, 3)

    tm_p = min(256, tp)
    cfg_p = dict(bsz=bp, t=tp, t_valid=tp, tm=tm_p, tm_ffn=min(256, tp), nb_ffn=1, tc_rwkv=256,
                 rope=_rope_tables(jnp.arange(tp)), q_dtype=BF16)
    pos_s = past_len + jnp.arange(tsp)
    rope_s = tuple(jnp.tile(tb, (bs, 1)) for tb in _rope_tables(pos_s))
    cfg_s = dict(bsz=bs, t=tsp, t_valid=ts, tm=bs * tsp, tm_ffn=tsp, nb_ffn=math.gcd(bs, 16), tc_rwkv=RWKV_CHUNK, rope=rope_s,
                 q_dtype=F32)

    xp = x_prompt.reshape(bp * tp, d)
    xs = jnp.pad(x_sample, ((0, 0), (0, tsp - ts), (0, 0))).reshape(bs * tsp, d)

    zero_st = dict(ssd_tail=jnp.zeros((bp, SUBLANES, SSD_CONV_DIM), F32),
                   ssd_h=jnp.zeros((bp, SSD_HEADS // 2, LANES, SSD_STATE), F32),
                   rwkv_tail=jnp.zeros((bp, SUBLANES, RWKV_IN), F32),
                   rwkv_s=jnp.zeros((bp, RWKV_PAIRS, LANES, LANES), F32),
                   ffn_tail=jnp.zeros((bp, SUBLANES, 2 * D_FF), F32))

    new_p = [[] for _ in range(7)]
    new_s = [[] for _ in range(7)]
    for i in range(depth):
        p = _prep_layer(i, w)
        p["seg2"], p["seg_t2"] = seg2, seg_t2

        def prompt_attend(q, k, v, ckv, kpe):
            return _flash(q, k, v, bp, min(512, tp))

        xp, st_p, ckv_p, kpe_p = _layer(xp, zero_st, p, cfg_p, prompt_attend)

        st_in = dict(ssd_tail=_tail_block(state_ssm_conv[i], SSD_CONV - 1),
                     ssd_h=state_ssm[i].reshape(bs, SSD_HEADS // 2, LANES, SSD_STATE),
                     rwkv_tail=_tail_block(state_rwkv_shift[i][:, None, :], 1),
                     rwkv_s=_pack_rwkv_state(state_rwkv[i]),
                     ffn_tail=_tail_block(state_ffn_conv[i], FFN_CONV - 1))

        def sample_attend(q, k, v, ckv, kpe, i=i, p=p):
            new_c = jnp.pad(ckv.reshape(bs, tsp, MLA_KV_RANK), ((0, 0), (0, PAGE_SIZE - tsp), (0, 0)))
            new_r_t = jnp.pad(jnp.swapaxes(kpe.reshape(bs, tsp, LANES), 1, 2), ((0, 0), (0, 0), (0, PAGE_SIZE - tsp)))
            return _paged_attention(page_table, q, cache_kv_latent, rope_pool_t, i, new_c, new_r_t, p, ts)

        xs, st_s, ckv_s, kpe_s = _layer(xs, st_in, p, cfg_s, sample_attend)

        for lst, st, ckv, kpe, b_, t_, tv in ((new_p, st_p, ckv_p, kpe_p, bp, tp, tp),
                                              (new_s, st_s, ckv_s, kpe_s, bs, tsp, ts)):
            lst[0].append(ckv.reshape(b_, t_, MLA_KV_RANK)[:, :tv])
            lst[1].append(kpe.reshape(b_, t_, LANES)[:, :tv, :MLA_ROPE])
            lst[2].append(st["ssd_h"].reshape(b_, SSD_HEADS, SSD_HEAD_DIM, SSD_STATE))
            lst[3].append(st["ssd_tail"][:, SUBLANES - (SSD_CONV - 1):])
            lst[4].append(_unpack_rwkv_state(st["rwkv_s"]))
            lst[5].append(st["rwkv_tail"][:, SUBLANES - 1])
            lst[6].append(st["ffn_tail"][:, SUBLANES - (FFN_CONV - 1):])

    outs_p = [jnp.stack(v_, axis=0) for v_ in new_p]
    outs_s = [jnp.stack(v_, axis=0) for v_ in new_s]
    y_p = xp.reshape(bp, tp, d)
    y_s = xs.reshape(bs, tsp, d)[:, :ts]
    return (y_p, y_s, *outs_p, *outs_s)
```

```python
import functools
import math

import numpy as np
import jax
import jax.numpy as jnp
from jax import lax
from jax.experimental import pallas as pl
from jax.experimental.pallas import tpu as pltpu

F32 = jnp.float32
BF16 = jnp.bfloat16

D_MODEL = 1024
PAGE_SIZE = 128

SSD_HEAD_DIM = 64
SSD_INNER = 1024
SSD_HEADS = 16
SSD_GROUPS = 2
SSD_STATE = 128
SSD_CONV = 4
SSD_CHUNK = 128
SSD_SHORT_CHUNK = 16
SSD_CONV_DIM = SSD_INNER + 2 * SSD_GROUPS * SSD_STATE
SSD_IN = SSD_INNER + SSD_CONV_DIM + SSD_HEADS
SSD_IN_PAD = SSD_INNER + SSD_CONV_DIM + 128

RWKV_HEAD_DIM = 64
RWKV_DIM = 1024
RWKV_HEADS = 16
RWKV_DECAY_LORA = 64
RWKV_A_LORA = 64
RWKV_GATE_LORA = 128
RWKV_IN = 3 * RWKV_DIM + RWKV_DECAY_LORA + RWKV_A_LORA + RWKV_GATE_LORA
RWKV_LN_EPS = 64e-5

MLA_HEADS = 8
MLA_NOPE = 128
MLA_ROPE = 64
MLA_QK = MLA_NOPE + MLA_ROPE
MLA_V = 128
MLA_Q_RANK = 512
MLA_KV_RANK = 256
MLA_IN = MLA_Q_RANK + MLA_KV_RANK + MLA_ROPE
MLA_IN_PAD = MLA_Q_RANK + MLA_KV_RANK + 128
MLA_QK_PAD = 256
ROPE_BASE = 10000.0
MLA_SCALE = MLA_QK ** -0.5

N_BRANCH = 3
D_FF = 2816
FFN_CONV = 3
NORM_EPS = 1e-6

SUBLANES = 8
LANES = 128
SAMPLE_T_PAD = 8
NEG_BIG = -1e30

V7X_VMEM_BYTES = 64 * 1024 * 1024
VMEM_LIMIT = V7X_VMEM_BYTES // 8 * 7


def _cparams(*sem):
    return pltpu.CompilerParams(dimension_semantics=sem, vmem_limit_bytes=VMEM_LIMIT)


def _full(shape):
    nd = len(shape)
    return pl.BlockSpec(shape, lambda *_: (0,) * nd)


def _rms(x, eps=NORM_EPS):
    return x * lax.rsqrt(jnp.mean(x * x, axis=-1, keepdims=True) + eps)


def _sigmoid(x):
    return 1.0 / (1.0 + jnp.exp(-x))


def _silu(x):
    return x * _sigmoid(x)


def _softplus(x):
    return jnp.maximum(x, 0.0) + jnp.log(1.0 + jnp.exp(-jnp.abs(x)))


def _dot(a, b):
    return jnp.dot(a, b, preferred_element_type=F32)


def _dot_nt(a, b):
    return lax.dot_general(a, b, (((1,), (1,)), ((), ())), preferred_element_type=F32)


def _dot_tn(a, b):
    return lax.dot_general(a, b, (((0,), (0,)), ((), ())), preferred_element_type=F32)


def _hi_lo(x):
    hi = x.astype(BF16)
    return hi, (x - hi.astype(F32)).astype(BF16)


def _split_hi_lo(x):
    return jnp.concatenate(_hi_lo(x), axis=1)


def _in_proj(x_ref, g_ref, w_ref):
    return _dot((_rms(x_ref[...]) * g_ref[...]).astype(BF16), w_ref[...])


def _resident(shape):
    nd = len(shape)
    return pl.BlockSpec(shape, lambda *_: (0,) * nd, pipeline_mode=pl.Buffered(1))


def _ssd_kernel(x_ref, g_ref, w_ref, tail0_ref, h0_ref, cw_ref, cb_ref, dtb_ref, alog_ref, dsk_ref, ng_ref,
                o_ref, tail_ref, h_ref, buf, *, t_valid, L, nb):
    c = pl.program_id(1)

    @pl.when(c == 0)
    def _():
        buf[:, 0:SUBLANES, :] = tail0_ref[...]
        h_ref[...] = h0_ref[...]

    u_all = _in_proj(x_ref, g_ref, w_ref)
    for i in range(nb):
        rows = slice(i * L, (i + 1) * L)
        o_ref[rows, :] = _ssd_chunk(u_all[rows], i, cw_ref, cb_ref, dtb_ref, alog_ref, dsk_ref, ng_ref,
                                    tail_ref, h_ref, buf, t_valid, L)


def _ssd_chunk(u, i, cw_ref, cb_ref, dtb_ref, alog_ref, dsk_ref, ng_ref, tail_ref, h_ref, buf, t_valid, L):
    z = u[:, :SSD_INNER]
    dt_raw = u[:, SSD_INNER + SSD_CONV_DIM:]
    buf[i, SUBLANES:SUBLANES + L, :] = u[:, SSD_INNER:SSD_INNER + SSD_CONV_DIM]
    acc = cb_ref[...]
    for j in range(SSD_CONV):
        acc = acc + buf[i, pl.ds(SUBLANES - (SSD_CONV - 1) + j, L), :] * cw_ref[j:j + 1, :]
    tail = buf[i, pl.ds(t_valid, SUBLANES), :]
    tail_ref[i] = tail
    buf[i, 0:SUBLANES, :] = tail
    xbc = _silu(acc)
    xs = xbc[:, :SSD_INNER]

    dt = _softplus(dt_raw + dtb_ref[...])
    if t_valid < L:
        row = lax.broadcasted_iota(jnp.int32, dt.shape, 0)
        dt = jnp.where(row < t_valid, dt, 0.0)
    a = -jnp.exp(alog_ref[...])
    da = dt * a
    ri = lax.broadcasted_iota(jnp.int32, (L, L), 0)
    ci = lax.broadcasted_iota(jnp.int32, (L, L), 1)
    causal = ri >= ci
    tri = jnp.where(causal, 1.0, 0.0).astype(F32)
    cs = jnp.dot(tri, da, preferred_element_type=F32, precision=lax.Precision.HIGHEST)
    cs_t = cs.T
    total = cs[L - 1:L, :]
    etot = jnp.exp(total)
    lane = lax.broadcasted_iota(jnp.int32, (L, LANES), 1)
    lo_half = lane < SSD_HEAD_DIM
    row_lo = lax.broadcasted_iota(jnp.int32, (LANES, LANES), 0) < SSD_HEAD_DIM

    def pair_cols(m, j0):
        return jnp.where(lo_half, m[:, j0:j0 + 1], m[:, j0 + 1:j0 + 2])

    hpg = SSD_HEADS // SSD_GROUPS
    y_parts = []
    for g in range(SSD_GROUPS):
        b_g = xbc[:, SSD_INNER + g * SSD_STATE:SSD_INNER + (g + 1) * SSD_STATE].astype(BF16)
        c0 = SSD_INNER + SSD_GROUPS * SSD_STATE + g * SSD_STATE
        c_g = xbc[:, c0:c0 + SSD_STATE].astype(BF16)
        cb = _dot_nt(c_g, b_g)
        for m in range(hpg // 2):
            q = g * (hpg // 2) + m
            j0 = 2 * q
            xs_p = xs[:, q * LANES:(q + 1) * LANES]
            xd = xs_p * pair_cols(dt, j0)
            xd_b = xd.astype(BF16)
            ys = []
            for e in range(2):
                j = j0 + e
                diff = cs[:, j:j + 1] - cs_t[j:j + 1, :]
                lm = jnp.where(causal, jnp.exp(jnp.where(causal, diff, 0.0)), 0.0)
                ys.append(_dot((cb * lm).astype(BF16), xd_b))
            y_diag = jnp.where(lo_half, ys[0], ys[1])
            hp = h_ref[i, q]
            cs_p = pair_cols(cs, j0)
            y_off = _dot_nt(c_g, hp.astype(BF16)) * jnp.exp(cs_p)
            tot_p = jnp.where(lo_half[0:1, :], total[:, j0:j0 + 1], total[:, j0 + 1:j0 + 2])
            dte = jnp.exp(tot_p - cs_p)
            contrib = _dot_tn((xd * dte).astype(BF16), b_g)
            decay = jnp.where(row_lo, etot[:, j0:j0 + 1], etot[:, j0 + 1:j0 + 2])
            h_ref[i, q] = hp * decay + contrib
            y_parts.append(y_diag + y_off + dsk_ref[:, q * LANES:(q + 1) * LANES] * xs_p)
    y = jnp.concatenate(y_parts, axis=1) * _silu(z)
    gw = SSD_INNER // SSD_GROUPS
    outs = [_rms(y[:, g * gw:(g + 1) * gw]) * ng_ref[:, g * gw:(g + 1) * gw] for g in range(SSD_GROUPS)]
    return jnp.concatenate(outs, axis=1).astype(BF16)


def _ssd(x, tail0, h0, p, t_valid, L, nb):
    bsz = h0.shape[0]
    n, d = x.shape
    nch = n // bsz // L
    assert nb == 1 or nch == 1
    kern = functools.partial(_ssd_kernel, t_valid=t_valid, L=L, nb=nb)
    return pl.pallas_call(
        kern,
        grid=(bsz // nb, nch),
        in_specs=[pl.BlockSpec((nb * L, d), lambda b, c: (b * nch + c, 0)), _full((1, d)),
                  _resident((d, SSD_IN_PAD)),
                  pl.BlockSpec((nb, SUBLANES, SSD_CONV_DIM), lambda b, c: (b, 0, 0)),
                  pl.BlockSpec((nb, SSD_HEADS // 2, LANES, SSD_STATE), lambda b, c: (b, 0, 0, 0)),
                  _full((SSD_CONV, SSD_CONV_DIM)), _full((1, SSD_CONV_DIM)),
                  _full((1, LANES)), _full((1, LANES)), _full((1, SSD_INNER)), _full((1, SSD_INNER))],
        out_specs=[pl.BlockSpec((nb * L, SSD_INNER), lambda b, c: (b * nch + c, 0)),
                   pl.BlockSpec((nb, SUBLANES, SSD_CONV_DIM), lambda b, c: (b, 0, 0)),
                   pl.BlockSpec((nb, SSD_HEADS // 2, LANES, SSD_STATE), lambda b, c: (b, 0, 0, 0))],
        out_shape=[jax.ShapeDtypeStruct((n, SSD_INNER), BF16),
                   jax.ShapeDtypeStruct((bsz, SUBLANES, SSD_CONV_DIM), F32),
                   jax.ShapeDtypeStruct((bsz, SSD_HEADS // 2, LANES, SSD_STATE), F32)],
        scratch_shapes=[pltpu.VMEM((nb, SUBLANES + L, SSD_CONV_DIM), F32)],
        compiler_params=_cparams("parallel", "arbitrary"),
    )(x, p["norm_mix_g"], p["w_ssd"], tail0, h0, p["ssd_conv_w"], p["ssd_conv_b"], p["ssd_dt_bias"],
      p["ssd_a_log"], p["ssd_d"], p["ssd_norm_g"])


def _seg_sum_bcast(x, seg2, seg_t2):
    s16 = _dot(_split_hi_lo(x), seg2)
    return _dot(_split_hi_lo(s16), seg_t2)


def _rwkv_pre_kernel(x_ref, g_in_ref, w_in_ref, tail0_ref, mu_ref, w0_ref, wup_ref, a0_ref, aup_ref, gup_ref, kk_ref,
                     ka_ref, seg2_ref, segt2_ref,
                     r_ref, w_ref, k_ref, v_ref, al_ref, be_ref, g_ref, tail_ref, buf, *, tm, t_valid, nb):
    j = pl.program_id(1)

    @pl.when(j == 0)
    def _():
        buf[:, 0:SUBLANES, :] = tail0_ref[...]

    u = _in_proj(x_ref, g_in_ref, w_in_ref)
    prevs = []
    for i in range(nb):
        buf[i, SUBLANES:SUBLANES + tm, :] = u[i * tm:(i + 1) * tm]
        prevs.append(buf[i, pl.ds(SUBLANES - 1, tm), :])
        tail = buf[i, pl.ds(t_valid, SUBLANES), :]
        tail_ref[i] = tail
        buf[i, 0:SUBLANES, :] = tail
    prev = jnp.concatenate(prevs, axis=0) if nb > 1 else prevs[0]

    d = RWKV_DIM
    f = u + mu_ref[...] * (prev - u)
    r, k, v = f[:, :d], f[:, d:2 * d], f[:, 2 * d:3 * d]
    lo = f[:, 3 * d:3 * d + LANES]
    glo = f[:, 3 * d + LANES:]
    ww = w0_ref[...] + _dot(jnp.tanh(lo).astype(BF16), wup_ref[...])
    w_log = -_softplus(-ww) - 0.5
    a = _sigmoid(a0_ref[...] + _dot(lo.astype(BF16), aup_ref[...]))
    kk = k * kk_ref[...]
    ssb = _seg_sum_bcast(kk * kk, seg2_ref[...], segt2_ref[...])
    kkn = kk / jnp.maximum(jnp.sqrt(ssb), 1e-12)
    r_ref[...] = r
    w_ref[...] = -jnp.exp(w_log)
    k_ref[...] = k * (1.0 + (a - 1.0) * ka_ref[...])
    v_ref[...] = v
    al_ref[...] = -kkn
    be_ref[...] = kkn * a
    g_ref[...] = _dot(_sigmoid(glo).astype(BF16), gup_ref[...])


def _rwkv_pre(x, tail0, p, tm, t_valid, nb):
    bsz = tail0.shape[0]
    n = x.shape[0]
    nt = n // bsz // tm
    assert nb == 1 or nt == 1
    d = RWKV_DIM
    kern = functools.partial(_rwkv_pre_kernel, tm=tm, t_valid=t_valid, nb=nb)
    row = pl.BlockSpec((nb * tm, d), lambda b, j: (b * nt + j, 0))
    return pl.pallas_call(
        kern,
        grid=(bsz // nb, nt),
        in_specs=[row, _full((1, D_MODEL)), _resident((D_MODEL, RWKV_IN)),
                  pl.BlockSpec((nb, SUBLANES, RWKV_IN), lambda b, j: (b, 0, 0)),
                  _full((1, RWKV_IN)), _full((1, d)), _full((LANES, d)), _full((1, d)),
                  _full((LANES, d)), _full((LANES, d)), _full((1, d)), _full((1, d)),
                  _full((2 * d, LANES)), _full((2 * LANES, d))],
        out_specs=[row] * 7 + [pl.BlockSpec((nb, SUBLANES, RWKV_IN), lambda b, j: (b, 0, 0))],
        out_shape=[jax.ShapeDtypeStruct((n, d), F32)] * 7
                  + [jax.ShapeDtypeStruct((bsz, SUBLANES, RWKV_IN), F32)],
        scratch_shapes=[pltpu.VMEM((nb, SUBLANES + tm, RWKV_IN), F32)],
        compiler_params=_cparams("parallel", "arbitrary"),
    )(x, p["norm_mix_g"], p["w_rwkv"], tail0, p["rwkv_shift_mu"], p["rwkv_w0"], p["rwkv_w_up"], p["rwkv_a0"],
      p["rwkv_a_up"],
      p["rwkv_g_up"], p["rwkv_k_k"], p["rwkv_k_a"], p["seg2"], p["seg_t2"])


RWKV_CHUNK = 64
RWKV_SOLVE_BLOCK = 16
RWKV_PAIRS = RWKV_HEADS // 2


def _mm(a, b):
    return _dot(a.astype(BF16), b.astype(BF16))


def _mm3(a, b):
    ah, al = _hi_lo(a)
    bh, bl = _hi_lo(b)
    m = a.shape[0]
    both = _dot(jnp.concatenate([ah, al], axis=0), bh)
    return both[:m] + both[m:] + _dot(ah, bl)


def _mm2(a, b):
    ah, al = _hi_lo(a)
    m = a.shape[0]
    both = _dot(jnp.concatenate([ah, al], axis=0), b.astype(BF16))
    return both[:m] + both[m:]


def _rwkv_chunk_kernel(r_ref, lw_ref, k_ref, v_ref, al_ref, be_ref, s0_ref, y_ref, s_ref,
                       *, nchunks, single_block):
    c = pl.program_id(1)

    @pl.when(c == 0)
    def _():
        s_ref[...] = s0_ref[...]

    C, HD = RWKV_CHUNK, RWKV_HEAD_DIM
    r2 = lax.broadcasted_iota(jnp.int32, (LANES, LANES), 0)
    c2 = lax.broadcasted_iota(jnp.int32, (LANES, LANES), 1)
    eye = jnp.where(r2 == c2, 1.0, 0.0).astype(F32)
    diag_blk = (r2 // RWKV_SOLVE_BLOCK) == (c2 // RWKV_SOLVE_BLOCK)
    same_head = (r2 // HD) == (c2 // HD)
    tri = jnp.where(lax.broadcasted_iota(jnp.int32, (C, C), 0) >= lax.broadcasted_iota(jnp.int32, (C, C), 1),
                    1.0, 0.0).astype(F32)
    t_i = lax.broadcasted_iota(jnp.int32, (C, LANES), 0)
    lane = lax.broadcasted_iota(jnp.int32, (C, LANES), 1)
    lo_half = lane < HD
    hi_half = lane >= HD
    strict = (lane % HD) < t_i
    incl = (lane % HD) <= t_i
    zeros = jnp.zeros((C, LANES), F32)

    def body(ci, carry):
        rows = pl.ds(pl.multiple_of(ci * C, C), C)
        cum_all = jnp.dot(tri, lw_ref[rows, :], preferred_element_type=F32, precision=lax.Precision.HIGHEST)
        pairs = range(RWKV_PAIRS)
        sls = [slice(p * LANES, (p + 1) * LANES) for p in pairs]
        cum = [cum_all[:, sl] for sl in sls]
        be = [be_ref[rows, sl] for sl in sls]
        kk = [k_ref[rows, sl] for sl in sls]
        vv = [v_ref[rows, sl] for sl in sls]
        g_inv = [jnp.exp(-cum[p]) for p in pairs]
        ar = [jnp.concatenate([al_ref[rows, sls[p]] * jnp.exp(cum[p] - lw_ref[rows, sls[p]]),
                               r_ref[rows, sls[p]] * jnp.exp(cum[p])], axis=0) for p in pairs]
        ar_b = [ar[p].astype(BF16) for p in pairs]
        bt = [be[p] * g_inv[p] for p in pairs]
        kt = [kk[p] * g_inv[p] for p in pairs]
        x01 = [_dot_nt(ar_b[p], jnp.concatenate(
            [jnp.where(lo_half, bt[p], 0.0), jnp.where(lo_half, kt[p], 0.0),
             jnp.where(hi_half, kt[p], 0.0), jnp.where(hi_half, bt[p], 0.0)], axis=0).astype(BF16)) for p in pairs]
        x0 = [x01[p][:, :LANES] for p in pairs]
        x1 = [x01[p][:, LANES:] for p in pairs]
        nb = [jnp.concatenate([jnp.where(lo_half & strict, x0[p][:C], 0.0),
                               jnp.where(hi_half & strict, x1[p][:C], 0.0)], axis=0) for p in pairs]
        sbd = [s_ref[p] for p in pairs]

        def read_state(p):
            s_hi, s_lo = _hi_lo(sbd[p])
            return _dot_nt(jnp.concatenate([ar_b[p], ar_b[p]], axis=1),
                           jnp.concatenate([s_hi, s_lo], axis=1))

        ars = [read_state(p) for p in pairs]

        def strict_ak(p):
            lhs = jnp.concatenate([jnp.where(hi_half & strict, x0[p][:C], 0.0),
                                   jnp.where(lo_half & strict, x1[p][:C], 0.0)], axis=0)
            rhs_w = jnp.concatenate([jnp.concatenate([zeros, vv[p]], axis=0),
                                     jnp.concatenate([vv[p], zeros], axis=0)], axis=1)
            both = _mm(lhs, rhs_w)
            return jnp.where(lo_half, both[:C, :LANES], both[C:, LANES:])

        rhs = [ars[p][:C] + strict_ak(p) for p in pairs]
        rhs2 = [jnp.concatenate([rhs[p], rhs[p]], axis=0) for p in pairs]
        nd = [jnp.where(diag_blk, nb[p], 0.0) for p in pairs]
        loff = [nb[p] - nd[p] for p in pairs]
        pm = [eye + nd[p] for p in pairs]
        wide = lambda a_, b_: jnp.concatenate([a_, b_], axis=1)
        n2 = [_mm3(nd[p], nd[p]) for p in pairs]
        t2 = [_mm3(n2[p], wide(pm[p], n2[p])) for p in pairs]
        pm = [pm[p] + t2[p][:, :LANES] for p in pairs]
        t4 = [_mm3(t2[p][:, LANES:], wide(pm[p], t2[p][:, LANES:])) for p in pairs]
        pm = [pm[p] + t4[p][:, :LANES] for p in pairs]
        td = [pm[p] + _mm3(t4[p][:, LANES:], pm[p]) for p in pairs]
        if single_block:
            u2 = [_mm2(td[p], rhs2[p]) for p in pairs]
        else:
            t1 = [_mm2(td[p], wide(loff[p], rhs2[p])) for p in pairs]
            m1 = [t1[p][:, :LANES] for p in pairs]
            x_a = [t1[p][:, LANES:] for p in pairs]
            m2 = [_mm(m1[p], m1[p]) for p in pairs]
            x_b = [x_a[p] + _mm(m2[p], x_a[p]) for p in pairs]
            u2 = [x_b[p] + _mm(m1[p], x_b[p]) for p in pairs]
        up = [jnp.where(lo_half, u2[p][:C], u2[p][C:]) for p in pairs]
        for p in pairs:
            uv = jnp.concatenate([up[p], vv[p]], axis=0)
            vu = jnp.concatenate([vv[p], up[p]], axis=0)
            both = _mm(jnp.concatenate([jnp.where(incl, x0[p][C:], 0.0), jnp.where(incl, x1[p][C:], 0.0)], axis=0),
                       jnp.concatenate([uv, vu], axis=1))
            y_ref[rows, sls[p]] = ars[p][C:] + jnp.where(lo_half, both[:C, :LANES], both[C:, LANES:])
            cum_c = cum[p][C - 1:C, :]
            g_end = jnp.exp(cum_c - cum[p])
            bkh = jnp.concatenate([be[p] * g_end, kk[p] * g_end], axis=0)
            upd = _dot_tn(uv.astype(BF16), bkh.astype(BF16))
            s_ref[p] = sbd[p] * jnp.exp(cum_c) + jnp.where(same_head, upd, 0.0)
        return carry

    lax.fori_loop(0, nchunks, body, 0)


def _rwkv_scan(r, lw, k, v, al, be, s0, bsz, tc, single_block):
    n, d = r.shape
    t_total = n // bsz
    nt = t_total // tc
    kern = functools.partial(_rwkv_chunk_kernel, nchunks=tc // RWKV_CHUNK, single_block=single_block)
    row = pl.BlockSpec((tc, d), lambda b, c: (b * nt + c, 0))
    st = pl.BlockSpec((None, RWKV_PAIRS, LANES, LANES), lambda b, c: (b, 0, 0, 0))
    return pl.pallas_call(
        kern,
        grid=(bsz, nt),
        in_specs=[row] * 6 + [st],
        out_specs=[row, st],
        out_shape=[jax.ShapeDtypeStruct((n, d), F32),
                   jax.ShapeDtypeStruct((bsz, RWKV_PAIRS, LANES, LANES), F32)],
        compiler_params=_cparams("parallel", "arbitrary"),
    )(r, lw, k, v, al, be, s0)


def _rwkv_post_kernel(y_ref, r_ref, k_ref, v_ref, g_ref, lng_ref, lnb_ref, rk_ref, seg2_ref, segt2_ref, o_ref):
    seg2, seg_t2 = seg2_ref[...], segt2_ref[...]
    y = y_ref[...]
    inv = 1.0 / RWKV_HEAD_DIM
    dlt = y - _seg_sum_bcast(y, seg2, seg_t2) * inv
    var = _seg_sum_bcast(dlt * dlt, seg2, seg_t2) * inv
    yn = dlt * lax.rsqrt(var + RWKV_LN_EPS) * lng_ref[...] + lnb_ref[...]
    bonus = _seg_sum_bcast(r_ref[...] * k_ref[...] * rk_ref[...], seg2, seg_t2) * v_ref[...]
    o_ref[...] = ((yn + bonus) * g_ref[...]).astype(BF16)


def _rwkv_post(y, r, k, v, g, p, tm):
    n, d = y.shape
    row = pl.BlockSpec((tm, d), lambda i: (i, 0))
    return pl.pallas_call(
        _rwkv_post_kernel,
        grid=(n // tm,),
        in_specs=[row] * 5 + [_full((1, d))] * 3 + [_full((2 * d, LANES)), _full((2 * LANES, d))],
        out_specs=row,
        out_shape=jax.ShapeDtypeStruct((n, d), BF16),
        compiler_params=_cparams("parallel"),
    )(y, r, k, v, g, p["rwkv_ln_g"], p["rwkv_ln_b"], p["rwkv_r_k"], p["seg2"], p["seg_t2"])


def _mla_prep_kernel(x_ref, g_in_ref, w_in_ref, cos_ref, s1_ref, s2_ref, qag_ref, wqn_ref, wqr_ref, kvg_ref,
                     wkb_ref, wvb_ref,
                     gqn_ref, gqr_ref, gkn_ref, gkr_ref, q_ref, k_ref, v_ref, ckv_ref, kpe_ref):
    u = _in_proj(x_ref, g_in_ref, w_in_ref)
    cq = u[:, :MLA_Q_RANK]
    ckv_raw = u[:, MLA_Q_RANK:MLA_Q_RANK + MLA_KV_RANK]
    kr_raw = u[:, MLA_Q_RANK + MLA_KV_RANK:]
    cos, s1, s2 = cos_ref[...], s1_ref[...], s2_ref[...]

    def rope(blk):
        return (blk * cos + pltpu.roll(blk, LANES - MLA_ROPE // 2, 1) * s1
                + pltpu.roll(blk, MLA_ROPE // 2, 1) * s2)

    cqn = (_rms(cq) * qag_ref[...]).astype(BF16)
    qn = _dot(cqn, wqn_ref[...])
    qr = _dot(cqn, wqr_ref[...])
    ckv = _rms(ckv_raw) * kvg_ref[...]
    ckv_ref[...] = ckv
    kpe = rope(kr_raw)
    kpe_ref[...] = kpe
    ckb = ckv.astype(BF16)
    kn = _dot(ckb, wkb_ref[...])
    v_ref[...] = _dot(ckb, wvb_ref[...]).astype(BF16)
    ss_kpe = jnp.sum(kpe * kpe, axis=-1, keepdims=True)
    gqn, gqr, gkn, gkr = gqn_ref[...], gqr_ref[...], gkn_ref[...], gkr_ref[...]
    for h in range(MLA_HEADS):
        sl = slice(h * MLA_NOPE, (h + 1) * MLA_NOPE)
        o0 = h * MLA_QK_PAD
        qn_h = qn[:, sl]
        qr_h = rope(qr[:, sl])
        ss = jnp.sum(qn_h * qn_h, axis=-1, keepdims=True) + jnp.sum(qr_h * qr_h, axis=-1, keepdims=True)
        rn = lax.rsqrt(ss * (1.0 / MLA_QK) + NORM_EPS) * MLA_SCALE
        q_ref[:, o0:o0 + MLA_NOPE] = (qn_h * rn * gqn).astype(q_ref.dtype)
        q_ref[:, o0 + MLA_NOPE:o0 + MLA_QK_PAD] = (qr_h * rn * gqr).astype(q_ref.dtype)
        kn_h = kn[:, sl]
        ssk = jnp.sum(kn_h * kn_h, axis=-1, keepdims=True) + ss_kpe
        rnk = lax.rsqrt(ssk * (1.0 / MLA_QK) + NORM_EPS)
        k_ref[:, o0:o0 + MLA_NOPE] = (kn_h * rnk * gkn).astype(BF16)
        k_ref[:, o0 + MLA_NOPE:o0 + MLA_QK_PAD] = (kpe * rnk * gkr).astype(BF16)


def _mla_prep(x, rope_tabs, p, tm, q_dtype):
    n, d = x.shape
    ttab = rope_tabs[0].shape[0]
    ntab = ttab // tm
    hq = MLA_HEADS * MLA_QK_PAD
    hv = MLA_HEADS * MLA_V
    tab = pl.BlockSpec((tm, LANES), lambda i: (i % ntab, 0))
    return pl.pallas_call(
        _mla_prep_kernel,
        grid=(n // tm,),
        in_specs=[pl.BlockSpec((tm, d), lambda i: (i, 0)), _full((1, d)), _full((d, MLA_IN_PAD)), tab, tab, tab,
                  _full((1, MLA_Q_RANK)), _full((MLA_Q_RANK, hv)), _full((MLA_Q_RANK, hv)),
                  _full((1, MLA_KV_RANK)), _full((MLA_KV_RANK, hv)), _full((MLA_KV_RANK, hv)),
                  _full((1, LANES)), _full((1, LANES)), _full((1, LANES)), _full((1, LANES))],
        out_specs=[pl.BlockSpec((tm, hq), lambda i: (i, 0)), pl.BlockSpec((tm, hq), lambda i: (i, 0)),
                   pl.BlockSpec((tm, hv), lambda i: (i, 0)),
                   pl.BlockSpec((tm, MLA_KV_RANK), lambda i: (i, 0)),
                   pl.BlockSpec((tm, LANES), lambda i: (i, 0))],
        out_shape=[jax.ShapeDtypeStruct((n, hq), q_dtype), jax.ShapeDtypeStruct((n, hq), BF16),
                   jax.ShapeDtypeStruct((n, hv), BF16), jax.ShapeDtypeStruct((n, MLA_KV_RANK), F32),
                   jax.ShapeDtypeStruct((n, LANES), F32)],
        compiler_params=_cparams("parallel"),
    )(x, p["norm_mix_g"], p["w_mla"], *rope_tabs, p["mla_q_a_g"], p["wq_nope"], p["wq_rope"], p["mla_kv_a_g"], p["w_kb"], p["w_vb"],
      p["gq_nope"], p["gq_rope"], p["gk_nope"], p["gk_rope"])


FLASH_HEAD_GROUP = 2


def _flash_kernel(q_ref, k_ref, v_ref, o_ref, m_sc, acc_sc, *, tq):
    qi = pl.program_id(1)
    ki = pl.program_id(2)
    nblk = tq // LANES

    @pl.when(ki == 0)
    def _():
        m_sc[...] = jnp.full_like(m_sc, NEG_BIG)
        acc_sc[...] = jnp.zeros_like(acc_sc)

    ones = jnp.ones((tq, LANES), BF16)

    def update(diagonal):
        for h0 in range(0, MLA_HEADS, FLASH_HEAD_GROUP):
            heads = range(h0, h0 + FLASH_HEAD_GROUP)
            s = {h: _dot_nt(q_ref[:, h * MLA_QK_PAD:(h + 1) * MLA_QK_PAD],
                            k_ref[:, h * MLA_QK_PAD:(h + 1) * MLA_QK_PAD]) for h in heads}
            if diagonal:
                row = lax.broadcasted_iota(jnp.int32, (tq, tq), 0)
                col = lax.broadcasted_iota(jnp.int32, (tq, tq), 1)
                s = {h: jnp.where(col <= row, s[h], NEG_BIG) for h in heads}
            m_prev = {h: m_sc[h] for h in heads}
            m_new = {h: jnp.maximum(m_prev[h], jnp.max(s[h], axis=-1, keepdims=True)) for h in heads}
            pr = {h: jnp.concatenate([jnp.exp(s[h][:, j * LANES:(j + 1) * LANES] - m_new[h]) for j in range(nblk)],
                                     axis=1).astype(BF16) for h in heads}
            corr = {h: jnp.exp(m_prev[h] - m_new[h]) for h in heads}
            for h in heads:
                v_ext = jnp.concatenate([v_ref[:, h * MLA_V:(h + 1) * MLA_V], ones], axis=1)
                acc_sc[h] = acc_sc[h] * jnp.concatenate([corr[h], corr[h]], axis=1) + _dot(pr[h], v_ext)
                m_sc[h] = m_new[h]

    @pl.when(ki < qi)
    def _():
        update(False)

    @pl.when(ki == qi)
    def _():
        update(True)
        for h in range(MLA_HEADS):
            acc = acc_sc[h]
            o_ref[:, h * MLA_V:(h + 1) * MLA_V] = (acc[:, :MLA_V] / acc[:, MLA_V:]).astype(BF16)


def _flash(q, k, v, bsz, tq):
    n = q.shape[0]
    nq = n // bsz // tq
    hq = MLA_HEADS * MLA_QK_PAD
    hv = MLA_HEADS * MLA_V
    kern = functools.partial(_flash_kernel, tq=tq)
    return pl.pallas_call(
        kern,
        grid=(bsz, nq, nq),
        in_specs=[pl.BlockSpec((tq, hq), lambda b, i, j: (b * nq + i, 0)),
                  pl.BlockSpec((tq, hq), lambda b, i, j: (b * nq + jnp.minimum(i, j), 0)),
                  pl.BlockSpec((tq, hv), lambda b, i, j: (b * nq + jnp.minimum(i, j), 0))],
        out_specs=pl.BlockSpec((tq, hv), lambda b, i, j: (b * nq + i, 0)),
        out_shape=jax.ShapeDtypeStruct((n, hv), BF16),
        scratch_shapes=[pltpu.VMEM((MLA_HEADS, tq, LANES), F32),
                        pltpu.VMEM((MLA_HEADS, tq, MLA_V + LANES), F32)],
        compiler_params=_cparams("parallel", "parallel", "arbitrary"),
    )(q, k, v)


PAGES_PER_STEP = 32
PAGED_ROWS = MLA_HEADS * SAMPLE_T_PAD


def _paged_kernel(pt_ref, q_ref, *refs, t_new, npp):
    lat_refs = refs[:npp]
    rope_refs = refs[npp:2 * npp]
    (newc_ref, newr_ref, wkbt_ref, wvb_ref, gkn_ref, gkr_ref, o_ref,
     m_sc, l_sc, acc_sc, wq_sc, qr_sc) = refs[2 * npp:]
    step = pl.program_id(1)
    nkn = MLA_HEADS * MLA_NOPE

    @pl.when(step == 0)
    def _():
        m_sc[...] = jnp.full_like(m_sc, NEG_BIG)
        l_sc[...] = jnp.zeros_like(l_sc)
        acc_sc[...] = jnp.zeros_like(acc_sc)
        q = q_ref[...]
        gkn, gkr = gkn_ref[...], gkr_ref[...]
        wkbt = wkbt_ref[...]
        wq_sc[0:nkn, :] = wkbt
        for h in range(MLA_HEADS):
            o0 = h * MLA_QK_PAD
            rows = slice(h * SAMPLE_T_PAD, (h + 1) * SAMPLE_T_PAD)
            qn = (q[:, o0:o0 + MLA_NOPE] * gkn).astype(BF16)
            wq_sc[nkn + h * SAMPLE_T_PAD:nkn + (h + 1) * SAMPLE_T_PAD, :] = _dot(
                qn, wkbt[h * MLA_NOPE:(h + 1) * MLA_NOPE, :]).astype(BF16)
            qr_sc[rows, :] = (q[:, o0 + MLA_NOPE:o0 + MLA_QK_PAD] * gkr).astype(BF16)

    wq = wq_sc[...]
    qr = qr_sc[...]

    def scores(cbs, krs):
        n = len(cbs)
        big = [_dot_nt(wq, cbs[i]) for i in range(n)]
        rope = [_dot(qr, krs[i].astype(BF16)) for i in range(n)]
        ssr = [jnp.sum(krs[i] * krs[i], axis=0, keepdims=True) for i in range(n)]
        out = []
        for i in range(n):
            rn = []
            for h in range(MLA_HEADS):
                kn_h = big[i][h * MLA_NOPE:(h + 1) * MLA_NOPE]
                ss = jnp.sum(kn_h * kn_h, axis=0, keepdims=True) + ssr[i]
                rn.append(jnp.broadcast_to(lax.rsqrt(ss * (1.0 / MLA_QK) + NORM_EPS),
                                           (SAMPLE_T_PAD, ss.shape[1])))
            out.append((big[i][nkn:] + rope[i]) * jnp.concatenate(rn, axis=0))
        return out

    def softmax_update(s_list, cb_list):
        m_prev = m_sc[...]
        m_new = m_prev
        for s in s_list:
            m_new = jnp.maximum(m_new, jnp.max(s, axis=-1, keepdims=True))
        corr = jnp.exp(m_prev - m_new)
        l_new = l_sc[...] * corr
        acc = acc_sc[...] * corr
        for s, cb in zip(s_list, cb_list):
            pr = jnp.exp(s - m_new)
            l_new = l_new + jnp.sum(pr, axis=-1, keepdims=True)
            acc = acc + _dot(pr.astype(BF16), cb)
        l_sc[...] = l_new
        acc_sc[...] = acc
        m_sc[...] = m_new

    grp = 2 if npp % 2 == 0 else 1
    rope_pad = jnp.zeros((LANES - MLA_ROPE, grp * PAGE_SIZE), F32)
    cbs, krs = [], []
    for i in range(0, npp, grp):
        cbs.append(jnp.concatenate([lat_refs[i + j][...] for j in range(grp)], axis=0).astype(BF16))
        krs.append(jnp.concatenate([jnp.concatenate([rope_refs[i + j][...] for j in range(grp)], axis=1),
                                    rope_pad], axis=0))
    softmax_update(scores(cbs, krs), cbs)

    @pl.when(step == pl.num_programs(1) - 1)
    def _():
        cb = newc_ref[...].astype(BF16)
        s = scores([cb], [newr_ref[...]])[0]
        key = lax.broadcasted_iota(jnp.int32, s.shape, 1)
        qry = lax.broadcasted_iota(jnp.int32, s.shape, 0) % SAMPLE_T_PAD
        softmax_update([jnp.where((key <= qry) & (key < t_new), s, NEG_BIG)], [cb])
        o_lat = (acc_sc[...] / l_sc[...]).astype(BF16)
        wvb = wvb_ref[...]
        for h in range(MLA_HEADS):
            o_ref[:, h * MLA_V:(h + 1) * MLA_V] = _dot(
                o_lat[h * SAMPLE_T_PAD:(h + 1) * SAMPLE_T_PAD],
                wvb[:, h * MLA_V:(h + 1) * MLA_V])


def _paged_attention(page_table, q, lat_pool, rope_pool_t, layer, new_c, new_r_t, p, t_new):
    bsz, n_pages = page_table.shape
    npp = math.gcd(PAGES_PER_STEP, n_pages)
    nsteps = n_pages // npp
    hq = MLA_HEADS * MLA_QK_PAD
    hv = MLA_HEADS * MLA_V

    def lat_spec(i):
        return pl.BlockSpec((None, None, PAGE_SIZE, MLA_KV_RANK),
                            lambda b, s, pt: (layer, pt[b, s * npp + i], 0, 0))

    def rope_spec(i):
        return pl.BlockSpec((None, None, MLA_ROPE, PAGE_SIZE),
                            lambda b, s, pt: (layer, pt[b, s * npp + i], 0, 0))

    grid_spec = pltpu.PrefetchScalarGridSpec(
        num_scalar_prefetch=1,
        grid=(bsz, nsteps),
        in_specs=[pl.BlockSpec((SAMPLE_T_PAD, hq), lambda b, s, pt: (b, 0))]
                 + [lat_spec(i) for i in range(npp)] + [rope_spec(i) for i in range(npp)]
                 + [pl.BlockSpec((None, PAGE_SIZE, MLA_KV_RANK), lambda b, s, pt: (b, 0, 0)),
                    pl.BlockSpec((None, LANES, PAGE_SIZE), lambda b, s, pt: (b, 0, 0)),
                    pl.BlockSpec((hv, MLA_KV_RANK), lambda b, s, pt: (0, 0)),
                    pl.BlockSpec((MLA_KV_RANK, hv), lambda b, s, pt: (0, 0)),
                    pl.BlockSpec((1, LANES), lambda b, s, pt: (0, 0)),
                    pl.BlockSpec((1, LANES), lambda b, s, pt: (0, 0))],
        out_specs=pl.BlockSpec((SAMPLE_T_PAD, hv), lambda b, s, pt: (b, 0)),
        scratch_shapes=[pltpu.VMEM((PAGED_ROWS, 1), F32),
                        pltpu.VMEM((PAGED_ROWS, 1), F32),
                        pltpu.VMEM((PAGED_ROWS, MLA_KV_RANK), F32),
                        pltpu.VMEM((hv + PAGED_ROWS, MLA_KV_RANK), BF16),
                        pltpu.VMEM((PAGED_ROWS, LANES), BF16)],
    )
    kern = functools.partial(_paged_kernel, t_new=t_new, npp=npp)
    return pl.pallas_call(
        kern,
        grid_spec=grid_spec,
        out_shape=jax.ShapeDtypeStruct((bsz * SAMPLE_T_PAD, hv), F32),
        compiler_params=_cparams("parallel", "arbitrary"),
    )(page_table, q, *([lat_pool] * npp), *([rope_pool_t] * npp), new_c, new_r_t,
      p["w_kb_t"], p["w_vb"], p["gk_nope"], p["gk_rope"])


def _merge_kernel(x_ref, g_ref, wg_ref, bg_ref, os_ref, or_ref, om_ref, ws_ref, wr_ref, wm_ref, wo_ref, o_ref):
    h = (_rms(x_ref[...]) * g_ref[...]).astype(BF16)
    gl = _dot(h, wg_ref[...]) + bg_ref[...]
    d = D_MODEL
    merged = (_sigmoid(gl[:, :d]) * _dot(os_ref[...], ws_ref[...])
              + _sigmoid(gl[:, d:2 * d]) * _dot(or_ref[...], wr_ref[...])
              + _sigmoid(gl[:, 2 * d:]) * _dot(om_ref[...].astype(BF16), wm_ref[...]))
    o_ref[...] = x_ref[...] + _dot(merged.astype(BF16), wo_ref[...])


def _merge(x, o_ssd, o_rwkv, o_mla, p, tm):
    n, d = x.shape
    row = pl.BlockSpec((tm, d), lambda i: (i, 0))
    return pl.pallas_call(
        _merge_kernel,
        grid=(n // tm,),
        in_specs=[row, _full((1, d)), _full((d, N_BRANCH * d)), _full((1, N_BRANCH * d)), row, row, row]
                 + [_full((d, d))] * 4,
        out_specs=row,
        out_shape=jax.ShapeDtypeStruct((n, d), F32),
        compiler_params=_cparams("parallel"),
    )(x, p["norm_mix_g"], p["w_gate"], p["b_gate"], o_ssd, o_rwkv, o_mla,
      p["w_o_ssd"], p["w_o_rwkv"], p["w_o_mla"], p["w_out"])


def _ffn_kernel(x_ref, tail0_ref, g_ref, wup_ref, cw_ref, cb_ref, wdn_ref, o_ref, tail_ref, buf,
                *, tm, t_valid, nb):
    j = pl.program_id(1)

    @pl.when(j == 0)
    def _():
        buf[:, 0:SUBLANES, :] = tail0_ref[...]

    x = x_ref[...]
    h2 = (_rms(x) * g_ref[...]).astype(BF16)
    up = _dot(h2, wup_ref[...])
    acts = []
    for i in range(nb):
        buf[i, SUBLANES:SUBLANES + tm, :] = up[i * tm:(i + 1) * tm]
        conv = cb_ref[...]
        for c in range(FFN_CONV):
            conv = conv + buf[i, pl.ds(SUBLANES - (FFN_CONV - 1) + c, tm), :] * cw_ref[c:c + 1, :]
        tail = buf[i, pl.ds(t_valid, SUBLANES), :]
        tail_ref[i] = tail
        buf[i, 0:SUBLANES, :] = tail
        acts.append(_silu(conv[:, :D_FF]) * conv[:, D_FF:])
    act = (jnp.concatenate(acts, axis=0) if nb > 1 else acts[0]).astype(BF16)
    o_ref[...] = x + _dot(act, wdn_ref[...])


def _ffn(x, tail0, p, tm, t_valid, nb):
    bsz = tail0.shape[0]
    n, d = x.shape
    nt = n // bsz // tm
    assert nb == 1 or nt == 1
    kern = functools.partial(_ffn_kernel, tm=tm, t_valid=t_valid, nb=nb)
    return pl.pallas_call(
        kern,
        grid=(bsz // nb, nt),
        in_specs=[pl.BlockSpec((nb * tm, d), lambda b, j: (b * nt + j, 0)),
                  pl.BlockSpec((nb, SUBLANES, 2 * D_FF), lambda b, j: (b, 0, 0)),
                  _full((1, d)),
                  pl.BlockSpec((d, 2 * D_FF), lambda b, j: (0, 0), pipeline_mode=pl.Buffered(1)),
                  _full((FFN_CONV, 2 * D_FF)), _full((1, 2 * D_FF)),
                  pl.BlockSpec((D_FF, d), lambda b, j: (0, 0), pipeline_mode=pl.Buffered(1))],
        out_specs=[pl.BlockSpec((nb * tm, d), lambda b, j: (b * nt + j, 0)),
                   pl.BlockSpec((nb, SUBLANES, 2 * D_FF), lambda b, j: (b, 0, 0))],
        out_shape=[jax.ShapeDtypeStruct((n, d), F32),
                   jax.ShapeDtypeStruct((bsz, SUBLANES, 2 * D_FF), F32)],
        scratch_shapes=[pltpu.VMEM((nb, SUBLANES + tm, 2 * D_FF), F32)],
        compiler_params=_cparams("parallel", "arbitrary"),
    )(x, tail0, p["norm_ffn_g"], p["ffn_w_up"], p["ffn_conv_w"], p["ffn_conv_b"], p["ffn_w_down"])


def _pad_cols(w, width):
    return jnp.pad(w, ((0, 0), (0, width - w.shape[1])))


def _row(v):
    return v.reshape(1, -1)


def _prep_layer(i, w):
    d = RWKV_DIM
    w_in = w["w_in"][i]
    o_r = SSD_IN
    o_m = SSD_IN + RWKV_IN
    p = {}
    p["norm_mix_g"] = _row(w["norm_mix_g"][i])
    p["w_ssd"] = _pad_cols(w_in[:, :o_r], SSD_IN_PAD).astype(BF16)
    p["w_rwkv"] = w_in[:, o_r:o_m].astype(BF16)
    p["w_mla"] = _pad_cols(w_in[:, o_m:], MLA_IN_PAD).astype(BF16)
    p["w_gate"] = w["w_gate"][i].astype(BF16)
    p["b_gate"] = _row(w["b_gate"][i])
    p["ssd_conv_w"] = w["ssd_conv_w"][i]
    p["ssd_conv_b"] = _row(w["ssd_conv_b"][i])
    p["ssd_dt_bias"] = _pad_cols(_row(w["ssd_dt_bias"][i]), LANES)
    p["ssd_a_log"] = _pad_cols(_row(w["ssd_a_log"][i]), LANES)
    p["ssd_d"] = _row(jnp.repeat(w["ssd_d"][i], SSD_HEAD_DIM))
    p["ssd_norm_g"] = _row(w["ssd_norm_g"][i])
    p["rwkv_shift_mu"] = _row(w["rwkv_shift_mu"][i])
    p["rwkv_w0"] = _row(w["rwkv_w0"][i])
    zeros_lora = jnp.zeros((RWKV_DECAY_LORA, d), F32)
    p["rwkv_w_up"] = jnp.concatenate([w["rwkv_w_up"][i], zeros_lora], axis=0).astype(BF16)
    p["rwkv_a0"] = _row(w["rwkv_a0"][i])
    p["rwkv_a_up"] = jnp.concatenate([zeros_lora, w["rwkv_a_up"][i]], axis=0).astype(BF16)
    p["rwkv_g_up"] = w["rwkv_g_up"][i].astype(BF16)
    p["rwkv_k_k"] = _row(w["rwkv_k_k"][i])
    p["rwkv_k_a"] = _row(w["rwkv_k_a"][i])
    p["rwkv_r_k"] = _row(w["rwkv_r_k"][i])
    p["rwkv_ln_g"] = _row(w["rwkv_ln_g"][i])
    p["rwkv_ln_b"] = _row(w["rwkv_ln_b"][i])
    wq = w["mla_w_q_b"][i]
    hv = MLA_HEADS * MLA_NOPE
    p["mla_q_a_g"] = _row(w["mla_q_a_g"][i])
    p["wq_nope"] = wq[:, :, :MLA_NOPE].reshape(MLA_Q_RANK, hv).astype(BF16)
    p["wq_rope"] = jnp.pad(wq[:, :, MLA_NOPE:], ((0, 0), (0, 0), (0, MLA_NOPE - MLA_ROPE))
                           ).reshape(MLA_Q_RANK, hv).astype(BF16)
    p["mla_kv_a_g"] = _row(w["mla_kv_a_g"][i])
    p["w_kb"] = w["mla_w_kb"][i].reshape(MLA_KV_RANK, hv).astype(BF16)
    p["w_kb_t"] = p["w_kb"].T
    p["w_vb"] = w["mla_w_vb"][i].reshape(MLA_KV_RANK, MLA_HEADS * MLA_V).astype(BF16)
    gq, gk = w["mla_q_norm_g"][i], w["mla_k_norm_g"][i]
    p["gq_nope"] = _row(gq[:MLA_NOPE])
    p["gq_rope"] = _pad_cols(_row(gq[MLA_NOPE:]), LANES)
    p["gk_nope"] = _row(gk[:MLA_NOPE])
    p["gk_rope"] = _pad_cols(_row(gk[MLA_NOPE:]), LANES)
    for name in ("w_o_ssd", "w_o_rwkv", "w_o_mla", "w_out"):
        p[name] = w[name][i].astype(BF16)
    p["norm_ffn_g"] = _row(w["norm_ffn_g"][i])
    p["ffn_w_up"] = w["ffn_w_up"][i].astype(BF16)
    p["ffn_conv_w"] = w["ffn_conv_w"][i]
    p["ffn_conv_b"] = _row(w["ffn_conv_b"][i])
    p["ffn_w_down"] = w["ffn_w_down"][i].astype(BF16)
    return p


def _constants():
    ch = np.arange(RWKV_DIM)
    seg = (ch[:, None] // RWKV_HEAD_DIM == np.arange(LANES)[None, :]).astype(np.float32)
    seg2 = np.concatenate([seg, seg], axis=0)
    seg_t2 = np.concatenate([seg.T, seg.T], axis=0)
    return jnp.asarray(seg2, BF16), jnp.asarray(seg_t2, BF16)


def _rope_tables(pos):
    half = MLA_ROPE // 2
    inv = ROPE_BASE ** (-jnp.arange(half, dtype=F32) / half)
    ang = pos.astype(F32)[:, None] * inv[None, :]
    cos, sin = jnp.cos(ang), jnp.sin(ang)
    z = jnp.zeros_like(cos)
    z2 = jnp.zeros((pos.shape[0], LANES - MLA_ROPE), F32)
    return (jnp.concatenate([cos, cos, z2], axis=1),
            jnp.concatenate([-sin, z, z2], axis=1),
            jnp.concatenate([z, sin, z2], axis=1))


def _tail_block(state, rows):
    return jnp.pad(state, ((0, 0), (SUBLANES - rows, 0), (0, 0)))


def _pack_rwkv_state(s):
    b = s.shape[0]
    s5 = s.reshape(b, RWKV_PAIRS, 2, RWKV_HEAD_DIM, RWKV_HEAD_DIM)
    z = jnp.zeros_like(s5[:, :, 0])
    top = jnp.concatenate([s5[:, :, 0], z], axis=-1)
    bot = jnp.concatenate([z, s5[:, :, 1]], axis=-1)
    return jnp.concatenate([top, bot], axis=-2)


def _unpack_rwkv_state(s):
    b = s.shape[0]
    hd = RWKV_HEAD_DIM
    return jnp.stack([s[:, :, :hd, :hd], s[:, :, hd:, hd:]], axis=2).reshape(b, RWKV_HEADS, hd, hd)


def _layer(x, st, p, cfg, attend):
    bsz, t, t_valid, tm = cfg["bsz"], cfg["t"], cfg["t_valid"], cfg["tm"]
    n = x.shape[0]
    tmn = min(tm, n)

    if t % SSD_CHUNK:
        lc = SSD_SHORT_CHUNK
        x_pad = jnp.pad(x.reshape(bsz, t, D_MODEL), ((0, 0), (0, lc - t), (0, 0)))
        o_ssd, ssd_tail, ssd_h = _ssd(x_pad.reshape(bsz * lc, D_MODEL), st["ssd_tail"], st["ssd_h"], p, t_valid, lc,
                                      math.gcd(bsz, 8))
        o_ssd = o_ssd.reshape(bsz, lc, SSD_INNER)[:, :t].reshape(n, SSD_INNER)
    else:
        o_ssd, ssd_tail, ssd_h = _ssd(x, st["ssd_tail"], st["ssd_h"], p, SSD_CHUNK, SSD_CHUNK, 1)

    tm_r = min(tm, t)
    nb_r = math.gcd(bsz, 16) if t <= SAMPLE_T_PAD else 1
    r, lw, k2, v, al, be, gg, rwkv_tail = _rwkv_pre(x, st["rwkv_tail"], p, tm_r, min(t_valid, tm_r), nb_r)
    d = RWKV_DIM
    if t % RWKV_CHUNK:
        def chunk_pad(a_):
            a_ = a_.reshape(bsz, t, d)[:, :t_valid]
            return jnp.pad(a_, ((0, 0), (0, RWKV_CHUNK - t_valid), (0, 0))).reshape(bsz * RWKV_CHUNK, d)
        yy, rwkv_s = _rwkv_scan(*(chunk_pad(a_) for a_ in (r, lw, k2, v, al, be)), st["rwkv_s"], bsz, RWKV_CHUNK,
                                t_valid <= RWKV_SOLVE_BLOCK)
        yy = yy.reshape(bsz, RWKV_CHUNK, d)[:, :t].reshape(n, d)
    else:
        yy, rwkv_s = _rwkv_scan(r, lw, k2, v, al, be, st["rwkv_s"], bsz, min(t, cfg["tc_rwkv"]), False)
    o_rwkv = _rwkv_post(yy, r, k2, v, gg, p, tmn)

    q, k, vv, ckv, kpe = _mla_prep(x, cfg["rope"], p, tmn, cfg["q_dtype"])
    o_mla = attend(q, k, vv, ckv, kpe)

    x = _merge(x, o_ssd, o_rwkv, o_mla, p, tmn)
    tm_f = min(cfg["tm_ffn"], t)
    x, ffn_tail = _ffn(x, st["ffn_tail"], p, tm_f, min(t_valid, tm_f), cfg["nb_ffn"])
    new_st = dict(ssd_tail=ssd_tail, ssd_h=ssd_h, rwkv_tail=rwkv_tail, rwkv_s=rwkv_s, ffn_tail=ffn_tail)
    return x, new_st, ckv, kpe


def kernel(x_prompt, x_sample, cache_kv_latent, cache_k_rope, page_table, state_ssm, state_ssm_conv,
           state_rwkv, state_rwkv_shift, state_ffn_conv, norm_mix_g, w_in, ssd_conv_w, ssd_conv_b,
           ssd_dt_bias, ssd_a_log, ssd_d, ssd_norm_g, rwkv_shift_mu, rwkv_w0, rwkv_w_up, rwkv_a0,
           rwkv_a_up, rwkv_g_up, rwkv_k_k, rwkv_k_a, rwkv_r_k, rwkv_ln_g, rwkv_ln_b, mla_q_a_g,
           mla_w_q_b, mla_kv_a_g, mla_w_kb, mla_w_vb, mla_q_norm_g, mla_k_norm_g, w_gate, b_gate,
           w_o_ssd, w_o_rwkv, w_o_mla, w_out, norm_ffn_g, ffn_w_up, ffn_conv_w, ffn_conv_b, ffn_w_down):
    w = dict(norm_mix_g=norm_mix_g, w_in=w_in, ssd_conv_w=ssd_conv_w, ssd_conv_b=ssd_conv_b,
             ssd_dt_bias=ssd_dt_bias, ssd_a_log=ssd_a_log, ssd_d=ssd_d, ssd_norm_g=ssd_norm_g,
             rwkv_shift_mu=rwkv_shift_mu, rwkv_w0=rwkv_w0, rwkv_w_up=rwkv_w_up, rwkv_a0=rwkv_a0,
             rwkv_a_up=rwkv_a_up, rwkv_g_up=rwkv_g_up, rwkv_k_k=rwkv_k_k, rwkv_k_a=rwkv_k_a,
             rwkv_r_k=rwkv_r_k, rwkv_ln_g=rwkv_ln_g, rwkv_ln_b=rwkv_ln_b, mla_q_a_g=mla_q_a_g,
             mla_w_q_b=mla_w_q_b, mla_kv_a_g=mla_kv_a_g, mla_w_kb=mla_w_kb, mla_w_vb=mla_w_vb,
             mla_q_norm_g=mla_q_norm_g, mla_k_norm_g=mla_k_norm_g, w_gate=w_gate, b_gate=b_gate,
             w_o_ssd=w_o_ssd, w_o_rwkv=w_o_rwkv, w_o_mla=w_o_mla, w_out=w_out, norm_ffn_g=norm_ffn_g,
             ffn_w_up=ffn_w_up, ffn_conv_w=ffn_conv_w, ffn_conv_b=ffn_conv_b, ffn_w_down=ffn_w_down)
    depth = w_in.shape[0]
    bp, tp, d = x_prompt.shape
    bs, ts, _ = x_sample.shape
    tsp = SAMPLE_T_PAD
    past_len = page_table.shape[1] * PAGE_SIZE
    seg2, seg_t2 = _constants()
    rope_pool_t = jnp.swapaxes(cache_k_rope, 2, 3)

    tm_p = min(256, tp)
    cfg_p = dict(bsz=bp, t=tp, t_valid=tp, tm=tm_p, tm_ffn=min(256, tp), nb_ffn=1, tc_rwkv=256,
                 rope=_rope_tables(jnp.arange(tp)), q_dtype=BF16)
    pos_s = past_len + jnp.arange(tsp)
    rope_s = tuple(jnp.tile(tb, (bs, 1)) for tb in _rope_tables(pos_s))
    cfg_s = dict(bsz=bs, t=tsp, t_valid=ts, tm=bs * tsp, tm_ffn=tsp, nb_ffn=math.gcd(bs, 16), tc_rwkv=RWKV_CHUNK, rope=rope_s,
                 q_dtype=F32)

    xp = x_prompt.reshape(bp * tp, d)
    xs = jnp.pad(x_sample, ((0, 0), (0, tsp - ts), (0, 0))).reshape(bs * tsp, d)

    zero_st = dict(ssd_tail=jnp.zeros((bp, SUBLANES, SSD_CONV_DIM), F32),
                   ssd_h=jnp.zeros((bp, SSD_HEADS // 2, LANES, SSD_STATE), F32),
                   rwkv_tail=jnp.zeros((bp, SUBLANES, RWKV_IN), F32),
                   rwkv_s=jnp.zeros((bp, RWKV_PAIRS, LANES, LANES), F32),
                   ffn_tail=jnp.zeros((bp, SUBLANES, 2 * D_FF), F32))

    new_p = [[] for _ in range(7)]
    new_s = [[] for _ in range(7)]
    for i in range(depth):
        p = _prep_layer(i, w)
        p["seg2"], p["seg_t2"] = seg2, seg_t2

        def prompt_attend(q, k, v, ckv, kpe):
            return _flash(q, k, v, bp, min(512, tp))

        xp, st_p, ckv_p, kpe_p = _layer(xp, zero_st, p, cfg_p, prompt_attend)

        st_in = dict(ssd_tail=_tail_block(state_ssm_conv[i], SSD_CONV - 1),
                     ssd_h=state_ssm[i].reshape(bs, SSD_HEADS // 2, LANES, SSD_STATE),
                     rwkv_tail=_tail_block(state_rwkv_shift[i][:, None, :], 1),
                     rwkv_s=_pack_rwkv_state(state_rwkv[i]),
                     ffn_tail=_tail_block(state_ffn_conv[i], FFN_CONV - 1))

        def sample_attend(q, k, v, ckv, kpe, i=i, p=p):
            new_c = jnp.pad(ckv.reshape(bs, tsp, MLA_KV_RANK), ((0, 0), (0, PAGE_SIZE - tsp), (0, 0)))
            new_r_t = jnp.pad(jnp.swapaxes(kpe.reshape(bs, tsp, LANES), 1, 2), ((0, 0), (0, 0), (0, PAGE_SIZE - tsp)))
            return _paged_attention(page_table, q, cache_kv_latent, rope_pool_t, i, new_c, new_r_t, p, ts)

        xs, st_s, ckv_s, kpe_s = _layer(xs, st_in, p, cfg_s, sample_attend)

        for lst, st, ckv, kpe, b_, t_, tv in ((new_p, st_p, ckv_p, kpe_p, bp, tp, tp),
                                              (new_s, st_s, ckv_s, kpe_s, bs, tsp, ts)):
            lst[0].append(ckv.reshape(b_, t_, MLA_KV_RANK)[:, :tv])
            lst[1].append(kpe.reshape(b_, t_, LANES)[:, :tv, :MLA_ROPE])
            lst[2].append(st["ssd_h"].reshape(b_, SSD_HEADS, SSD_HEAD_DIM, SSD_STATE))
            lst[3].append(st["ssd_tail"][:, SUBLANES - (SSD_CONV - 1):])
            lst[4].append(_unpack_rwkv_state(st["rwkv_s"]))
            lst[5].append(st["rwkv_tail"][:, SUBLANES - 1])
            lst[6].append(st["ffn_tail"][:, SUBLANES - (FFN_CONV - 1):])

    outs_p = [jnp.stack(v_, axis=0) for v_ in new_p]
    outs_s = [jnp.stack(v_, axis=0) for v_ in new_s]
    y_p = xp.reshape(bp, tp, d)
    y_s = xs.reshape(bs, tsp, d)[:, :ts]
    return (y_p, y_s, *outs_p, *outs_s)
```

```python
import functools
import math

import numpy as np
import jax
import jax.numpy as jnp
from jax import lax
from jax.experimental import pallas as pl
from jax.experimental.pallas import tpu as pltpu

F32 = jnp.float32
BF16 = jnp.bfloat16

D_MODEL = 1024
PAGE_SIZE = 128

SSD_HEAD_DIM = 64
SSD_INNER = 1024
SSD_HEADS = 16
SSD_GROUPS = 2
SSD_STATE = 128
SSD_CONV = 4
SSD_CHUNK = 128
SSD_SHORT_CHUNK = 16
SSD_CONV_DIM = SSD_INNER + 2 * SSD_GROUPS * SSD_STATE
SSD_IN = SSD_INNER + SSD_CONV_DIM + SSD_HEADS
SSD_IN_PAD = SSD_INNER + SSD_CONV_DIM + 128

RWKV_HEAD_DIM = 64
RWKV_DIM = 1024
RWKV_HEADS = 16
RWKV_DECAY_LORA = 64
RWKV_A_LORA = 64
RWKV_GATE_LORA = 128
RWKV_IN = 3 * RWKV_DIM + RWKV_DECAY_LORA + RWKV_A_LORA + RWKV_GATE_LORA
RWKV_LN_EPS = 64e-5

MLA_HEADS = 8
MLA_NOPE = 128
MLA_ROPE = 64
MLA_QK = MLA_NOPE + MLA_ROPE
MLA_V = 128
MLA_Q_RANK = 512
MLA_KV_RANK = 256
MLA_IN = MLA_Q_RANK + MLA_KV_RANK + MLA_ROPE
MLA_IN_PAD = MLA_Q_RANK + MLA_KV_RANK + 128
MLA_QK_PAD = 256
ROPE_BASE = 10000.0
MLA_SCALE = MLA_QK ** -0.5

N_BRANCH = 3
D_FF = 2816
FFN_CONV = 3
NORM_EPS = 1e-6

SUBLANES = 8
LANES = 128
SAMPLE_T_PAD = 8
NEG_BIG = -1e30

V7X_VMEM_BYTES = 64 * 1024 * 1024
VMEM_LIMIT = V7X_VMEM_BYTES // 8 * 7


def _cparams(*sem):
    return pltpu.CompilerParams(dimension_semantics=sem, vmem_limit_bytes=VMEM_LIMIT)


def _full(shape):
    nd = len(shape)
    return pl.BlockSpec(shape, lambda *_: (0,) * nd)


def _rms(x, eps=NORM_EPS):
    return x * lax.rsqrt(jnp.mean(x * x, axis=-1, keepdims=True) + eps)


def _sigmoid(x):
    return 1.0 / (1.0 + jnp.exp(-x))


def _silu(x):
    return x * _sigmoid(x)


def _softplus(x):
    return jnp.maximum(x, 0.0) + jnp.log(1.0 + jnp.exp(-jnp.abs(x)))


def _dot(a, b):
    return jnp.dot(a, b, preferred_element_type=F32)


def _dot_nt(a, b):
    return lax.dot_general(a, b, (((1,), (1,)), ((), ())), preferred_element_type=F32)


def _dot_tn(a, b):
    return lax.dot_general(a, b, (((0,), (0,)), ((), ())), preferred_element_type=F32)


def _hi_lo(x):
    hi = x.astype(BF16)
    return hi, (x - hi.astype(F32)).astype(BF16)


def _split_hi_lo(x):
    return jnp.concatenate(_hi_lo(x), axis=1)


def _in_proj(x_ref, g_ref, w_ref):
    return _dot((_rms(x_ref[...]) * g_ref[...]).astype(BF16), w_ref[...])


def _resident(shape):
    nd = len(shape)
    return pl.BlockSpec(shape, lambda *_: (0,) * nd, pipeline_mode=pl.Buffered(1))


def _ssd_kernel(x_ref, g_ref, w_ref, tail0_ref, h0_ref, cw_ref, cb_ref, dtb_ref, alog_ref, dsk_ref, ng_ref,
                o_ref, tail_ref, h_ref, buf, *, t_valid, L, nb):
    c = pl.program_id(1)

    @pl.when(c == 0)
    def _():
        buf[:, 0:SUBLANES, :] = tail0_ref[...]
        h_ref[...] = h0_ref[...]

    u_all = _in_proj(x_ref, g_ref, w_ref)
    for i in range(nb):
        rows = slice(i * L, (i + 1) * L)
        o_ref[rows, :] = _ssd_chunk(u_all[rows], i, cw_ref, cb_ref, dtb_ref, alog_ref, dsk_ref, ng_ref,
                                    tail_ref, h_ref, buf, t_valid, L)


def _ssd_chunk(u, i, cw_ref, cb_ref, dtb_ref, alog_ref, dsk_ref, ng_ref, tail_ref, h_ref, buf, t_valid, L):
    z = u[:, :SSD_INNER]
    dt_raw = u[:, SSD_INNER + SSD_CONV_DIM:]
    buf[i, SUBLANES:SUBLANES + L, :] = u[:, SSD_INNER:SSD_INNER + SSD_CONV_DIM]
    acc = cb_ref[...]
    for j in range(SSD_CONV):
        acc = acc + buf[i, pl.ds(SUBLANES - (SSD_CONV - 1) + j, L), :] * cw_ref[j:j + 1, :]
    tail = buf[i, pl.ds(t_valid, SUBLANES), :]
    tail_ref[i] = tail
    buf[i, 0:SUBLANES, :] = tail
    xbc = _silu(acc)
    xs = xbc[:, :SSD_INNER]

    dt = _softplus(dt_raw + dtb_ref[...])
    if t_valid < L:
        row = lax.broadcasted_iota(jnp.int32, dt.shape, 0)
        dt = jnp.where(row < t_valid, dt, 0.0)
    a = -jnp.exp(alog_ref[...])
    da = dt * a
    ri = lax.broadcasted_iota(jnp.int32, (L, L), 0)
    ci = lax.broadcasted_iota(jnp.int32, (L, L), 1)
    causal = ri >= ci
    tri = jnp.where(causal, 1.0, 0.0).astype(F32)
    cs = jnp.dot(tri, da, preferred_element_type=F32, precision=lax.Precision.HIGHEST)
    cs_t = cs.T
    total = cs[L - 1:L, :]
    etot = jnp.exp(total)
    lane = lax.broadcasted_iota(jnp.int32, (L, LANES), 1)
    lo_half = lane < SSD_HEAD_DIM
    row_lo = lax.broadcasted_iota(jnp.int32, (LANES, LANES), 0) < SSD_HEAD_DIM

    def pair_cols(m, j0):
        return jnp.where(lo_half, m[:, j0:j0 + 1], m[:, j0 + 1:j0 + 2])

    hpg = SSD_HEADS // SSD_GROUPS
    y_parts = []
    for g in range(SSD_GROUPS):
        b_g = xbc[:, SSD_INNER + g * SSD_STATE:SSD_INNER + (g + 1) * SSD_STATE].astype(BF16)
        c0 = SSD_INNER + SSD_GROUPS * SSD_STATE + g * SSD_STATE
        c_g = xbc[:, c0:c0 + SSD_STATE].astype(BF16)
        cb = _dot_nt(c_g, b_g)
        for m in range(hpg // 2):
            q = g * (hpg // 2) + m
            j0 = 2 * q
            xs_p = xs[:, q * LANES:(q + 1) * LANES]
            xd = xs_p * pair_cols(dt, j0)
            xd_b = xd.astype(BF16)
            ys = []
            for e in range(2):
                j = j0 + e
                diff = cs[:, j:j + 1] - cs_t[j:j + 1, :]
                lm = jnp.where(causal, jnp.exp(jnp.where(causal, diff, 0.0)), 0.0)
                ys.append(_dot((cb * lm).astype(BF16), xd_b))
            y_diag = jnp.where(lo_half, ys[0], ys[1])
            hp = h_ref[i, q]
            cs_p = pair_cols(cs, j0)
            y_off = _dot_nt(c_g, hp.astype(BF16)) * jnp.exp(cs_p)
            tot_p = jnp.where(lo_half[0:1, :], total[:, j0:j0 + 1], total[:, j0 + 1:j0 + 2])
            dte = jnp.exp(tot_p - cs_p)
            contrib = _dot_tn((xd * dte).astype(BF16), b_g)
            decay = jnp.where(row_lo, etot[:, j0:j0 + 1], etot[:, j0 + 1:j0 + 2])
            h_ref[i, q] = hp * decay + contrib
            y_parts.append(y_diag + y_off + dsk_ref[:, q * LANES:(q + 1) * LANES] * xs_p)
    y = jnp.concatenate(y_parts, axis=1) * _silu(z)
    gw = SSD_INNER // SSD_GROUPS
    outs = [_rms(y[:, g * gw:(g + 1) * gw]) * ng_ref[:, g * gw:(g + 1) * gw] for g in range(SSD_GROUPS)]
    return jnp.concatenate(outs, axis=1).astype(BF16)


def _ssd(x, tail0, h0, p, t_valid, L, nb):
    bsz = h0.shape[0]
    n, d = x.shape
    nch = n // bsz // L
    assert nb == 1 or nch == 1
    kern = functools.partial(_ssd_kernel, t_valid=t_valid, L=L, nb=nb)
    return pl.pallas_call(
        kern,
        grid=(bsz // nb, nch),
        in_specs=[pl.BlockSpec((nb * L, d), lambda b, c: (b * nch + c, 0)), _full((1, d)),
                  _resident((d, SSD_IN_PAD)),
                  pl.BlockSpec((nb, SUBLANES, SSD_CONV_DIM), lambda b, c: (b, 0, 0)),
                  pl.BlockSpec((nb, SSD_HEADS // 2, LANES, SSD_STATE), lambda b, c: (b, 0, 0, 0)),
                  _full((SSD_CONV, SSD_CONV_DIM)), _full((1, SSD_CONV_DIM)),
                  _full((1, LANES)), _full((1, LANES)), _full((1, SSD_INNER)), _full((1, SSD_INNER))],
        out_specs=[pl.BlockSpec((nb * L, SSD_INNER), lambda b, c: (b * nch + c, 0)),
                   pl.BlockSpec((nb, SUBLANES, SSD_CONV_DIM), lambda b, c: (b, 0, 0)),
                   pl.BlockSpec((nb, SSD_HEADS // 2, LANES, SSD_STATE), lambda b, c: (b, 0, 0, 0))],
        out_shape=[jax.ShapeDtypeStruct((n, SSD_INNER), BF16),
                   jax.ShapeDtypeStruct((bsz, SUBLANES, SSD_CONV_DIM), F32),
                   jax.ShapeDtypeStruct((bsz, SSD_HEADS // 2, LANES, SSD_STATE), F32)],
        scratch_shapes=[pltpu.VMEM((nb, SUBLANES + L, SSD_CONV_DIM), F32)],
        compiler_params=_cparams("parallel", "arbitrary"),
    )(x, p["norm_mix_g"], p["w_ssd"], tail0, h0, p["ssd_conv_w"], p["ssd_conv_b"], p["ssd_dt_bias"],
      p["ssd_a_log"], p["ssd_d"], p["ssd_norm_g"])


def _seg_sum_bcast(x, seg2, seg_t2):
    s16 = _dot(_split_hi_lo(x), seg2)
    return _dot(_split_hi_lo(s16), seg_t2)


def _rwkv_pre_kernel(x_ref, g_in_ref, w_in_ref, tail0_ref, mu_ref, w0_ref, wup_ref, a0_ref, aup_ref, gup_ref, kk_ref,
                     ka_ref, seg2_ref, segt2_ref,
                     r_ref, w_ref, k_ref, v_ref, al_ref, be_ref, g_ref, tail_ref, buf, *, tm, t_valid, nb):
    j = pl.program_id(1)

    @pl.when(j == 0)
    def _():
        buf[:, 0:SUBLANES, :] = tail0_ref[...]

    u = _in_proj(x_ref, g_in_ref, w_in_ref)
    prevs = []
    for i in range(nb):
        buf[i, SUBLANES:SUBLANES + tm, :] = u[i * tm:(i + 1) * tm]
        prevs.append(buf[i, pl.ds(SUBLANES - 1, tm), :])
        tail = buf[i, pl.ds(t_valid, SUBLANES), :]
        tail_ref[i] = tail
        buf[i, 0:SUBLANES, :] = tail
    prev = jnp.concatenate(prevs, axis=0) if nb > 1 else prevs[0]

    d = RWKV_DIM
    f = u + mu_ref[...] * (prev - u)
    r, k, v = f[:, :d], f[:, d:2 * d], f[:, 2 * d:3 * d]
    lo = f[:, 3 * d:3 * d + LANES]
    glo = f[:, 3 * d + LANES:]
    ww = w0_ref[...] + _dot(jnp.tanh(lo).astype(BF16), wup_ref[...])
    w_log = -_softplus(-ww) - 0.5
    a = _sigmoid(a0_ref[...] + _dot(lo.astype(BF16), aup_ref[...]))
    kk = k * kk_ref[...]
    ssb = _seg_sum_bcast(kk * kk, seg2_ref[...], segt2_ref[...])
    kkn = kk / jnp.maximum(jnp.sqrt(ssb), 1e-12)
    r_ref[...] = r
    w_ref[...] = -jnp.exp(w_log)
    k_ref[...] = k * (1.0 + (a - 1.0) * ka_ref[...])
    v_ref[...] = v
    al_ref[...] = -kkn
    be_ref[...] = kkn * a
    g_ref[...] = _dot(_sigmoid(glo).astype(BF16), gup_ref[...])


def _rwkv_pre(x, tail0, p, tm, t_valid, nb):
    bsz = tail0.shape[0]
    n = x.shape[0]
    nt = n // bsz // tm
    assert nb == 1 or nt == 1
    d = RWKV_DIM
    kern = functools.partial(_rwkv_pre_kernel, tm=tm, t_valid=t_valid, nb=nb)
    row = pl.BlockSpec((nb * tm, d), lambda b, j: (b * nt + j, 0))
    return pl.pallas_call(
        kern,
        grid=(bsz // nb, nt),
        in_specs=[row, _full((1, D_MODEL)), _resident((D_MODEL, RWKV_IN)),
                  pl.BlockSpec((nb, SUBLANES, RWKV_IN), lambda b, j: (b, 0, 0)),
                  _full((1, RWKV_IN)), _full((1, d)), _full((LANES, d)), _full((1, d)),
                  _full((LANES, d)), _full((LANES, d)), _full((1, d)), _full((1, d)),
                  _full((2 * d, LANES)), _full((2 * LANES, d))],
        out_specs=[row] * 7 + [pl.BlockSpec((nb, SUBLANES, RWKV_IN), lambda b, j: (b, 0, 0))],
        out_shape=[jax.ShapeDtypeStruct((n, d), F32)] * 7
                  + [jax.ShapeDtypeStruct((bsz, SUBLANES, RWKV_IN), F32)],
        scratch_shapes=[pltpu.VMEM((nb, SUBLANES + tm, RWKV_IN), F32)],
        compiler_params=_cparams("parallel", "arbitrary"),
    )(x, p["norm_mix_g"], p["w_rwkv"], tail0, p["rwkv_shift_mu"], p["rwkv_w0"], p["rwkv_w_up"], p["rwkv_a0"],
      p["rwkv_a_up"],
      p["rwkv_g_up"], p["rwkv_k_k"], p["rwkv_k_a"], p["seg2"], p["seg_t2"])


RWKV_CHUNK = 64
RWKV_SOLVE_BLOCK = 16
RWKV_PAIRS = RWKV_HEADS // 2


def _mm(a, b):
    return _dot(a.astype(BF16), b.astype(BF16))


def _mm3(a, b):
    ah, al = _hi_lo(a)
    bh, bl = _hi_lo(b)
    m = a.shape[0]
    both = _dot(jnp.concatenate([ah, al], axis=0), bh)
    return both[:m] + both[m:] + _dot(ah, bl)


def _mm2(a, b):
    ah, al = _hi_lo(a)
    m = a.shape[0]
    both = _dot(jnp.concatenate([ah, al], axis=0), b.astype(BF16))
    return both[:m] + both[m:]


def _rwkv_chunk_kernel(r_ref, lw_ref, k_ref, v_ref, al_ref, be_ref, s0_ref, y_ref, s_ref,
                       *, nchunks, single_block):
    c = pl.program_id(1)

    @pl.when(c == 0)
    def _():
        s_ref[...] = s0_ref[...]

    C, HD = RWKV_CHUNK, RWKV_HEAD_DIM
    r2 = lax.broadcasted_iota(jnp.int32, (LANES, LANES), 0)
    c2 = lax.broadcasted_iota(jnp.int32, (LANES, LANES), 1)
    eye = jnp.where(r2 == c2, 1.0, 0.0).astype(F32)
    diag_blk = (r2 // RWKV_SOLVE_BLOCK) == (c2 // RWKV_SOLVE_BLOCK)
    same_head = (r2 // HD) == (c2 // HD)
    tri = jnp.where(lax.broadcasted_iota(jnp.int32, (C, C), 0) >= lax.broadcasted_iota(jnp.int32, (C, C), 1),
                    1.0, 0.0).astype(BF16)
    t_i = lax.broadcasted_iota(jnp.int32, (C, LANES), 0)
    lane = lax.broadcasted_iota(jnp.int32, (C, LANES), 1)
    lo_half = lane < HD
    hi_half = lane >= HD
    strict = (lane % HD) < t_i
    incl = (lane % HD) <= t_i
    zeros = jnp.zeros((C, LANES), F32)

    def body(ci, carry):
        rows = pl.ds(pl.multiple_of(ci * C, C), C)
        lw_all = lw_ref[rows, :]
        lw_hi = lw_all.astype(BF16)
        lw_mid, lw_lo = _hi_lo(lw_all - lw_hi.astype(F32))
        cum_all = _dot(tri, lw_hi) + (_dot(tri, lw_mid) + _dot(tri, lw_lo))
        pairs = range(RWKV_PAIRS)
        sls = [slice(p * LANES, (p + 1) * LANES) for p in pairs]
        cum = [cum_all[:, sl] for sl in sls]
        be = [be_ref[rows, sl] for sl in sls]
        kk = [k_ref[rows, sl] for sl in sls]
        vv = [v_ref[rows, sl] for sl in sls]
        g_inv = [jnp.exp(-cum[p]) for p in pairs]
        ar = [jnp.concatenate([al_ref[rows, sls[p]] * jnp.exp(cum[p] - lw_ref[rows, sls[p]]),
                               r_ref[rows, sls[p]] * jnp.exp(cum[p])], axis=0) for p in pairs]
        ar_b = [ar[p].astype(BF16) for p in pairs]
        bt = [be[p] * g_inv[p] for p in pairs]
        kt = [kk[p] * g_inv[p] for p in pairs]
        x01 = [_dot_nt(ar_b[p], jnp.concatenate(
            [jnp.where(lo_half, bt[p], 0.0), jnp.where(lo_half, kt[p], 0.0),
             jnp.where(hi_half, kt[p], 0.0), jnp.where(hi_half, bt[p], 0.0)], axis=0).astype(BF16)) for p in pairs]
        x0 = [x01[p][:, :LANES] for p in pairs]
        x1 = [x01[p][:, LANES:] for p in pairs]
        nb = [jnp.concatenate([jnp.where(lo_half & strict, x0[p][:C], 0.0),
                               jnp.where(hi_half & strict, x1[p][:C], 0.0)], axis=0) for p in pairs]
        sbd = [s_ref[p] for p in pairs]

        def read_state(p):
            s_hi, s_lo = _hi_lo(sbd[p])
            return _dot_nt(jnp.concatenate([ar_b[p], ar_b[p]], axis=1),
                           jnp.concatenate([s_hi, s_lo], axis=1))

        ars = [read_state(p) for p in pairs]

        def strict_ak(p):
            lhs = jnp.concatenate([jnp.where(hi_half & strict, x0[p][:C], 0.0),
                                   jnp.where(lo_half & strict, x1[p][:C], 0.0)], axis=0)
            rhs_w = jnp.concatenate([jnp.concatenate([zeros, vv[p]], axis=0),
                                     jnp.concatenate([vv[p], zeros], axis=0)], axis=1)
            both = _mm(lhs, rhs_w)
            return jnp.where(lo_half, both[:C, :LANES], both[C:, LANES:])

        rhs = [ars[p][:C] + strict_ak(p) for p in pairs]
        rhs2 = [jnp.concatenate([rhs[p], rhs[p]], axis=0) for p in pairs]
        nd = [jnp.where(diag_blk, nb[p], 0.0) for p in pairs]
        loff = [nb[p] - nd[p] for p in pairs]
        pm = [eye + nd[p] for p in pairs]
        wide = lambda a_, b_: jnp.concatenate([a_, b_], axis=1)
        n2 = [_mm3(nd[p], nd[p]) for p in pairs]
        t2 = [_mm3(n2[p], wide(pm[p], n2[p])) for p in pairs]
        pm = [pm[p] + t2[p][:, :LANES] for p in pairs]
        t4 = [_mm3(t2[p][:, LANES:], wide(pm[p], t2[p][:, LANES:])) for p in pairs]
        pm = [pm[p] + t4[p][:, :LANES] for p in pairs]
        td = [pm[p] + _mm3(t4[p][:, LANES:], pm[p]) for p in pairs]
        if single_block:
            u2 = [_mm2(td[p], rhs2[p]) for p in pairs]
        else:
            t1 = [_mm2(td[p], wide(loff[p], rhs2[p])) for p in pairs]
            m1 = [t1[p][:, :LANES] for p in pairs]
            x_a = [t1[p][:, LANES:] for p in pairs]
            m2 = [_mm(m1[p], m1[p]) for p in pairs]
            x_b = [x_a[p] + _mm(m2[p], x_a[p]) for p in pairs]
            u2 = [x_b[p] + _mm(m1[p], x_b[p]) for p in pairs]
        up = [jnp.where(lo_half, u2[p][:C], u2[p][C:]) for p in pairs]
        for p in pairs:
            uv = jnp.concatenate([up[p], vv[p]], axis=0)
            vu = jnp.concatenate([vv[p], up[p]], axis=0)
            both = _mm(jnp.concatenate([jnp.where(incl, x0[p][C:], 0.0), jnp.where(incl, x1[p][C:], 0.0)], axis=0),
                       jnp.concatenate([uv, vu], axis=1))
            y_ref[rows, sls[p]] = ars[p][C:] + jnp.where(lo_half, both[:C, :LANES], both[C:, LANES:])
            cum_c = cum[p][C - 1:C, :]
            g_end = jnp.exp(cum_c - cum[p])
            bkh = jnp.concatenate([be[p] * g_end, kk[p] * g_end], axis=0)
            upd = _dot_tn(uv.astype(BF16), bkh.astype(BF16))
            s_ref[p] = sbd[p] * jnp.exp(cum_c) + jnp.where(same_head, upd, 0.0)
        return carry

    lax.fori_loop(0, nchunks, body, 0)


def _rwkv_scan(r, lw, k, v, al, be, s0, bsz, tc, single_block):
    n, d = r.shape
    t_total = n // bsz
    nt = t_total // tc
    kern = functools.partial(_rwkv_chunk_kernel, nchunks=tc // RWKV_CHUNK, single_block=single_block)
    row = pl.BlockSpec((tc, d), lambda b, c: (b * nt + c, 0))
    st = pl.BlockSpec((None, RWKV_PAIRS, LANES, LANES), lambda b, c: (b, 0, 0, 0))
    return pl.pallas_call(
        kern,
        grid=(bsz, nt),
        in_specs=[row] * 6 + [st],
        out_specs=[row, st],
        out_shape=[jax.ShapeDtypeStruct((n, d), F32),
                   jax.ShapeDtypeStruct((bsz, RWKV_PAIRS, LANES, LANES), F32)],
        compiler_params=_cparams("parallel", "arbitrary"),
    )(r, lw, k, v, al, be, s0)


def _rwkv_epilogue(y, r, k, v, g, lng, lnb, rk, seg2, seg_t2):
    inv = 1.0 / RWKV_HEAD_DIM
    dlt = y - _seg_sum_bcast(y, seg2, seg_t2) * inv
    var = _seg_sum_bcast(dlt * dlt, seg2, seg_t2) * inv
    yn = dlt * lax.rsqrt(var + RWKV_LN_EPS) * lng + lnb
    bonus = _seg_sum_bcast(r * k * rk, seg2, seg_t2) * v
    return ((yn + bonus) * g).astype(BF16)


def _mla_prep_kernel(x_ref, g_in_ref, w_in_ref, cos_ref, s1_ref, s2_ref, qag_ref, wqn_ref, wqr_ref, kvg_ref,
                     wkb_ref, wvb_ref,
                     gqn_ref, gqr_ref, gkn_ref, gkr_ref, q_ref, k_ref, v_ref, ckv_ref, kpe_ref):
    u = _in_proj(x_ref, g_in_ref, w_in_ref)
    cq = u[:, :MLA_Q_RANK]
    ckv_raw = u[:, MLA_Q_RANK:MLA_Q_RANK + MLA_KV_RANK]
    kr_raw = u[:, MLA_Q_RANK + MLA_KV_RANK:]
    cos, s1, s2 = cos_ref[...], s1_ref[...], s2_ref[...]

    def rope(blk):
        return (blk * cos + pltpu.roll(blk, LANES - MLA_ROPE // 2, 1) * s1
                + pltpu.roll(blk, MLA_ROPE // 2, 1) * s2)

    cqn = (_rms(cq) * qag_ref[...]).astype(BF16)
    qn = _dot(cqn, wqn_ref[...])
    qr = _dot(cqn, wqr_ref[...])
    ckv = _rms(ckv_raw) * kvg_ref[...]
    ckv_ref[...] = ckv
    kpe = rope(kr_raw)
    kpe_ref[...] = kpe
    ckb = ckv.astype(BF16)
    kn = _dot(ckb, wkb_ref[...])
    v_ref[...] = _dot(ckb, wvb_ref[...]).astype(BF16)
    ss_kpe = jnp.sum(kpe * kpe, axis=-1, keepdims=True)
    gqn, gqr, gkn, gkr = gqn_ref[...], gqr_ref[...], gkn_ref[...], gkr_ref[...]
    for h in range(MLA_HEADS):
        sl = slice(h * MLA_NOPE, (h + 1) * MLA_NOPE)
        o0 = h * MLA_QK_PAD
        qn_h = qn[:, sl]
        qr_h = rope(qr[:, sl])
        ss = jnp.sum(qn_h * qn_h, axis=-1, keepdims=True) + jnp.sum(qr_h * qr_h, axis=-1, keepdims=True)
        rn = lax.rsqrt(ss * (1.0 / MLA_QK) + NORM_EPS) * MLA_SCALE
        q_ref[:, o0:o0 + MLA_NOPE] = (qn_h * rn * gqn).astype(q_ref.dtype)
        q_ref[:, o0 + MLA_NOPE:o0 + MLA_QK_PAD] = (qr_h * rn * gqr).astype(q_ref.dtype)
        kn_h = kn[:, sl]
        ssk = jnp.sum(kn_h * kn_h, axis=-1, keepdims=True) + ss_kpe
        rnk = lax.rsqrt(ssk * (1.0 / MLA_QK) + NORM_EPS)
        k_ref[:, o0:o0 + MLA_NOPE] = (kn_h * rnk * gkn).astype(BF16)
        k_ref[:, o0 + MLA_NOPE:o0 + MLA_QK_PAD] = (kpe * rnk * gkr).astype(BF16)


def _mla_prep(x, rope_tabs, p, tm, q_dtype):
    n, d = x.shape
    ttab = rope_tabs[0].shape[0]
    ntab = ttab // tm
    hq = MLA_HEADS * MLA_QK_PAD
    hv = MLA_HEADS * MLA_V
    tab = pl.BlockSpec((tm, LANES), lambda i: (i % ntab, 0))
    return pl.pallas_call(
        _mla_prep_kernel,
        grid=(n // tm,),
        in_specs=[pl.BlockSpec((tm, d), lambda i: (i, 0)), _full((1, d)), _full((d, MLA_IN_PAD)), tab, tab, tab,
                  _full((1, MLA_Q_RANK)), _full((MLA_Q_RANK, hv)), _full((MLA_Q_RANK, hv)),
                  _full((1, MLA_KV_RANK)), _full((MLA_KV_RANK, hv)), _full((MLA_KV_RANK, hv)),
                  _full((1, LANES)), _full((1, LANES)), _full((1, LANES)), _full((1, LANES))],
        out_specs=[pl.BlockSpec((tm, hq), lambda i: (i, 0)), pl.BlockSpec((tm, hq), lambda i: (i, 0)),
                   pl.BlockSpec((tm, hv), lambda i: (i, 0)),
                   pl.BlockSpec((tm, MLA_KV_RANK), lambda i: (i, 0)),
                   pl.BlockSpec((tm, LANES), lambda i: (i, 0))],
        out_shape=[jax.ShapeDtypeStruct((n, hq), q_dtype), jax.ShapeDtypeStruct((n, hq), BF16),
                   jax.ShapeDtypeStruct((n, hv), BF16), jax.ShapeDtypeStruct((n, MLA_KV_RANK), F32),
                   jax.ShapeDtypeStruct((n, LANES), F32)],
        compiler_params=_cparams("parallel"),
    )(x, p["norm_mix_g"], p["w_mla"], *rope_tabs, p["mla_q_a_g"], p["wq_nope"], p["wq_rope"], p["mla_kv_a_g"], p["w_kb"], p["w_vb"],
      p["gq_nope"], p["gq_rope"], p["gk_nope"], p["gk_rope"])


FLASH_HEAD_GROUP = 2


def _flash_kernel(q_ref, k_ref, v_ref, o_ref, m_sc, acc_sc, *, tq):
    qi = pl.program_id(1)
    ki = pl.program_id(2)
    nblk = tq // LANES

    @pl.when(ki == 0)
    def _():
        m_sc[...] = jnp.full_like(m_sc, NEG_BIG)
        acc_sc[...] = jnp.zeros_like(acc_sc)

    ones = jnp.ones((tq, LANES), BF16)

    def update(diagonal):
        for h0 in range(0, MLA_HEADS, FLASH_HEAD_GROUP):
            heads = range(h0, h0 + FLASH_HEAD_GROUP)
            s = {h: _dot_nt(q_ref[:, h * MLA_QK_PAD:(h + 1) * MLA_QK_PAD],
                            k_ref[:, h * MLA_QK_PAD:(h + 1) * MLA_QK_PAD]) for h in heads}
            if diagonal:
                row = lax.broadcasted_iota(jnp.int32, (tq, tq), 0)
                col = lax.broadcasted_iota(jnp.int32, (tq, tq), 1)
                s = {h: jnp.where(col <= row, s[h], NEG_BIG) for h in heads}
            m_prev = {h: m_sc[h] for h in heads}
            m_new = {h: jnp.maximum(m_prev[h], jnp.max(s[h], axis=-1, keepdims=True)) for h in heads}
            pr = {h: jnp.concatenate([jnp.exp(s[h][:, j * LANES:(j + 1) * LANES] - m_new[h]) for j in range(nblk)],
                                     axis=1).astype(BF16) for h in heads}
            corr = {h: jnp.exp(m_prev[h] - m_new[h]) for h in heads}
            for h in heads:
                v_ext = jnp.concatenate([v_ref[:, h * MLA_V:(h + 1) * MLA_V], ones], axis=1)
                acc_sc[h] = acc_sc[h] * jnp.concatenate([corr[h], corr[h]], axis=1) + _dot(pr[h], v_ext)
                m_sc[h] = m_new[h]

    @pl.when(ki < qi)
    def _():
        update(False)

    @pl.when(ki == qi)
    def _():
        update(True)
        for h in range(MLA_HEADS):
            acc = acc_sc[h]
            o_ref[:, h * MLA_V:(h + 1) * MLA_V] = (acc[:, :MLA_V] / acc[:, MLA_V:]).astype(BF16)


def _flash(q, k, v, bsz, tq):
    n = q.shape[0]
    nq = n // bsz // tq
    hq = MLA_HEADS * MLA_QK_PAD
    hv = MLA_HEADS * MLA_V
    kern = functools.partial(_flash_kernel, tq=tq)
    return pl.pallas_call(
        kern,
        grid=(bsz, nq, nq),
        in_specs=[pl.BlockSpec((tq, hq), lambda b, i, j: (b * nq + i, 0)),
                  pl.BlockSpec((tq, hq), lambda b, i, j: (b * nq + jnp.minimum(i, j), 0)),
                  pl.BlockSpec((tq, hv), lambda b, i, j: (b * nq + jnp.minimum(i, j), 0))],
        out_specs=pl.BlockSpec((tq, hv), lambda b, i, j: (b * nq + i, 0)),
        out_shape=jax.ShapeDtypeStruct((n, hv), BF16),
        scratch_shapes=[pltpu.VMEM((MLA_HEADS, tq, LANES), F32),
                        pltpu.VMEM((MLA_HEADS, tq, MLA_V + LANES), F32)],
        compiler_params=_cparams("parallel", "parallel", "arbitrary"),
    )(q, k, v)


PAGES_PER_STEP = 32
PAGED_ROWS = MLA_HEADS * SAMPLE_T_PAD


def _paged_kernel(pt_ref, q_ref, *refs, t_new, npp):
    lat_refs = refs[:npp]
    rope_refs = refs[npp:2 * npp]
    (newc_ref, newr_ref, wkbt_ref, wvb_ref, gkn_ref, gkr_ref, o_ref,
     m_sc, l_sc, acc_sc, wq_sc, qr_sc) = refs[2 * npp:]
    step = pl.program_id(1)
    nkn = MLA_HEADS * MLA_NOPE

    @pl.when(step == 0)
    def _():
        m_sc[...] = jnp.full_like(m_sc, NEG_BIG)
        l_sc[...] = jnp.zeros_like(l_sc)
        acc_sc[...] = jnp.zeros_like(acc_sc)
        q = q_ref[...]
        gkn, gkr = gkn_ref[...], gkr_ref[...]
        wkbt = wkbt_ref[...]
        wq_sc[0:nkn, :] = wkbt
        for h in range(MLA_HEADS):
            o0 = h * MLA_QK_PAD
            rows = slice(h * SAMPLE_T_PAD, (h + 1) * SAMPLE_T_PAD)
            qn = (q[:, o0:o0 + MLA_NOPE] * gkn).astype(BF16)
            wq_sc[nkn + h * SAMPLE_T_PAD:nkn + (h + 1) * SAMPLE_T_PAD, :] = _dot(
                qn, wkbt[h * MLA_NOPE:(h + 1) * MLA_NOPE, :]).astype(BF16)
            qr_sc[rows, :] = (q[:, o0 + MLA_NOPE:o0 + MLA_QK_PAD] * gkr).astype(BF16)

    wq = wq_sc[...]
    qr = qr_sc[...]

    def scores(cbs, krs):
        n = len(cbs)
        big = [_dot_nt(wq, cbs[i]) for i in range(n)]
        rope = [_dot(qr, krs[i].astype(BF16)) for i in range(n)]
        ssr = [jnp.sum(krs[i] * krs[i], axis=0, keepdims=True) for i in range(n)]
        out = []
        for i in range(n):
            rn = []
            for h in range(MLA_HEADS):
                kn_h = big[i][h * MLA_NOPE:(h + 1) * MLA_NOPE]
                ss = jnp.sum(kn_h * kn_h, axis=0, keepdims=True) + ssr[i]
                rn.append(jnp.broadcast_to(lax.rsqrt(ss * (1.0 / MLA_QK) + NORM_EPS),
                                           (SAMPLE_T_PAD, ss.shape[1])))
            out.append((big[i][nkn:] + rope[i]) * jnp.concatenate(rn, axis=0))
        return out

    def softmax_update(s_list, cb_list):
        m_prev = m_sc[...]
        m_new = m_prev
        for s in s_list:
            m_new = jnp.maximum(m_new, jnp.max(s, axis=-1, keepdims=True))
        corr = jnp.exp(m_prev - m_new)
        l_new = l_sc[...] * corr
        acc = acc_sc[...] * corr
        for s, cb in zip(s_list, cb_list):
            pr = jnp.exp(s - m_new)
            l_new = l_new + jnp.sum(pr, axis=-1, keepdims=True)
            acc = acc + _dot(pr.astype(BF16), cb)
        l_sc[...] = l_new
        acc_sc[...] = acc
        m_sc[...] = m_new

    grp = 2 if npp % 2 == 0 else 1
    rope_pad = jnp.zeros((LANES - MLA_ROPE, grp * PAGE_SIZE), F32)
    cbs, krs = [], []
    for i in range(0, npp, grp):
        cbs.append(jnp.concatenate([lat_refs[i + j][...] for j in range(grp)], axis=0).astype(BF16))
        krs.append(jnp.concatenate([jnp.concatenate([rope_refs[i + j][...] for j in range(grp)], axis=1),
                                    rope_pad], axis=0))
    softmax_update(scores(cbs, krs), cbs)

    @pl.when(step == pl.num_programs(1) - 1)
    def _():
        cb = newc_ref[...].astype(BF16)
        s = scores([cb], [newr_ref[...]])[0]
        key = lax.broadcasted_iota(jnp.int32, s.shape, 1)
        qry = lax.broadcasted_iota(jnp.int32, s.shape, 0) % SAMPLE_T_PAD
        softmax_update([jnp.where((key <= qry) & (key < t_new), s, NEG_BIG)], [cb])
        o_lat = (acc_sc[...] / l_sc[...]).astype(BF16)
        wvb = wvb_ref[...]
        for h in range(MLA_HEADS):
            o_ref[:, h * MLA_V:(h + 1) * MLA_V] = _dot(
                o_lat[h * SAMPLE_T_PAD:(h + 1) * SAMPLE_T_PAD],
                wvb[:, h * MLA_V:(h + 1) * MLA_V])


def _paged_attention(page_table, q, lat_pool, rope_pool_t, layer, new_c, new_r_t, p, t_new):
    bsz, n_pages = page_table.shape
    npp = math.gcd(PAGES_PER_STEP, n_pages)
    nsteps = n_pages // npp
    hq = MLA_HEADS * MLA_QK_PAD
    hv = MLA_HEADS * MLA_V

    def lat_spec(i):
        return pl.BlockSpec((None, None, PAGE_SIZE, MLA_KV_RANK),
                            lambda b, s, pt: (layer, pt[b, s * npp + i], 0, 0))

    def rope_spec(i):
        return pl.BlockSpec((None, None, MLA_ROPE, PAGE_SIZE),
                            lambda b, s, pt: (layer, pt[b, s * npp + i], 0, 0))

    grid_spec = pltpu.PrefetchScalarGridSpec(
        num_scalar_prefetch=1,
        grid=(bsz, nsteps),
        in_specs=[pl.BlockSpec((SAMPLE_T_PAD, hq), lambda b, s, pt: (b, 0))]
                 + [lat_spec(i) for i in range(npp)] + [rope_spec(i) for i in range(npp)]
                 + [pl.BlockSpec((None, PAGE_SIZE, MLA_KV_RANK), lambda b, s, pt: (b, 0, 0)),
                    pl.BlockSpec((None, LANES, PAGE_SIZE), lambda b, s, pt: (b, 0, 0)),
                    pl.BlockSpec((hv, MLA_KV_RANK), lambda b, s, pt: (0, 0)),
                    pl.BlockSpec((MLA_KV_RANK, hv), lambda b, s, pt: (0, 0)),
                    pl.BlockSpec((1, LANES), lambda b, s, pt: (0, 0)),
                    pl.BlockSpec((1, LANES), lambda b, s, pt: (0, 0))],
        out_specs=pl.BlockSpec((SAMPLE_T_PAD, hv), lambda b, s, pt: (b, 0)),
        scratch_shapes=[pltpu.VMEM((PAGED_ROWS, 1), F32),
                        pltpu.VMEM((PAGED_ROWS, 1), F32),
                        pltpu.VMEM((PAGED_ROWS, MLA_KV_RANK), F32),
                        pltpu.VMEM((hv + PAGED_ROWS, MLA_KV_RANK), BF16),
                        pltpu.VMEM((PAGED_ROWS, LANES), BF16)],
    )
    kern = functools.partial(_paged_kernel, t_new=t_new, npp=npp)
    return pl.pallas_call(
        kern,
        grid_spec=grid_spec,
        out_shape=jax.ShapeDtypeStruct((bsz * SAMPLE_T_PAD, hv), F32),
        compiler_params=_cparams("parallel", "arbitrary"),
    )(page_table, q, *([lat_pool] * npp), *([rope_pool_t] * npp), new_c, new_r_t,
      p["w_kb_t"], p["w_vb"], p["gk_nope"], p["gk_rope"])


def _merge_kernel(x_ref, g_ref, wg_ref, bg_ref, os_ref, om_ref, y_ref, r_ref, k_ref, v_ref, gg_ref,
                  lng_ref, lnb_ref, rk_ref, seg2_ref, segt2_ref, ws_ref, wr_ref, wm_ref, wo_ref, o_ref):
    h = (_rms(x_ref[...]) * g_ref[...]).astype(BF16)
    gl = _dot(h, wg_ref[...]) + bg_ref[...]
    o_rwkv = _rwkv_epilogue(y_ref[...], r_ref[...], k_ref[...], v_ref[...], gg_ref[...],
                            lng_ref[...], lnb_ref[...], rk_ref[...], seg2_ref[...], segt2_ref[...])
    d = D_MODEL
    merged = (_sigmoid(gl[:, :d]) * _dot(os_ref[...], ws_ref[...])
              + _sigmoid(gl[:, d:2 * d]) * _dot(o_rwkv, wr_ref[...])
              + _sigmoid(gl[:, 2 * d:]) * _dot(om_ref[...].astype(BF16), wm_ref[...]))
    o_ref[...] = x_ref[...] + _dot(merged.astype(BF16), wo_ref[...])


def _merge(x, o_ssd, o_mla, rwkv_parts, p, tm):
    n, d = x.shape
    row = pl.BlockSpec((tm, d), lambda i: (i, 0))
    return pl.pallas_call(
        _merge_kernel,
        grid=(n // tm,),
        in_specs=[row, _full((1, d)), _resident((d, N_BRANCH * d)), _full((1, N_BRANCH * d)), row, row]
                 + [row] * 5 + [_full((1, d))] * 3 + [_full((2 * d, LANES)), _full((2 * LANES, d))]
                 + [_resident((d, d))] * 4,
        out_specs=row,
        out_shape=jax.ShapeDtypeStruct((n, d), F32),
        compiler_params=_cparams("parallel"),
    )(x, p["norm_mix_g"], p["w_gate"], p["b_gate"], o_ssd, o_mla, *rwkv_parts,
      p["rwkv_ln_g"], p["rwkv_ln_b"], p["rwkv_r_k"], p["seg2"], p["seg_t2"],
      p["w_o_ssd"], p["w_o_rwkv"], p["w_o_mla"], p["w_out"])


def _ffn_kernel(x_ref, tail0_ref, g_ref, wup_ref, cw_ref, cb_ref, wdn_ref, o_ref, tail_ref, buf,
                *, tm, t_valid, nb):
    j = pl.program_id(1)

    @pl.when(j == 0)
    def _():
        buf[:, 0:SUBLANES, :] = tail0_ref[...]

    x = x_ref[...]
    h2 = (_rms(x) * g_ref[...]).astype(BF16)
    up = _dot(h2, wup_ref[...])
    acts = []
    for i in range(nb):
        buf[i, SUBLANES:SUBLANES + tm, :] = up[i * tm:(i + 1) * tm]
        conv = cb_ref[...]
        for c in range(FFN_CONV):
            conv = conv + buf[i, pl.ds(SUBLANES - (FFN_CONV - 1) + c, tm), :] * cw_ref[c:c + 1, :]
        tail = buf[i, pl.ds(t_valid, SUBLANES), :]
        tail_ref[i] = tail
        buf[i, 0:SUBLANES, :] = tail
        acts.append(_silu(conv[:, :D_FF]) * conv[:, D_FF:])
    act = (jnp.concatenate(acts, axis=0) if nb > 1 else acts[0]).astype(BF16)
    o_ref[...] = x + _dot(act, wdn_ref[...])


def _ffn(x, tail0, p, tm, t_valid, nb):
    bsz = tail0.shape[0]
    n, d = x.shape
    nt = n // bsz // tm
    assert nb == 1 or nt == 1
    kern = functools.partial(_ffn_kernel, tm=tm, t_valid=t_valid, nb=nb)
    return pl.pallas_call(
        kern,
        grid=(bsz // nb, nt),
        in_specs=[pl.BlockSpec((nb * tm, d), lambda b, j: (b * nt + j, 0)),
                  pl.BlockSpec((nb, SUBLANES, 2 * D_FF), lambda b, j: (b, 0, 0)),
                  _full((1, d)),
                  pl.BlockSpec((d, 2 * D_FF), lambda b, j: (0, 0), pipeline_mode=pl.Buffered(1)),
                  _full((FFN_CONV, 2 * D_FF)), _full((1, 2 * D_FF)),
                  pl.BlockSpec((D_FF, d), lambda b, j: (0, 0), pipeline_mode=pl.Buffered(1))],
        out_specs=[pl.BlockSpec((nb * tm, d), lambda b, j: (b * nt + j, 0)),
                   pl.BlockSpec((nb, SUBLANES, 2 * D_FF), lambda b, j: (b, 0, 0))],
        out_shape=[jax.ShapeDtypeStruct((n, d), F32),
                   jax.ShapeDtypeStruct((bsz, SUBLANES, 2 * D_FF), F32)],
        scratch_shapes=[pltpu.VMEM((nb, SUBLANES + tm, 2 * D_FF), F32)],
        compiler_params=_cparams("parallel", "arbitrary"),
    )(x, tail0, p["norm_ffn_g"], p["ffn_w_up"], p["ffn_conv_w"], p["ffn_conv_b"], p["ffn_w_down"])


def _pad_cols(w, width):
    return jnp.pad(w, ((0, 0), (0, width - w.shape[1])))


def _row(v):
    return v.reshape(1, -1)


def _prep_layer(i, w):
    d = RWKV_DIM
    w_in = w["w_in"][i]
    o_r = SSD_IN
    o_m = SSD_IN + RWKV_IN
    p = {}
    p["norm_mix_g"] = _row(w["norm_mix_g"][i])
    p["w_ssd"] = _pad_cols(w_in[:, :o_r], SSD_IN_PAD).astype(BF16)
    p["w_rwkv"] = w_in[:, o_r:o_m].astype(BF16)
    p["w_mla"] = _pad_cols(w_in[:, o_m:], MLA_IN_PAD).astype(BF16)
    p["w_gate"] = w["w_gate"][i].astype(BF16)
    p["b_gate"] = _row(w["b_gate"][i])
    p["ssd_conv_w"] = w["ssd_conv_w"][i]
    p["ssd_conv_b"] = _row(w["ssd_conv_b"][i])
    p["ssd_dt_bias"] = _pad_cols(_row(w["ssd_dt_bias"][i]), LANES)
    p["ssd_a_log"] = _pad_cols(_row(w["ssd_a_log"][i]), LANES)
    p["ssd_d"] = _row(jnp.repeat(w["ssd_d"][i], SSD_HEAD_DIM))
    p["ssd_norm_g"] = _row(w["ssd_norm_g"][i])
    p["rwkv_shift_mu"] = _row(w["rwkv_shift_mu"][i])
    p["rwkv_w0"] = _row(w["rwkv_w0"][i])
    zeros_lora = jnp.zeros((RWKV_DECAY_LORA, d), F32)
    p["rwkv_w_up"] = jnp.concatenate([w["rwkv_w_up"][i], zeros_lora], axis=0).astype(BF16)
    p["rwkv_a0"] = _row(w["rwkv_a0"][i])
    p["rwkv_a_up"] = jnp.concatenate([zeros_lora, w["rwkv_a_up"][i]], axis=0).astype(BF16)
    p["rwkv_g_up"] = w["rwkv_g_up"][i].astype(BF16)
    p["rwkv_k_k"] = _row(w["rwkv_k_k"][i])
    p["rwkv_k_a"] = _row(w["rwkv_k_a"][i])
    p["rwkv_r_k"] = _row(w["rwkv_r_k"][i])
    p["rwkv_ln_g"] = _row(w["rwkv_ln_g"][i])
    p["rwkv_ln_b"] = _row(w["rwkv_ln_b"][i])
    wq = w["mla_w_q_b"][i]
    hv = MLA_HEADS * MLA_NOPE
    p["mla_q_a_g"] = _row(w["mla_q_a_g"][i])
    p["wq_nope"] = wq[:, :, :MLA_NOPE].reshape(MLA_Q_RANK, hv).astype(BF16)
    p["wq_rope"] = jnp.pad(wq[:, :, MLA_NOPE:], ((0, 0), (0, 0), (0, MLA_NOPE - MLA_ROPE))
                           ).reshape(MLA_Q_RANK, hv).astype(BF16)
    p["mla_kv_a_g"] = _row(w["mla_kv_a_g"][i])
    p["w_kb"] = w["mla_w_kb"][i].reshape(MLA_KV_RANK, hv).astype(BF16)
    p["w_kb_t"] = p["w_kb"].T
    p["w_vb"] = w["mla_w_vb"][i].reshape(MLA_KV_RANK, MLA_HEADS * MLA_V).astype(BF16)
    gq, gk = w["mla_q_norm_g"][i], w["mla_k_norm_g"][i]
    p["gq_nope"] = _row(gq[:MLA_NOPE])
    p["gq_rope"] = _pad_cols(_row(gq[MLA_NOPE:]), LANES)
    p["gk_nope"] = _row(gk[:MLA_NOPE])
    p["gk_rope"] = _pad_cols(_row(gk[MLA_NOPE:]), LANES)
    for name in ("w_o_ssd", "w_o_rwkv", "w_o_mla", "w_out"):
        p[name] = w[name][i].astype(BF16)
    p["norm_ffn_g"] = _row(w["norm_ffn_g"][i])
    p["ffn_w_up"] = w["ffn_w_up"][i].astype(BF16)
    p["ffn_conv_w"] = w["ffn_conv_w"][i]
    p["ffn_conv_b"] = _row(w["ffn_conv_b"][i])
    p["ffn_w_down"] = w["ffn_w_down"][i].astype(BF16)
    return p


def _constants():
    ch = np.arange(RWKV_DIM)
    seg = (ch[:, None] // RWKV_HEAD_DIM == np.arange(LANES)[None, :]).astype(np.float32)
    seg2 = np.concatenate([seg, seg], axis=0)
    seg_t2 = np.concatenate([seg.T, seg.T], axis=0)
    return jnp.asarray(seg2, BF16), jnp.asarray(seg_t2, BF16)


def _rope_tables(pos):
    half = MLA_ROPE // 2
    inv = ROPE_BASE ** (-jnp.arange(half, dtype=F32) / half)
    ang = pos.astype(F32)[:, None] * inv[None, :]
    cos, sin = jnp.cos(ang), jnp.sin(ang)
    z = jnp.zeros_like(cos)
    z2 = jnp.zeros((pos.shape[0], LANES - MLA_ROPE), F32)
    return (jnp.concatenate([cos, cos, z2], axis=1),
            jnp.concatenate([-sin, z, z2], axis=1),
            jnp.concatenate([z, sin, z2], axis=1))


def _tail_block(state, rows):
    return jnp.pad(state, ((0, 0), (SUBLANES - rows, 0), (0, 0)))


def _pack_rwkv_state(s):
    b = s.shape[0]
    s5 = s.reshape(b, RWKV_PAIRS, 2, RWKV_HEAD_DIM, RWKV_HEAD_DIM)
    z = jnp.zeros_like(s5[:, :, 0])
    top = jnp.concatenate([s5[:, :, 0], z], axis=-1)
    bot = jnp.concatenate([z, s5[:, :, 1]], axis=-1)
    return jnp.concatenate([top, bot], axis=-2)


def _unpack_rwkv_state(s):
    b = s.shape[0]
    hd = RWKV_HEAD_DIM
    return jnp.stack([s[:, :, :hd, :hd], s[:, :, hd:, hd:]], axis=2).reshape(b, RWKV_HEADS, hd, hd)


def _layer(x, st, p, cfg, attend):
    bsz, t, t_valid, tm = cfg["bsz"], cfg["t"], cfg["t_valid"], cfg["tm"]
    n = x.shape[0]
    tmn = min(tm, n)

    if t % SSD_CHUNK:
        lc = SSD_SHORT_CHUNK
        x_pad = jnp.pad(x.reshape(bsz, t, D_MODEL), ((0, 0), (0, lc - t), (0, 0)))
        o_ssd, ssd_tail, ssd_h = _ssd(x_pad.reshape(bsz * lc, D_MODEL), st["ssd_tail"], st["ssd_h"], p, t_valid, lc,
                                      math.gcd(bsz, 8))
        o_ssd = o_ssd.reshape(bsz, lc, SSD_INNER)[:, :t].reshape(n, SSD_INNER)
    else:
        o_ssd, ssd_tail, ssd_h = _ssd(x, st["ssd_tail"], st["ssd_h"], p, SSD_CHUNK, SSD_CHUNK, 1)

    tm_r = min(tm, t)
    nb_r = math.gcd(bsz, 16) if t <= SAMPLE_T_PAD else 1
    r, lw, k2, v, al, be, gg, rwkv_tail = _rwkv_pre(x, st["rwkv_tail"], p, tm_r, min(t_valid, tm_r), nb_r)
    d = RWKV_DIM
    if t % RWKV_CHUNK:
        def chunk_pad(a_):
            a_ = a_.reshape(bsz, t, d)[:, :t_valid]
            return jnp.pad(a_, ((0, 0), (0, RWKV_CHUNK - t_valid), (0, 0))).reshape(bsz * RWKV_CHUNK, d)
        yy, rwkv_s = _rwkv_scan(*(chunk_pad(a_) for a_ in (r, lw, k2, v, al, be)), st["rwkv_s"], bsz, RWKV_CHUNK,
                                t_valid <= RWKV_SOLVE_BLOCK)
        yy = yy.reshape(bsz, RWKV_CHUNK, d)[:, :t].reshape(n, d)
    else:
        yy, rwkv_s = _rwkv_scan(r, lw, k2, v, al, be, st["rwkv_s"], bsz, min(t, cfg["tc_rwkv"]), False)

    q, k, vv, ckv, kpe = _mla_prep(x, cfg["rope"], p, tmn, cfg["q_dtype"])
    o_mla = attend(q, k, vv, ckv, kpe)

    x = _merge(x, o_ssd, o_mla, (yy, r, k2, v, gg), p, tmn)
    tm_f = min(cfg["tm_ffn"], t)
    x, ffn_tail = _ffn(x, st["ffn_tail"], p, tm_f, min(t_valid, tm_f), cfg["nb_ffn"])
    new_st = dict(ssd_tail=ssd_tail, ssd_h=ssd_h, rwkv_tail=rwkv_tail, rwkv_s=rwkv_s, ffn_tail=ffn_tail)
    return x, new_st, ckv, kpe


def kernel(x_prompt, x_sample, cache_kv_latent, cache_k_rope, page_table, state_ssm, state_ssm_conv,
           state_rwkv, state_rwkv_shift, state_ffn_conv, norm_mix_g, w_in, ssd_conv_w, ssd_conv_b,
           ssd_dt_bias, ssd_a_log, ssd_d, ssd_norm_g, rwkv_shift_mu, rwkv_w0, rwkv_w_up, rwkv_a0,
           rwkv_a_up, rwkv_g_up, rwkv_k_k, rwkv_k_a, rwkv_r_k, rwkv_ln_g, rwkv_ln_b, mla_q_a_g,
           mla_w_q_b, mla_kv_a_g, mla_w_kb, mla_w_vb, mla_q_norm_g, mla_k_norm_g, w_gate, b_gate,
           w_o_ssd, w_o_rwkv, w_o_mla, w_out, norm_ffn_g, ffn_w_up, ffn_conv_w, ffn_conv_b, ffn_w_down):
    w = dict(norm_mix_g=norm_mix_g, w_in=w_in, ssd_conv_w=ssd_conv_w, ssd_conv_b=ssd_conv_b,
             ssd_dt_bias=ssd_dt_bias, ssd_a_log=ssd_a_log, ssd_d=ssd_d, ssd_norm_g=ssd_norm_g,
             rwkv_shift_mu=rwkv_shift_mu, rwkv_w0=rwkv_w0, rwkv_w_up=rwkv_w_up, rwkv_a0=rwkv_a0,
             rwkv_a_up=rwkv_a_up, rwkv_g_up=rwkv_g_up, rwkv_k_k=rwkv_k_k, rwkv_k_a=rwkv_k_a,
             rwkv_r_k=rwkv_r_k, rwkv_ln_g=rwkv_ln_g, rwkv_ln_b=rwkv_ln_b, mla_q_a_g=mla_q_a_g,
             mla_w_q_b=mla_w_q_b, mla_kv_a_g=mla_kv_a_g, mla_w_kb=mla_w_kb, mla_w_vb=mla_w_vb,
             mla_q_norm_g=mla_q_norm_g, mla_k_norm_g=mla_k_norm_g, w_gate=w_gate, b_gate=b_gate,
             w_o_ssd=w_o_ssd, w_o_rwkv=w_o_rwkv, w_o_mla=w_o_mla, w_out=w_out, norm_ffn_g=norm_ffn_g,
             ffn_w_up=ffn_w_up, ffn_conv_w=ffn_conv_w, ffn_conv_b=ffn_conv_b, ffn_w_down=ffn_w_down)
    depth = w_in.shape[0]
    bp, tp, d = x_prompt.shape
    bs, ts, _ = x_sample.shape
    tsp = SAMPLE_T_PAD
    past_len = page_table.shape[1] * PAGE_SIZE
    seg2, seg_t2 = _constants()
    rope_pool_t = jnp.swapaxes(cache_k_rope, 2, 3)

    tm_p = min(256, tp)
    cfg_p = dict(bsz=bp, t=tp, t_valid=tp, tm=tm_p, tm_ffn=min(256, tp), nb_ffn=1, tc_rwkv=256,
                 rope=_rope_tables(jnp.arange(tp)), q_dtype=BF16)
    pos_s = past_len + jnp.arange(tsp)
    rope_s = tuple(jnp.tile(tb, (bs, 1)) for tb in _rope_tables(pos_s))
    cfg_s = dict(bsz=bs, t=tsp, t_valid=ts, tm=bs * tsp, tm_ffn=tsp, nb_ffn=math.gcd(bs, 16), tc_rwkv=RWKV_CHUNK, rope=rope_s,
                 q_dtype=F32)

    xp = x_prompt.reshape(bp * tp, d)
    xs = jnp.pad(x_sample, ((0, 0), (0, tsp - ts), (0, 0))).reshape(bs * tsp, d)

    zero_st = dict(ssd_tail=jnp.zeros((bp, SUBLANES, SSD_CONV_DIM), F32),
                   ssd_h=jnp.zeros((bp, SSD_HEADS // 2, LANES, SSD_STATE), F32),
                   rwkv_tail=jnp.zeros((bp, SUBLANES, RWKV_IN), F32),
                   rwkv_s=jnp.zeros((bp, RWKV_PAIRS, LANES, LANES), F32),
                   ffn_tail=jnp.zeros((bp, SUBLANES, 2 * D_FF), F32))

    new_p = [[] for _ in range(7)]
    new_s = [[] for _ in range(7)]
    for i in range(depth):
        p = _prep_layer(i, w)
        p["seg2"], p["seg_t2"] = seg2, seg_t2

        def prompt_attend(q, k, v, ckv, kpe):
            return _flash(q, k, v, bp, min(512, tp))

        xp, st_p, ckv_p, kpe_p = _layer(xp, zero_st, p, cfg_p, prompt_attend)

        st_in = dict(ssd_tail=_tail_block(state_ssm_conv[i], SSD_CONV - 1),
                     ssd_h=state_ssm[i].reshape(bs, SSD_HEADS // 2, LANES, SSD_STATE),
                     rwkv_tail=_tail_block(state_rwkv_shift[i][:, None, :], 1),
                     rwkv_s=_pack_rwkv_state(state_rwkv[i]),
                     ffn_tail=_tail_block(state_ffn_conv[i], FFN_CONV - 1))

        def sample_attend(q, k, v, ckv, kpe, i=i, p=p):
            new_c = jnp.pad(ckv.reshape(bs, tsp, MLA_KV_RANK), ((0, 0), (0, PAGE_SIZE - tsp), (0, 0)))
            new_r_t = jnp.pad(jnp.swapaxes(kpe.reshape(bs, tsp, LANES), 1, 2), ((0, 0), (0, 0), (0, PAGE_SIZE - tsp)))
            return _paged_attention(page_table, q, cache_kv_latent, rope_pool_t, i, new_c, new_r_t, p, ts)

        xs, st_s, ckv_s, kpe_s = _layer(xs, st_in, p, cfg_s, sample_attend)

        for lst, st, ckv, kpe, b_, t_, tv in ((new_p, st_p, ckv_p, kpe_p, bp, tp, tp),
                                              (new_s, st_s, ckv_s, kpe_s, bs, tsp, ts)):
            lst[0].append(ckv.reshape(b_, t_, MLA_KV_RANK)[:, :tv])
            lst[1].append(kpe.reshape(b_, t_, LANES)[:, :tv, :MLA_ROPE])
            lst[2].append(st["ssd_h"].reshape(b_, SSD_HEADS, SSD_HEAD_DIM, SSD_STATE))
            lst[3].append(st["ssd_tail"][:, SUBLANES - (SSD_CONV - 1):])
            lst[4].append(_unpack_rwkv_state(st["rwkv_s"]))
            lst[5].append(st["rwkv_tail"][:, SUBLANES - 1])
            lst[6].append(st["ffn_tail"][:, SUBLANES - (FFN_CONV - 1):])

    outs_p = [jnp.stack(v_, axis=0) for v_ in new_p]
    outs_s = [jnp.stack(v_, axis=0) for v_ in new_s]
    y_p = xp.reshape(bp, tp, d)
    y_s = xs.reshape(bs, tsp, d)[:, :ts]
    return (y_p, y_s, *outs_p, *outs_s)
```
